```python
import jax
import jax.numpy as jnp
from jax import lax
import numpy as np

D_MODEL = 4096
BATCH = 4
SEQ = 2048
DEPTH = 1
DEC_BATCH = 128
DEC_SEQ = 1
PAST_LEN = 16384
PAGE_SIZE = 128

D_A = D_MODEL // 2
HEAD_DIM = 64
H_A = D_A // HEAD_DIM
DECAY_RANK = 96
ICLR_RANK = 96
GATE_RANK = 256
SHIFT_W = 3 * D_A + DECAY_RANK + ICLR_RANK + GATE_RANK
POOL_W = D_MODEL // 2
POOL_GROUPS = 4
POOL_GROUP_W = POOL_W // POOL_GROUPS
POOL_WINDOWS = (2, 4, 8, 16)
POOL_BUF = 15
PROJ_W = SHIFT_W + POOL_W + 2 * D_MODEL
N_GROUPS = 4
EXPERTS_PER_GROUP = 8
N_EXPERTS = N_GROUPS * EXPERTS_PER_GROUP
TOP_K = 2
D_EXPERT = 1024
MOE_BLOCK = 128
RMS_EPS = 1e-6
GN_EPS = 6.4e-4
L2_EPS = 1e-12

kernel_name = 'rwkv7_pool_hmoe_decode_step'


def rmsnorm(x, g):
    xf = x.astype(jnp.float32)
    xf = xf * lax.rsqrt(jnp.mean(xf * xf, axis=-1, keepdims=True) + RMS_EPS)
    return xf.astype(x.dtype) * g


def token_shift(p, prev_row, mu):
    p_prev = jnp.concatenate([prev_row[:, None, :].astype(p.dtype), p[:, :-1]], axis=1)
    return p + (p_prev - p) * mu, p[:, -1]


def wkv7_recurrence(s0, r, w, k, v, kk, a):
    def step(S, inp):
        r_t, w_t, k_t, v_t, kk_t, a_t = inp
        sa = jnp.einsum('bhij,bhj->bhi', S, -kk_t)
        S = (S * w_t[:, :, None, :]
             + sa[..., None] * (kk_t * a_t)[:, :, None, :]
             + v_t[..., None] * k_t[:, :, None, :])
        return S, jnp.einsum('bhij,bhj->bhi', S, r_t)
    xs = (jnp.moveaxis(r, 1, 0), jnp.moveaxis(w, 1, 0), jnp.moveaxis(k, 1, 0),
          jnp.moveaxis(v, 1, 0), jnp.moveaxis(kk, 1, 0), jnp.moveaxis(a, 1, 0))
    s_last, ys = lax.scan(step, s0, xs)
    return jnp.moveaxis(ys, 0, 1), s_last


def rwkv7_branch(ps, s_wkv, w0, w_decay_up, a0, w_iclr_up, w_gate_up, k_k, k_a, r_k, ln_x_g, ln_x_b):
    B, T, _ = ps.shape
    cuts = [D_A, 2 * D_A, 3 * D_A, 3 * D_A + DECAY_RANK, 3 * D_A + DECAY_RANK + ICLR_RANK]
    r, k, v, wd, ad, gd = jnp.split(ps.astype(jnp.float32), cuts, axis=-1)
    log_w = -jax.nn.softplus(-(w0 + jnp.tanh(wd) @ w_decay_up)) - 0.5
    decay = jnp.exp(-jnp.exp(log_w))
    a = jax.nn.sigmoid(a0 + ad @ w_iclr_up)
    gate = jax.nn.sigmoid(gd) @ w_gate_up
    kk = k * k_k
    k = k * (1.0 + (a - 1.0) * k_a)
    def heads(t):
        return t.reshape(B, T, H_A, HEAD_DIM)
    r_h, w_h, k_h, v_h, a_h, kk_h = heads(r), heads(decay), heads(k), heads(v), heads(a), heads(kk)
    kk_h = kk_h / jnp.maximum(jnp.sqrt(jnp.sum(kk_h * kk_h, axis=-1, keepdims=True)), L2_EPS)
    y, s_new = wkv7_recurrence(s_wkv.astype(jnp.float32), r_h, w_h, k_h, v_h, kk_h, a_h)
    mu = jnp.mean(y, axis=-1, keepdims=True)
    var = jnp.mean(jnp.square(y - mu), axis=-1, keepdims=True)
    y = ((y - mu) * lax.rsqrt(var + GN_EPS)).reshape(B, T, D_A) * ln_x_g + ln_x_b
    bonus = (jnp.sum(r_h * k_h * r_k, axis=-1, keepdims=True) * v_h).reshape(B, T, D_A)
    out = (y + bonus) * gate
    return out.astype(ps.dtype), s_new.astype(s_wkv.dtype)


def pool_branch(u, prev, pos0, w_pool, pool_scale):
    B, T, _ = u.shape
    ext = jnp.concatenate([prev.astype(u.dtype), u], axis=1)
    ef = ext.astype(jnp.float32)
    cs = jnp.concatenate([jnp.zeros((B, 1, POOL_W), jnp.float32), jnp.cumsum(ef, axis=1)], axis=1)
    end = cs[:, POOL_BUF + 1:POOL_BUF + 1 + T]
    pos = pos0 + jnp.arange(T, dtype=jnp.int32)
    means = []
    for gi, win in enumerate(POOL_WINDOWS):
        c0, c1 = gi * POOL_GROUP_W, (gi + 1) * POOL_GROUP_W
        start = cs[:, POOL_BUF + 1 - win:POOL_BUF + 1 - win + T, c0:c1]
        cnt = jnp.minimum(pos + 1, win).astype(jnp.float32)[None, :, None]
        means.append((end[..., c0:c1] - start) / cnt)
    pooled = jnp.concatenate(means, axis=-1) - ef[:, POOL_BUF:]
    y = jnp.einsum('btgc,gcd->btgd', pooled.reshape(B, T, POOL_GROUPS, POOL_GROUP_W),
                   w_pool.astype(jnp.float32)).reshape(B, T, POOL_W) * pool_scale
    return y.astype(u.dtype), ext[:, -POOL_BUF:]


def moe_ffn(x, w_router_group, w_router_expert, w_exp_gate, w_exp_up, w_exp_down):
    B, T, D = x.shape
    n = B * T
    xt = x.reshape(n, D)
    p_group = jax.nn.softmax((xt @ w_router_group).astype(jnp.float32), axis=-1)
    g_sel = jnp.argmax(p_group, axis=-1)
    p_sel = jnp.take_along_axis(p_group, g_sel[:, None], axis=-1)
    le = (xt @ w_router_expert).astype(jnp.float32).reshape(n, N_GROUPS, EXPERTS_PER_GROUP)
    le_g = jnp.take_along_axis(le, g_sel[:, None, None], axis=1)[:, 0]
    top_v, top_i = lax.top_k(le_g, TOP_K)
    wts = p_sel * jax.nn.softmax(top_v, axis=-1)
    eid = (g_sel[:, None] * EXPERTS_PER_GROUP + top_i).reshape(-1).astype(jnp.int32)
    tok = jnp.repeat(jnp.arange(n, dtype=jnp.int32), TOP_K)
    wt = wts.reshape(-1)
    n_assign = n * TOP_K
    blk = min(MOE_BLOCK, max(8, n_assign // N_EXPERTS))
    n_blocks = -(-n_assign // blk) + N_EXPERTS
    order = jnp.argsort(eid)
    e_s, tok_s, wt_s = eid[order], tok[order], wt[order]
    counts = jnp.bincount(eid, length=N_EXPERTS)
    starts = jnp.cumsum(counts) - counts
    padded = (counts + blk - 1) // blk * blk
    p_ends = jnp.cumsum(padded)
    p_starts = p_ends - padded
    dest = p_starts[e_s] + (jnp.arange(n_assign, dtype=jnp.int32) - starts[e_s])
    xbuf = jnp.zeros((n_blocks * blk, D), x.dtype).at[dest].set(xt[tok_s])
    block_e = jnp.minimum(jnp.searchsorted(p_ends, jnp.arange(n_blocks, dtype=jnp.int32) * blk, side='right'),
                          N_EXPERTS - 1)

    def expert_block(args):
        xb, e = args
        return (jax.nn.silu(xb @ w_exp_gate[e]) * (xb @ w_exp_up[e])) @ w_exp_down[e]

    ybuf = lax.map(expert_block, (xbuf.reshape(n_blocks, blk, D), block_e)).reshape(n_blocks * blk, D)
    y = jax.ops.segment_sum(ybuf[dest] * wt_s[:, None].astype(x.dtype), tok_s, num_segments=n)
    return y.reshape(B, T, D)


def decoder_layer(x, s_wkv, s_shift, s_pool, pos0, norm1_g, w_in, mu_shift, w0, w_decay_up, a0,
                  w_iclr_up, w_gate_up, k_k, k_a, r_k, ln_x_g, ln_x_b, w_pool, pool_scale,
                  w_branch_a, w_branch_b, w_out, norm2_g, w_router_group, w_router_expert,
                  w_exp_gate, w_exp_up, w_exp_down):
    h = rmsnorm(x, norm1_g)
    p = h @ w_in
    p_shift, u, g_a, g_b = jnp.split(p, [SHIFT_W, SHIFT_W + POOL_W, SHIFT_W + POOL_W + D_MODEL], axis=-1)
    ps, new_shift = token_shift(p_shift, s_shift, mu_shift)
    ya, new_wkv = rwkv7_branch(ps, s_wkv, w0, w_decay_up, a0, w_iclr_up, w_gate_up,
                               k_k, k_a, r_k, ln_x_g, ln_x_b)
    yb, new_pool = pool_branch(u, s_pool, pos0, w_pool, pool_scale)
    merged = jax.nn.sigmoid(g_a) * (ya @ w_branch_a) + jax.nn.sigmoid(g_b) * (yb @ w_branch_b)
    x = x + merged @ w_out
    x = x + moe_ffn(rmsnorm(x, norm2_g), w_router_group, w_router_expert, w_exp_gate, w_exp_up, w_exp_down)
    return x, new_wkv, new_shift, new_pool


def setup_inputs(seed: int = 0) -> dict:
    key = jax.random.key(seed)
    ks = iter(jax.random.split(key, 40))
    L = DEPTH

    def nrm(shape, scale):
        return scale * jax.random.normal(next(ks), shape, jnp.float32)

    return {
        'x_prompt': nrm((BATCH, SEQ, D_MODEL), 1.0),
        'x_sample': nrm((DEC_BATCH, DEC_SEQ, D_MODEL), 1.0),
        'state_wkv': nrm((L, DEC_BATCH, H_A, HEAD_DIM, HEAD_DIM), 0.1),
        'state_shift': nrm((L, DEC_BATCH, SHIFT_W), 1.0),
        'state_pool': nrm((L, DEC_BATCH, POOL_BUF, POOL_W), 1.0),
        'norm1_g': 1.0 + nrm((L, D_MODEL), 0.05),
        'w_in': nrm((L, D_MODEL, PROJ_W), D_MODEL ** -0.5),
        'mu_shift': jax.random.uniform(next(ks), (L, SHIFT_W), jnp.float32, 0.2, 0.8),
        'w0': -1.0 + nrm((L, D_A), 0.5),
        'w_decay_up': nrm((L, DECAY_RANK, D_A), 0.3 * DECAY_RANK ** -0.5),
        'a0': nrm((L, D_A), 0.2),
        'w_iclr_up': nrm((L, ICLR_RANK, D_A), 0.3 * ICLR_RANK ** -0.5),
        'w_gate_up': nrm((L, GATE_RANK, D_A), GATE_RANK ** -0.5),
        'k_k': 0.85 + nrm((L, D_A), 0.05),
        'k_a': 1.0 + nrm((L, D_A), 0.05),
        'r_k': nrm((L, H_A, HEAD_DIM), 0.3),
        'ln_x_g': 1.0 + nrm((L, D_A), 0.05),
        'ln_x_b': nrm((L, D_A), 0.02),
        'w_pool': nrm((L, POOL_GROUPS, POOL_GROUP_W, POOL_GROUP_W), POOL_GROUP_W ** -0.5),
        'pool_scale': 1.0 + nrm((L, POOL_W), 0.1),
        'w_branch_a': nrm((L, D_A, D_MODEL), D_A ** -0.5),
        'w_branch_b': nrm((L, POOL_W, D_MODEL), POOL_W ** -0.5),
        'w_out': nrm((L, D_MODEL, D_MODEL), D_MODEL ** -0.5),
        'norm2_g': 1.0 + nrm((L, D_MODEL), 0.05),
        'w_router_group': nrm((L, D_MODEL, N_GROUPS), D_MODEL ** -0.5),
        'w_router_expert': nrm((L, D_MODEL, N_EXPERTS), D_MODEL ** -0.5),
        'w_exp_gate': nrm((L, N_EXPERTS, D_MODEL, D_EXPERT), D_MODEL ** -0.5),
        'w_exp_up': nrm((L, N_EXPERTS, D_MODEL, D_EXPERT), D_MODEL ** -0.5),
        'w_exp_down': nrm((L, N_EXPERTS, D_EXPERT, D_MODEL), D_EXPERT ** -0.5),
        'norm_f_g': 1.0 + nrm((D_MODEL,), 0.05),
    }


def reference(x_prompt, x_sample, state_wkv, state_shift, state_pool, norm1_g, w_in, mu_shift, w0,
              w_decay_up, a0, w_iclr_up, w_gate_up, k_k, k_a, r_k, ln_x_g, ln_x_b, w_pool, pool_scale,
              w_branch_a, w_branch_b, w_out, norm2_g, w_router_group, w_router_expert,
              w_exp_gate, w_exp_up, w_exp_down, norm_f_g):
    bp = x_prompt.shape[0]
    xp, xs = x_prompt, x_sample
    wkv_p, shift_p, pool_p, wkv_s, shift_s, pool_s = [], [], [], [], [], []
    for l in range(DEPTH):
        lw = (norm1_g[l], w_in[l], mu_shift[l], w0[l], w_decay_up[l], a0[l], w_iclr_up[l], w_gate_up[l],
              k_k[l], k_a[l], r_k[l], ln_x_g[l], ln_x_b[l], w_pool[l], pool_scale[l], w_branch_a[l],
              w_branch_b[l], w_out[l], norm2_g[l], w_router_group[l], w_router_expert[l],
              w_exp_gate[l], w_exp_up[l], w_exp_down[l])
        z_wkv = jnp.zeros((bp, H_A, HEAD_DIM, HEAD_DIM), state_wkv.dtype)
        z_shift = jnp.zeros((bp, SHIFT_W), state_shift.dtype)
        z_pool = jnp.zeros((bp, POOL_BUF, POOL_W), state_pool.dtype)
        xp, a1, a2, a3 = decoder_layer(xp, z_wkv, z_shift, z_pool, 0, *lw)
        xs, b1, b2, b3 = decoder_layer(xs, state_wkv[l], state_shift[l], state_pool[l], PAST_LEN, *lw)
        wkv_p.append(a1)
        shift_p.append(a2)
        pool_p.append(a3)
        wkv_s.append(b1)
        shift_s.append(b2)
        pool_s.append(b3)
    y_prompt = rmsnorm(xp, norm_f_g)
    y_sample = rmsnorm(xs, norm_f_g)
    new_wkv_prompt = jnp.stack(wkv_p)
    new_shift_prompt = jnp.stack(shift_p)
    new_pool_prompt = jnp.stack(pool_p)
    new_wkv_sample = jnp.stack(wkv_s)
    new_shift_sample = jnp.stack(shift_s)
    new_pool_sample = jnp.stack(pool_s)
    return (y_prompt, y_sample, new_wkv_prompt, new_shift_prompt, new_pool_prompt,
            new_wkv_sample, new_shift_sample, new_pool_sample)
```

```python
import functools
import math

import jax
import jax.numpy as jnp
from jax import lax
from jax.experimental import pallas as pl
from jax.experimental.pallas import tpu as pltpu

F32 = jnp.float32
BF16 = jnp.bfloat16

HEAD_DIM = 64
LANES = 128
DECAY_RANK = 96
ICLR_RANK = 96
GATE_RANK = 256
LORA_W = DECAY_RANK + ICLR_RANK + GATE_RANK
LORA_PAD = 512
POOL_WINDOWS = (2, 4, 8, 16)
POOL_BUF = 15
POOL_CARRY = 16
N_GROUPS = 4
EXPERTS_PER_GROUP = 8
N_EXPERTS = N_GROUPS * EXPERTS_PER_GROUP
TOP_K = 2
RMS_EPS = 1e-6
GN_EPS = 6.4e-4
L2_EPS = 1e-12
WKV_CHUNK = 64
VMEM_LIMIT = 56 * 1024 * 1024


def _cparams(sem, vmem=VMEM_LIMIT):
    return pltpu.CompilerParams(dimension_semantics=sem, vmem_limit_bytes=vmem)


def _tile(n, target, mult=8):
    best = None
    for t in range(mult, min(n, target) + 1, mult):
        if n % t == 0:
            best = t
    return best if best is not None else n


def _dot(a, b):
    return jnp.dot(a, b, preferred_element_type=F32)


def _sigmoid(x):
    return 1.0 / (1.0 + jnp.exp(-x))


def _norm1_kernel(xp_ref, xs_ref, g_ref, h_ref, xc_ref, *, n_prompt_tiles):
    i = pl.program_id(0)

    def body(x):
        xn = x * lax.rsqrt(jnp.mean(x * x, axis=-1, keepdims=True) + RMS_EPS)
        h_ref[...] = (xn * g_ref[...]).astype(BF16)
        xc_ref[...] = x

    @pl.when(i < n_prompt_tiles)
    def _():
        body(xp_ref[...])

    @pl.when(i >= n_prompt_tiles)
    def _():
        body(xs_ref[...])


def _norm1(xp, xs, g):
    n_p, d = xp.shape
    n_s = xs.shape[0]
    tr = _tile(n_s, 256)
    assert n_p % tr == 0
    npt, nst = n_p // tr, n_s // tr
    return pl.pallas_call(
        functools.partial(_norm1_kernel, n_prompt_tiles=npt),
        grid=(npt + nst,),
        in_specs=[pl.BlockSpec((tr, d), lambda i: (jnp.minimum(i, npt - 1), 0)),
                  pl.BlockSpec((tr, d), lambda i: (jnp.maximum(i - npt, 0), 0)),
                  pl.BlockSpec((1, d), lambda i: (0, 0))],
        out_specs=[pl.BlockSpec((tr, d), lambda i: (i, 0)),
                   pl.BlockSpec((tr, d), lambda i: (i, 0))],
        out_shape=[jax.ShapeDtypeStruct((n_p + n_s, d), BF16),
                   jax.ShapeDtypeStruct((n_p + n_s, d), F32)],
        compiler_params=_cparams(("arbitrary",)),
        name="norm1",
    )(xp, xs, g.reshape(1, d))


def _mm_kernel(a_ref, w_ref, o_ref):
    o_ref[...] = _dot(a_ref[...], w_ref[...].astype(BF16))


def _matmul(a, w, tm, tn):
    m, k = a.shape
    n = w.shape[1]
    return pl.pallas_call(
        _mm_kernel,
        grid=(m // tm, n // tn),
        in_specs=[pl.BlockSpec((tm, k), lambda i, j: (i, 0)),
                  pl.BlockSpec((k, tn), lambda i, j: (0, j))],
        out_specs=pl.BlockSpec((tm, tn), lambda i, j: (i, j)),
        out_shape=jax.ShapeDtypeStruct((m, n), F32),
        compiler_params=_cparams(("arbitrary", "arbitrary")),
        name="in_proj",
    )(a, w)


def _merge_kernel(ya_ref, yb_ref, wa_ref, wb_ref, ga_ref, gb_ref, o_ref):
    a = _dot(ya_ref[...], wa_ref[...].astype(BF16))
    b = _dot(yb_ref[...], wb_ref[...].astype(BF16))
    o_ref[...] = (_sigmoid(ga_ref[...]) * a + _sigmoid(gb_ref[...]) * b).astype(BF16)


def _merge(ya, yb, wa, wb, p, d, tm, tn):
    m, k = ya.shape
    ga0 = (2 * d) // tn
    gb0 = (3 * d) // tn
    return pl.pallas_call(
        _merge_kernel,
        grid=(m // tm, d // tn),
        in_specs=[pl.BlockSpec((tm, k), lambda i, j: (i, 0)),
                  pl.BlockSpec((tm, k), lambda i, j: (i, 0)),
                  pl.BlockSpec((k, tn), lambda i, j: (0, j)),
                  pl.BlockSpec((k, tn), lambda i, j: (0, j)),
                  pl.BlockSpec((tm, tn), lambda i, j: (i, ga0 + j)),
                  pl.BlockSpec((tm, tn), lambda i, j: (i, gb0 + j))],
        out_specs=pl.BlockSpec((tm, tn), lambda i, j: (i, j)),
        out_shape=jax.ShapeDtypeStruct((m, d), BF16),
        compiler_params=_cparams(("arbitrary", "arbitrary")),
        name="merge",
    )(ya, yb, wa, wb, p, p)


def _outproj_kernel(m_ref, w_ref, x_ref, o_ref):
    o_ref[...] = x_ref[...] + _dot(m_ref[...], w_ref[...].astype(BF16))


def _outproj(merged, w, x, tm, tn):
    m, k = merged.shape
    n = w.shape[1]
    return pl.pallas_call(
        _outproj_kernel,
        grid=(m // tm, n // tn),
        in_specs=[pl.BlockSpec((tm, k), lambda i, j: (i, 0)),
                  pl.BlockSpec((k, tn), lambda i, j: (0, j)),
                  pl.BlockSpec((tm, tn), lambda i, j: (i, j))],
        out_specs=pl.BlockSpec((tm, tn), lambda i, j: (i, j)),
        out_shape=jax.ShapeDtypeStruct((m, n), F32),
        compiler_params=_cparams(("arbitrary", "arbitrary")),
        name="out_proj",
    )(merged, w, x)


def _split2(x):
    hi = x.astype(BF16)
    lo = (x - hi.astype(F32)).astype(BF16)
    return hi, lo


def _split3(x):
    hi = x.astype(BF16)
    r1 = x - hi.astype(F32)
    mid = r1.astype(BF16)
    lo = (r1 - mid.astype(F32)).astype(BF16)
    return hi, mid, lo


def _dot_hp(a, b, dims=(((1,), (0,)), ((), ()))):
    (ca,), (cb,) = dims[0]
    ah, al = _split2(a)
    bh, bl = _split2(b)
    a3 = jnp.concatenate([ah, ah, al], axis=ca)
    b3 = jnp.concatenate([bh, bl, bh], axis=cb)
    return lax.dot_general(a3, b3, dims, preferred_element_type=F32)


_NT = (((1,), (1,)), ((), ()))
_TN = (((0,), (0,)), ((), ()))


def _seg_sum(x, e_ref, et_ref):
    e = e_ref[...]
    et = et_ref[...]
    hi, mid, lo = _split3(x)
    s = _dot(hi, e) + _dot(mid, e) + _dot(lo, e)
    hi, mid, lo = _split3(s)
    return _dot(hi, et) + _dot(mid, et) + _dot(lo, et)


def _rwkv_prep(r, k, v, z, w0, a0, k_k, k_a, r_k, wd, wa, wg, e_ref, et_ref):
    lora_w = _dot(jnp.tanh(z).astype(BF16), wd)
    lora_a = _dot(z.astype(BF16), wa)
    gate = _dot(_sigmoid(z).astype(BF16), wg)
    log_decay = -_sigmoid(w0 + lora_w) * math.exp(-0.5)
    a = _sigmoid(a0 + lora_a)
    kk = k * k_k
    kk = kk / jnp.maximum(jnp.sqrt(_seg_sum(kk * kk, e_ref, et_ref)), L2_EPS)
    k2 = k * (1.0 + (a - 1.0) * k_a)
    bonus = _seg_sum(r * k2 * r_k, e_ref, et_ref) * v
    return log_decay, a, gate, kk, k2, bonus


def _rwkv_finish(y, bonus, gate, ln_g, ln_b, e_ref, et_ref):
    inv = 1.0 / HEAD_DIM
    mu = _seg_sum(y, e_ref, et_ref) * inv
    yc = y - mu
    var = _seg_sum(yc * yc, e_ref, et_ref) * inv
    yn = yc * lax.rsqrt(var + GN_EPS) * ln_g + ln_b
    return (yn + bonus) * gate


def _wkv_prompt_kernel(pr_ref, pk_ref, pv_ref, pz_ref, mur_ref, muk_ref, muv_ref, muz_ref,
                       w0_ref, a0_ref, kk_ref, ka_ref, rk_ref, lng_ref, lnb_ref,
                       wd_ref, wa_ref, wg_ref, e_ref, et_ref, tri_ref,
                       ya_ref, sout_ref,
                       s_scr, cr_scr, ck_scr, cv_scr, cz_scr,
                       at_scr, bt_scr, kt_scr, rt_scr, bh_scr, kh_scr, v_scr, y_scr, gc_scr,
                       *, n_pairs, pair_unroll):
    c = pl.program_id(1)
    n_chunks = pl.num_programs(1)
    C = WKV_CHUNK

    @pl.when(c == 0)
    def _init():
        s_scr[...] = jnp.zeros_like(s_scr)
        cr_scr[...] = jnp.zeros_like(cr_scr)
        ck_scr[...] = jnp.zeros_like(ck_scr)
        cv_scr[...] = jnp.zeros_like(cv_scr)
        cz_scr[...] = jnp.zeros_like(cz_scr)

    def shift(p_ref, carry, mu_ref):
        p = p_ref[...]
        prev = pltpu.roll(p, 1, axis=0)
        row = lax.broadcasted_iota(jnp.int32, p.shape, 0)
        prev = jnp.where(row == 0, carry[0:1, :], prev)
        carry[0:1, :] = p[C - 1:C, :]
        return p + (prev - p) * mu_ref[...]

    r = shift(pr_ref, cr_scr, mur_ref)
    k = shift(pk_ref, ck_scr, muk_ref)
    v = shift(pv_ref, cv_scr, muv_ref)
    z = shift(pz_ref, cz_scr, muz_ref)
    lw, a, gate, kk, k2, bonus = _rwkv_prep(
        r, k, v, z, w0_ref[...], a0_ref[...], kk_ref[...], ka_ref[...], rk_ref[...],
        wd_ref[...], wa_ref[...], wg_ref[...], e_ref, et_ref)

    tri = tri_ref[...]
    hi, mid, lo = _split3(lw)
    cl = _dot(tri, hi) + _dot(tri, mid) + _dot(tri, lo)
    cl_end = cl[C - 1:C, :]
    beta = kk * a
    e_neg = jnp.exp(-cl)
    e_hat = jnp.exp(cl_end - cl)
    def to_pairs(scr, x):
        for q in range(n_pairs):
            scr[q] = x[:, q * LANES:(q + 1) * LANES]

    to_pairs(at_scr, -kk * jnp.exp(cl - lw))
    to_pairs(bt_scr, beta * e_neg)
    to_pairs(kt_scr, k2 * e_neg)
    to_pairs(rt_scr, r * jnp.exp(cl))
    to_pairs(bh_scr, beta * e_hat)
    to_pairs(kh_scr, k2 * e_hat)
    to_pairs(v_scr, v)
    to_pairs(gc_scr, jnp.broadcast_to(jnp.exp(cl_end), (2 * C, cl.shape[1])))

    lane = lax.broadcasted_iota(jnp.int32, (C, LANES), 1)
    first = lane < HEAD_DIM
    row2 = lax.broadcasted_iota(jnp.int32, (2 * C, 2 * C), 0)
    col2 = lax.broadcasted_iota(jnp.int32, (2 * C, 2 * C), 1)
    cbits = C.bit_length() - 1
    same = (row2 >> cbits) == (col2 >> cbits)
    tri_strict = same & ((row2 & (C - 1)) > (col2 & (C - 1)))
    tri_incl = same & ((row2 & (C - 1)) >= (col2 & (C - 1)))
    eye = row2 == col2

    def stack(x):
        return jnp.concatenate([jnp.where(first, x, 0.0), jnp.where(first, 0.0, x)], axis=0)

    def one_pair(p):
        a_s = stack(at_scr[p])
        b_s = stack(bt_scr[p])
        k_s = stack(kt_scr[p])
        r_s = stack(rt_scr[p])
        v_s = stack(v_scr[p])
        bh_s = stack(bh_scr[p])
        kh_s = stack(kh_scr[p])
        g = _dot_hp(jnp.concatenate([a_s, r_s], axis=0), jnp.concatenate([b_s, k_s], axis=0), _NT)
        l_ba = jnp.where(tri_strict, g[:2 * C, :2 * C], 0.0)
        l_ka = jnp.where(tri_strict, g[:2 * C, 2 * C:], 0.0)
        m_br = jnp.where(tri_incl, g[2 * C:, :2 * C], 0.0)
        m_kr = jnp.where(tri_incl, g[2 * C:, 2 * C:], 0.0)
        t_inv = jnp.where(eye, 1.0, l_ba)
        pw = l_ba
        steps = (C - 1).bit_length() - 1
        for _ in range(steps):
            pw = _dot_hp(pw, pw)
            t_inv = t_inv + _dot_hp(t_inv, pw)
        x = jnp.concatenate([a_s, _dot_hp(l_ka, v_s)], axis=1)
        tx = _dot_hp(t_inv, x)
        a_hat, u_hat = tx[:, :LANES], tx[:, LANES:]
        s_t = s_scr[p]
        ps = _dot_hp(jnp.concatenate([a_hat, r_s], axis=0), s_t)
        u = ps[:2 * C] + u_hat
        uv = jnp.concatenate([u, v_s], axis=0)
        ys = ps[2 * C:] + _dot_hp(jnp.concatenate([m_br, m_kr], axis=1), uv)
        y_scr[p] = ys[:C] + ys[C:]
        dg = jnp.where(eye, gc_scr[p], 0.0)
        s_scr[p] = _dot_hp(jnp.concatenate([bh_s, kh_s, dg], axis=0),
                           jnp.concatenate([uv, s_t], axis=0), _TN)

    def body(i, carry):
        for j in range(pair_unroll):
            one_pair(i * pair_unroll + j)
        return carry

    lax.fori_loop(0, n_pairs // pair_unroll, body, 0)

    y = jnp.concatenate([y_scr[q] for q in range(n_pairs)], axis=1)
    out = _rwkv_finish(y, bonus, gate, lng_ref[...], lnb_ref[...], e_ref, et_ref)
    ya_ref[...] = out.astype(BF16)

    @pl.when(c == n_chunks - 1)
    def _store_state():
        sout_ref[0] = s_scr[...]


def _wkv_prompt(p, n_b, seq, d_a, d, mu_rkv, mu_z, prm, lora, e_mat, et_mat):
    C = WKV_CHUNK
    n_chunks = seq // C
    n_pairs = d_a // LANES
    zb = (4 * d) // LORA_PAD
    row = lambda b, c: b * n_chunks + c
    vec = lambda w: pl.BlockSpec((1, w), lambda b, c: (0, 0))
    full = lambda s: pl.BlockSpec(s, lambda b, c: (0,) * len(s))
    tri = (lax.broadcasted_iota(jnp.int32, (C, C), 0) >= lax.broadcasted_iota(jnp.int32, (C, C), 1)).astype(BF16)
    in_specs = [pl.BlockSpec((C, d_a), lambda b, c: (row(b, c), 0)),
                pl.BlockSpec((C, d_a), lambda b, c: (row(b, c), 1)),
                pl.BlockSpec((C, d_a), lambda b, c: (row(b, c), 2)),
                pl.BlockSpec((C, LORA_PAD), lambda b, c: (row(b, c), zb)),
                vec(d_a), vec(d_a), vec(d_a), vec(LORA_PAD)] + [vec(d_a)] * 7 + [
                full((LORA_PAD, d_a))] * 3 + [full((d_a, LANES)), full((LANES, d_a)), full((C, C))]
    scr = [pltpu.VMEM((n_pairs, LANES, LANES), F32),
           pltpu.VMEM((8, d_a), F32), pltpu.VMEM((8, d_a), F32), pltpu.VMEM((8, d_a), F32),
           pltpu.VMEM((8, LORA_PAD), F32)] + [pltpu.VMEM((n_pairs, C, LANES), F32)] * 8 + [
           pltpu.VMEM((n_pairs, 2 * C, LANES), F32)]
    pair_unroll = 2 if n_pairs % 2 == 0 else 1
    ya, s_out = pl.pallas_call(
        functools.partial(_wkv_prompt_kernel, n_pairs=n_pairs, pair_unroll=pair_unroll),
        grid=(n_b, n_chunks),
        in_specs=in_specs,
        out_specs=[pl.BlockSpec((C, d_a), lambda b, c: (row(b, c), 0)),
                   pl.BlockSpec((1, n_pairs, LANES, LANES), lambda b, c: (b, 0, 0, 0))],
        out_shape=[jax.ShapeDtypeStruct((n_b * seq, d_a), BF16),
                   jax.ShapeDtypeStruct((n_b, n_pairs, LANES, LANES), F32)],
        scratch_shapes=scr,
        compiler_params=_cparams(("arbitrary", "arbitrary")),
        name="wkv_prompt",
    )(p, p, p, p, mu_rkv[0], mu_rkv[1], mu_rkv[2], mu_z, *prm, *lora, e_mat, et_mat, tri)
    return ya, s_out


def _wkv_sample_prep_kernel(pr_ref, pk_ref, pv_ref, pz_ref, sr_ref, sk_ref, sv_ref, sz_ref,
                            mur_ref, muk_ref, muv_ref, muz_ref,
                            w0_ref, a0_ref, kk_ref, ka_ref, rk_ref,
                            wd_ref, wa_ref, wg_ref, e_ref, et_ref,
                            w_out, nkk_out, kka_out, k2_out, r_out, vt_out, bonus_out, gate_out):
    def shift(p_ref, s_ref, mu_ref):
        p = p_ref[...]
        return p + (s_ref[...] - p) * mu_ref[...]

    r = shift(pr_ref, sr_ref, mur_ref)
    k = shift(pk_ref, sk_ref, muk_ref)
    v = shift(pv_ref, sv_ref, muv_ref)
    z = shift(pz_ref, sz_ref, muz_ref)
    lw, a, gate, kk, k2, bonus = _rwkv_prep(
        r, k, v, z, w0_ref[...], a0_ref[...], kk_ref[...], ka_ref[...], rk_ref[...],
        wd_ref[...], wa_ref[...], wg_ref[...], e_ref, et_ref)
    w_out[...] = jnp.exp(lw)
    nkk_out[...] = -kk
    kka_out[...] = kk * a
    k2_out[...] = k2
    r_out[...] = r
    vt_out[...] = v.T
    bonus_out[...] = bonus
    gate_out[...] = gate


def _wkv_sample_step_kernel(s_ref, w_ref, nkk_ref, kka_ref, k2_ref, r_ref, vt_ref,
                            snew_ref, yt_ref, ycol_scr, *, n_heads, n_seq):
    b = pl.program_id(0)
    lane = lax.broadcasted_iota(jnp.int32, vt_ref.shape, 1)
    pick = lane == b
    vcol = jnp.sum(jnp.where(pick, vt_ref[...], 0.0), axis=1, keepdims=True)
    N = HEAD_DIM
    for h in range(n_heads):
        cols = slice(h * N, (h + 1) * N)
        bc = lambda ref: jnp.broadcast_to(ref[0, :, cols], (N, N))
        s = s_ref[0, h]
        sa = jnp.sum(s * bc(nkk_ref), axis=1, keepdims=True)
        s_new = s * bc(w_ref) + sa * bc(kka_ref) + vcol[h * N:(h + 1) * N, :] * bc(k2_ref)
        snew_ref[0, h] = s_new
        ycol_scr[h * N:(h + 1) * N, :] = jnp.sum(s_new * bc(r_ref), axis=1, keepdims=True)

    @pl.when(b == 0)
    def _():
        yt_ref[...] = jnp.zeros_like(yt_ref)

    yt_ref[...] = jnp.where(pick, ycol_scr[...], yt_ref[...])


def _wkv_sample_finish_kernel(yt_ref, bonus_ref, gate_ref, lng_ref, lnb_ref, e_ref, et_ref, ya_ref):
    y = yt_ref[...].T
    out = _rwkv_finish(y, bonus_ref[...], gate_ref[...], lng_ref[...], lnb_ref[...], e_ref, et_ref)
    ya_ref[...] = out.astype(BF16)


def _wkv_sample(p, n_p, n_s, d_a, d, s_shift, state, mu_rkv, mu_z, prm, lora, e_mat, et_mat):
    assert n_p % n_s == 0
    rb = n_p // n_s
    zb = (4 * d) // LORA_PAD
    n_heads = d_a // HEAD_DIM
    w0, a0, k_k, k_a, r_k, ln_g, ln_b = prm
    full = lambda s: pl.BlockSpec(s, lambda i: (0,) * len(s))
    tok = jax.ShapeDtypeStruct((n_s, d_a), F32)
    outs = pl.pallas_call(
        _wkv_sample_prep_kernel,
        grid=(1,),
        in_specs=[pl.BlockSpec((n_s, d_a), lambda i: (rb, 0)),
                  pl.BlockSpec((n_s, d_a), lambda i: (rb, 1)),
                  pl.BlockSpec((n_s, d_a), lambda i: (rb, 2)),
                  pl.BlockSpec((n_s, LORA_PAD), lambda i: (rb, zb)),
                  full((n_s, d_a)), full((n_s, d_a)), full((n_s, d_a)), full((n_s, LORA_PAD)),
                  full((1, d_a)), full((1, d_a)), full((1, d_a)), full((1, LORA_PAD))]
                 + [full((1, d_a))] * 5 + [full((LORA_PAD, d_a))] * 3
                 + [full((d_a, LANES)), full((LANES, d_a))],
        out_specs=[full((n_s, d_a))] * 5 + [full((d_a, n_s))] + [full((n_s, d_a))] * 2,
        out_shape=[tok] * 5 + [jax.ShapeDtypeStruct((d_a, n_s), F32)] + [tok] * 2,
        compiler_params=_cparams(("arbitrary",)),
        name="wkv_sample_prep",
    )(p, p, p, p, *s_shift, mu_rkv[0], mu_rkv[1], mu_rkv[2], mu_z,
      w0, a0, k_k, k_a, r_k, *lora, e_mat, et_mat)
    w, nkk, kka, k2, r, vt, bonus, gate = outs
    row3 = lambda x: x.reshape(n_s, 1, d_a)
    rowspec = pl.BlockSpec((1, 1, d_a), lambda b: (b, 0, 0))
    s_new, yt = pl.pallas_call(
        functools.partial(_wkv_sample_step_kernel, n_heads=n_heads, n_seq=n_s),
        grid=(n_s,),
        in_specs=[pl.BlockSpec((1, n_heads, HEAD_DIM, HEAD_DIM), lambda b: (b, 0, 0, 0))]
                 + [rowspec] * 5 + [pl.BlockSpec((d_a, n_s), lambda b: (0, 0))],
        out_specs=[pl.BlockSpec((1, n_heads, HEAD_DIM, HEAD_DIM), lambda b: (b, 0, 0, 0)),
                   pl.BlockSpec((d_a, n_s), lambda b: (0, 0))],
        out_shape=[jax.ShapeDtypeStruct(state.shape, F32), jax.ShapeDtypeStruct((d_a, n_s), F32)],
        scratch_shapes=[pltpu.VMEM((d_a, 1), F32)],
        compiler_params=_cparams(("arbitrary",)),
        name="wkv_sample_step",
    )(state, row3(w), row3(nkk), row3(kka), row3(k2), row3(r), vt)
    ya = pl.pallas_call(
        _wkv_sample_finish_kernel,
        grid=(1,),
        in_specs=[full((d_a, n_s)), full((n_s, d_a)), full((n_s, d_a)), full((1, d_a)), full((1, d_a)),
                  full((d_a, LANES)), full((LANES, d_a))],
        out_specs=full((n_s, d_a)),
        out_shape=jax.ShapeDtypeStruct((n_s, d_a), BF16),
        compiler_params=_cparams(("arbitrary",)),
        name="wkv_sample_finish",
    )(yt, bonus, gate, ln_g, ln_b, e_mat, et_mat)
    return ya, s_new


def _pool_prompt_kernel(u_ref, w_ref, sc_ref, o_ref, carry_scr, *, tt, gw):
    t = pl.program_id(1)

    @pl.when(t == 0)
    def _():
        carry_scr[...] = jnp.zeros_like(carry_scr)

    u = u_ref[...]
    pos = t * tt + lax.broadcasted_iota(jnp.int32, (tt, gw), 0)
    for gi, win in enumerate(POOL_WINDOWS):
        cols = slice(gi * gw, (gi + 1) * gw)
        ug = u[:, cols]
        cur = jnp.concatenate([carry_scr[:, cols], ug], axis=0)
        off = 0
        step = 1
        while step < win:
            cur = cur[step:] + cur[:-step]
            off += step
            step *= 2
        wsum = cur[POOL_CARRY - off:POOL_CARRY - off + tt]
        cnt = jnp.minimum(pos + 1, win).astype(F32)
        pooled = wsum / cnt - ug
        y = _dot(pooled.astype(BF16), w_ref[gi].astype(BF16)) * sc_ref[:, cols]
        o_ref[:, cols] = y.astype(BF16)
    carry_scr[...] = u[tt - POOL_CARRY:, :]


def _pool_prompt(p, n_b, seq, pw, w_pool, pool_scale):
    tt = _tile(seq, 256)
    nt = seq // tt
    gw = pw // len(POOL_WINDOWS)
    return pl.pallas_call(
        functools.partial(_pool_prompt_kernel, tt=tt, gw=gw),
        grid=(n_b, nt),
        in_specs=[pl.BlockSpec((tt, pw), lambda b, t: (b * nt + t, 3)),
                  pl.BlockSpec(w_pool.shape, lambda b, t: (0, 0, 0)),
                  pl.BlockSpec((1, pw), lambda b, t: (0, 0))],
        out_specs=pl.BlockSpec((tt, pw), lambda b, t: (b * nt + t, 0)),
        out_shape=jax.ShapeDtypeStruct((n_b * seq, pw), BF16),
        scratch_shapes=[pltpu.VMEM((POOL_CARRY, pw), F32)],
        compiler_params=_cparams(("arbitrary", "arbitrary")),
        name="pool_prompt",
    )(p, w_pool, pool_scale.reshape(1, pw))


def _pool_sample_kernel(u_ref, hist_ref, w_ref, sc_ref, o_ref, *, gw):
    u = u_ref[...]
    for gi, win in enumerate(POOL_WINDOWS):
        cols = slice(gi * gw, (gi + 1) * gw)
        ug = u[:, cols]
        wsum = ug
        for dback in range(1, win):
            wsum = wsum + hist_ref[POOL_BUF - dback, :, cols]
        pooled = wsum / float(win) - ug
        y = _dot(pooled.astype(BF16), w_ref[gi].astype(BF16)) * sc_ref[:, cols]
        o_ref[:, cols] = y.astype(BF16)


def _pool_sample(p, n_p, n_s, pw, hist_t, w_pool, pool_scale):
    gw = pw // len(POOL_WINDOWS)
    full = lambda s: pl.BlockSpec(s, lambda i: (0,) * len(s))
    return pl.pallas_call(
        functools.partial(_pool_sample_kernel, gw=gw),
        grid=(1,),
        in_specs=[pl.BlockSpec((n_s, pw), lambda i: (n_p // n_s, 3)),
                  full(hist_t.shape), full(w_pool.shape), full((1, pw))],
        out_specs=full((n_s, pw)),
        out_shape=jax.ShapeDtypeStruct((n_s, pw), BF16),
        compiler_params=_cparams(("arbitrary",)),
        name="pool_sample",
    )(p, hist_t, w_pool, pool_scale.reshape(1, pw))


def _router_kernel(x_ref, g_ref, wr_ref, h_ref, eid_ref, wt_ref):
    x = x_ref[...]
    h = x * lax.rsqrt(jnp.mean(x * x, axis=-1, keepdims=True) + RMS_EPS) * g_ref[...]
    h_ref[...] = h
    logits = _dot(h.astype(BF16), wr_ref[...])
    lane = lax.broadcasted_iota(jnp.int32, logits.shape, 1)
    neg = jnp.float32(-jnp.inf)
    big = jnp.int32(1 << 20)
    is_g = lane < N_GROUPS
    lg = jnp.where(is_g, logits, neg)
    mg = jnp.max(lg, axis=1, keepdims=True)
    g_sel = jnp.min(jnp.where(is_g & (lg == mg), lane, big), axis=1, keepdims=True)
    p_sel = 1.0 / jnp.sum(jnp.where(is_g, jnp.exp(lg - mg), 0.0), axis=1, keepdims=True)
    e_lane = lane - N_GROUPS
    in_grp = (e_lane >= 0) & (e_lane < N_EXPERTS) & ((e_lane >> 3) == g_sel)
    le = jnp.where(in_grp, logits, neg)
    m1 = jnp.max(le, axis=1, keepdims=True)
    i1 = jnp.min(jnp.where(in_grp & (le == m1), lane, big), axis=1, keepdims=True)
    le2 = jnp.where(lane == i1, neg, le)
    m2 = jnp.max(le2, axis=1, keepdims=True)
    i2 = jnp.min(jnp.where(in_grp & (lane != i1) & (le2 == m2), lane, big), axis=1, keepdims=True)
    e2 = jnp.exp(m2 - m1)
    w1 = p_sel / (1.0 + e2)
    w2 = p_sel * e2 / (1.0 + e2)
    eid_ref[...] = jnp.where(lane == 0, i1 - N_GROUPS, jnp.where(lane == 1, i2 - N_GROUPS, 0))
    wt_ref[...] = jnp.where(lane == 0, w1, jnp.where(lane == 1, w2, 0.0))


def _router(x1, g, w_router):
    n, d = x1.shape
    tr = _tile(n, 256)
    return pl.pallas_call(
        _router_kernel,
        grid=(n // tr,),
        in_specs=[pl.BlockSpec((tr, d), lambda i: (i, 0)),
                  pl.BlockSpec((1, d), lambda i: (0, 0)),
                  pl.BlockSpec((d, LANES), lambda i: (0, 0))],
        out_specs=[pl.BlockSpec((tr, d), lambda i: (i, 0)),
                   pl.BlockSpec((tr, LANES), lambda i: (i, 0)),
                   pl.BlockSpec((tr, LANES), lambda i: (i, 0))],
        out_shape=[jax.ShapeDtypeStruct((n, d), F32),
                   jax.ShapeDtypeStruct((n, LANES), jnp.int32),
                   jax.ShapeDtypeStruct((n, LANES), F32)],
        compiler_params=_cparams(("arbitrary",)),
        name="router",
    )(x1, g.reshape(1, d), w_router)


def _moe_kernel(ge_ref, gs_ref, gn_ref, tok_ref, dst_ref,
                h_hbm, wg_ref, wu_ref, wd_ref, out_hbm,
                xf_scr, xb_scr, acc_scr, sem_g, sem_s, *, rows, n_ftiles):
    g = pl.program_id(0)
    f = pl.program_id(1)
    nrows = gn_ref[g]
    start = gs_ref[g]

    def gather_copy(tok, i):
        return pltpu.make_async_copy(h_hbm.at[pl.ds(tok, 1)], xf_scr.at[pl.ds(i, 1)], sem_g)

    def scatter_copy(i, dst):
        return pltpu.make_async_copy(acc_scr.at[pl.ds(i, 1)], out_hbm.at[pl.ds(dst, 1)], sem_s)

    @pl.when(nrows > 0)
    def _group():
        @pl.when(f == 0)
        def _gather():
            def issue(i, carry):
                gather_copy(tok_ref[start + i], i).start()
                return carry
            lax.fori_loop(0, rows, issue, 0)

            def wait(i, carry):
                gather_copy(0, 0).wait()
                return carry
            lax.fori_loop(0, rows, wait, 0)
            xb_scr[...] = xf_scr[...].astype(BF16)
            acc_scr[...] = jnp.zeros_like(acc_scr)

        x = xb_scr[...]
        hg = _dot(x, wg_ref[0].astype(BF16))
        hu = _dot(x, wu_ref[0].astype(BF16))
        act = (hg * _sigmoid(hg) * hu).astype(BF16)
        acc_scr[...] += _dot(act, wd_ref[0].astype(BF16))

        @pl.when(f == n_ftiles - 1)
        def _scatter():
            def issue(i, carry):
                scatter_copy(i, dst_ref[start + i]).start()
                return carry
            lax.fori_loop(0, nrows, issue, 0)

            def wait(i, carry):
                scatter_copy(0, 0).wait()
                return carry
            lax.fori_loop(0, nrows, wait, 0)


def _moe(h2, eid, w_gate, w_up, w_down, rows, tf):
    n, d = h2.shape
    d_e = w_gate.shape[2]
    n_assign = n * TOP_K
    n_ftiles = d_e // tf
    max_groups = -(-n_assign // rows) + N_EXPERTS
    eflat = eid.reshape(-1)
    order = jnp.argsort(eflat, stable=True).astype(jnp.int32)
    tok_sorted = order // TOP_K
    dst_sorted = (order % TOP_K) * n + tok_sorted
    counts = jnp.zeros((N_EXPERTS,), jnp.int32).at[eflat].add(1)
    starts = jnp.cumsum(counts) - counts
    groups_per_e = (counts + rows - 1) // rows
    g_ends = jnp.cumsum(groups_per_e)
    gidx = jnp.arange(max_groups, dtype=jnp.int32)
    n_groups = g_ends[-1]
    g_e = jnp.minimum(jnp.searchsorted(g_ends, gidx, side="right"), N_EXPERTS - 1).astype(jnp.int32)
    local = gidx - (g_ends[g_e] - groups_per_e[g_e])
    g_start = starts[g_e] + local * rows
    g_n = jnp.clip(counts[g_e] - local * rows, 0, rows)
    valid = gidx < n_groups
    last_e = g_e[jnp.maximum(n_groups - 1, 0)]
    g_e = jnp.where(valid, g_e, last_e).astype(jnp.int32)
    g_start = jnp.where(valid, g_start, 0).astype(jnp.int32)
    g_n = jnp.where(valid, g_n, 0).astype(jnp.int32)
    pad = jnp.zeros((rows,), jnp.int32)
    tok_sorted = jnp.concatenate([tok_sorted, pad])
    dst_sorted = jnp.concatenate([dst_sorted, pad])

    def fsel(g, f, gn):
        return jnp.where(gn[g] > 0, f, n_ftiles - 1)

    grid_spec = pltpu.PrefetchScalarGridSpec(
        num_scalar_prefetch=5,
        grid=(max_groups, n_ftiles),
        in_specs=[pl.BlockSpec(memory_space=pl.ANY),
                  pl.BlockSpec((1, d, tf), lambda g, f, ge, gs, gn, tk, ds: (ge[g], 0, fsel(g, f, gn))),
                  pl.BlockSpec((1, d, tf), lambda g, f, ge, gs, gn, tk, ds: (ge[g], 0, fsel(g, f, gn))),
                  pl.BlockSpec((1, tf, d), lambda g, f, ge, gs, gn, tk, ds: (ge[g], fsel(g, f, gn), 0))],
        out_specs=pl.BlockSpec(memory_space=pl.ANY),
        scratch_shapes=[pltpu.VMEM((rows, d), F32), pltpu.VMEM((rows, d), BF16), pltpu.VMEM((rows, d), F32),
                        pltpu.SemaphoreType.DMA(()), pltpu.SemaphoreType.DMA(())],
    )
    return pl.pallas_call(
        functools.partial(_moe_kernel, rows=rows, n_ftiles=n_ftiles),
        grid_spec=grid_spec,
        out_shape=jax.ShapeDtypeStruct((TOP_K * n, d), F32),
        compiler_params=_cparams(("arbitrary", "arbitrary")),
        name="moe_experts",
    )(g_e, g_start, g_n, tok_sorted, dst_sorted, h2, w_gate, w_up, w_down)


def _combine_kernel(x_ref, y0_ref, y1_ref, wt_ref, g_ref, op_ref, os_ref, *, n_prompt_tiles):
    i = pl.program_id(0)
    wt = wt_ref[...]
    x = x_ref[...] + (y0_ref[...] * wt[:, 0:1] + y1_ref[...] * wt[:, 1:2])
    y = x * lax.rsqrt(jnp.mean(x * x, axis=-1, keepdims=True) + RMS_EPS) * g_ref[...]

    @pl.when(i < n_prompt_tiles)
    def _():
        op_ref[...] = y

    @pl.when(i >= n_prompt_tiles)
    def _():
        os_ref[...] = y


def _combine(x1, y2, wt, g, n_p, n_s):
    n, d = x1.shape
    tr = _tile(n_s, 256)
    npt, nst = n_p // tr, n_s // tr
    nt = npt + nst
    return pl.pallas_call(
        functools.partial(_combine_kernel, n_prompt_tiles=npt),
        grid=(nt,),
        in_specs=[pl.BlockSpec((tr, d), lambda i: (i, 0)),
                  pl.BlockSpec((tr, d), lambda i: (i, 0)),
                  pl.BlockSpec((tr, d), lambda i: (i + nt, 0)),
                  pl.BlockSpec((tr, LANES), lambda i: (i, 0)),
                  pl.BlockSpec((1, d), lambda i: (0, 0))],
        out_specs=[pl.BlockSpec((tr, d), lambda i: (jnp.minimum(i, npt - 1), 0)),
                   pl.BlockSpec((tr, d), lambda i: (jnp.maximum(i - npt, 0), 0))],
        out_shape=[jax.ShapeDtypeStruct((n_p, d), F32), jax.ShapeDtypeStruct((n_s, d), F32)],
        compiler_params=_cparams(("arbitrary",)),
        name="combine",
    )(x1, y2, y2, wt, g.reshape(1, d))


def _layer(xp, xs, st_wkv, st_shift, st_pool, norm_out_g, norm1_g, w_in, mu_shift, w0, w_decay_up, a0,
           w_iclr_up, w_gate_up, k_k, k_a, r_k, ln_x_g, ln_x_b, w_pool, pool_scale,
           w_branch_a, w_branch_b, w_out, norm2_g, w_router_group, w_router_expert,
           w_exp_gate, w_exp_up, w_exp_down):
    n_b, seq, d = xp.shape
    n_sb = xs.shape[0]
    n_p, n_s = n_b * seq, n_sb * xs.shape[1]
    n = n_p + n_s
    d_a = w_branch_a.shape[0]
    pw = w_branch_b.shape[0]
    sw = 3 * d_a + LORA_W
    assert d_a == pw and d == 2 * d_a and xs.shape[1] == 1 and w_in.shape[1] == sw + pw + 2 * d

    zpad = LORA_PAD - LORA_W
    w_in_p = jnp.concatenate([w_in[:, :3 * d_a], w_in[:, sw:], w_in[:, 3 * d_a:sw],
                              jnp.zeros((d, zpad), F32)], axis=1)
    pieces = lambda t: ([t[..., i * d_a:(i + 1) * d_a] for i in range(3)],
                        jnp.pad(t[..., 3 * d_a:sw], [(0, 0)] * (t.ndim - 1) + [(0, zpad)]))
    mu_rkv, mu_z = pieces(mu_shift.reshape(1, sw))
    ss_rkv, ss_z = pieces(st_shift)
    vec = lambda t: t.reshape(1, d_a)
    prm = (vec(w0), vec(a0), vec(k_k), vec(k_a), vec(r_k), vec(ln_x_g), vec(ln_x_b))
    lpad = lambda w, r0: jnp.zeros((LORA_PAD, d_a), F32).at[r0:r0 + w.shape[0]].set(w).astype(BF16)
    lora = (lpad(w_decay_up, 0), lpad(w_iclr_up, DECAY_RANK), lpad(w_gate_up, DECAY_RANK + ICLR_RANK))
    head_of = jnp.arange(d_a, dtype=jnp.int32) // HEAD_DIM
    e_mat = (head_of[:, None] == jnp.arange(LANES, dtype=jnp.int32)[None, :]).astype(BF16)
    et_mat = e_mat.T

    tm = _tile(n, 1664, 8)
    h, xcat = _norm1(xp.reshape(n_p, d), xs.reshape(n_s, d), norm1_g)
    p = _matmul(h, w_in_p, tm, _tile(w_in_p.shape[1], 256, LANES))

    ya_p, s_pairs = _wkv_prompt(p, n_b, seq, d_a, d, mu_rkv, mu_z, prm, lora, e_mat, et_mat)
    ya_s, new_wkv_s = _wkv_sample(p, n_p, n_s, d_a, d, (*ss_rkv, ss_z), st_wkv, mu_rkv, mu_z,
                                  prm[:7], lora, e_mat, et_mat)
    yb_p = _pool_prompt(p, n_b, seq, pw, w_pool, pool_scale)
    yb_s = _pool_sample(p, n_p, n_s, pw, jnp.swapaxes(st_pool, 0, 1), w_pool, pool_scale)
    ya = jnp.concatenate([ya_p, ya_s], axis=0)
    yb = jnp.concatenate([yb_p, yb_s], axis=0)

    tn = _tile(d, 256, LANES)
    merged = _merge(ya, yb, w_branch_a, w_branch_b, p, d, tm, tn)
    x1 = _outproj(merged, w_out, xcat, tm, tn)

    w_router = jnp.concatenate([w_router_group, w_router_expert,
                                jnp.zeros((d, LANES - N_GROUPS - N_EXPERTS), F32)], axis=1).astype(BF16)
    h2, eid, wt = _router(x1, norm2_g, w_router)
    d_e = w_exp_gate.shape[2]
    y2 = _moe(h2, eid[:, :TOP_K], w_exp_gate, w_exp_up, w_exp_down,
              rows=512 if n * TOP_K >= 4096 else 64, tf=_tile(d_e, 256, LANES))
    y_p, y_s = _combine(x1, y2, wt, norm_out_g, n_p, n_s)

    n_heads = d_a // HEAD_DIM
    sp = s_pairs.reshape(n_b, n_heads // 2, 2, HEAD_DIM, 2, HEAD_DIM)
    new_wkv_p = jnp.stack([sp[:, :, 0, :, 0, :], sp[:, :, 1, :, 1, :]], axis=2)
    new_wkv_p = jnp.swapaxes(new_wkv_p.reshape(n_b, n_heads, HEAD_DIM, HEAD_DIM), -1, -2)
    unperm = lambda rows: jnp.concatenate([rows[:, :3 * d_a], rows[:, 4 * d:4 * d + LORA_W]], axis=1)
    p_last = p[:n_p].reshape(n_b, seq, -1)[:, -1]
    new_shift_p = unperm(p_last)
    new_shift_s = unperm(p[n_p:])
    u_p = p[:n_p, 3 * d_a:3 * d_a + pw].reshape(n_b, seq, pw)
    new_pool_p = u_p[:, seq - POOL_BUF:]
    new_pool_s = jnp.concatenate([st_pool[:, 1:], p[n_p:, 3 * d_a:3 * d_a + pw][:, None, :]], axis=1)
    return (y_p.reshape(n_b, seq, d), y_s.reshape(n_sb, 1, d),
            new_wkv_p, new_shift_p, new_pool_p, new_wkv_s, new_shift_s, new_pool_s)


def kernel(x_prompt, x_sample, state_wkv, state_shift, state_pool, norm1_g, w_in, mu_shift, w0, w_decay_up, a0, w_iclr_up, w_gate_up, k_k, k_a, r_k, ln_x_g, ln_x_b, w_pool, pool_scale, w_branch_a, w_branch_b, w_out, norm2_g, w_router_group, w_router_expert, w_exp_gate, w_exp_up, w_exp_down, norm_f_g):
    assert norm1_g.shape[0] == 1, "single-layer trunk"
    outs = _layer(x_prompt, x_sample, state_wkv[0], state_shift[0], state_pool[0], norm_f_g,
                  norm1_g[0], w_in[0], mu_shift[0], w0[0], w_decay_up[0], a0[0], w_iclr_up[0], w_gate_up[0],
                  k_k[0], k_a[0], r_k[0], ln_x_g[0], ln_x_b[0], w_pool[0], pool_scale[0],
                  w_branch_a[0], w_branch_b[0], w_out[0], norm2_g[0], w_router_group[0], w_router_expert[0],
                  w_exp_gate[0], w_exp_up[0], w_exp_down[0])
    y_p, y_s = outs[0], outs[1]
    return (y_p, y_s) + tuple(o[None] for o in outs[2:])
```

```python
import functools
import math

import jax
import jax.numpy as jnp
from jax import lax
from jax.experimental import pallas as pl
from jax.experimental.pallas import tpu as pltpu

F32 = jnp.float32
BF16 = jnp.bfloat16

HEAD_DIM = 64
LANES = 128
DECAY_RANK = 96
ICLR_RANK = 96
GATE_RANK = 256
LORA_W = DECAY_RANK + ICLR_RANK + GATE_RANK
LORA_PAD = 512
POOL_WINDOWS = (2, 4, 8, 16)
POOL_BUF = 15
POOL_CARRY = 16
N_GROUPS = 4
EXPERTS_PER_GROUP = 8
N_EXPERTS = N_GROUPS * EXPERTS_PER_GROUP
TOP_K = 2
RMS_EPS = 1e-6
GN_EPS = 6.4e-4
L2_EPS = 1e-12
WKV_CHUNK = 64
WKV_PAIR_UNROLL = 4
VMEM_LIMIT = 56 * 1024 * 1024


def _cparams(sem, vmem=VMEM_LIMIT):
    return pltpu.CompilerParams(dimension_semantics=sem, vmem_limit_bytes=vmem)


def _tile(n, target, mult=8):
    best = None
    for t in range(mult, min(n, target) + 1, mult):
        if n % t == 0:
            best = t
    return best if best is not None else n


def _dot(a, b):
    return jnp.dot(a, b, preferred_element_type=F32)


def _sigmoid(x):
    return 0.5 * jnp.tanh(0.5 * x) + 0.5


def _norm1_kernel(xp_ref, xs_ref, g_ref, h_ref, xc_ref, *, n_prompt_tiles):
    i = pl.program_id(0)

    def body(x):
        xn = x * lax.rsqrt(jnp.mean(x * x, axis=-1, keepdims=True) + RMS_EPS)
        h_ref[...] = (xn * g_ref[...]).astype(BF16)
        xc_ref[...] = x

    @pl.when(i < n_prompt_tiles)
    def _():
        body(xp_ref[...])

    @pl.when(i >= n_prompt_tiles)
    def _():
        body(xs_ref[...])


def _norm1(xp, xs, g):
    n_p, d = xp.shape
    n_s = xs.shape[0]
    tr = _tile(n_s, 256)
    assert n_p % tr == 0
    npt, nst = n_p // tr, n_s // tr
    return pl.pallas_call(
        functools.partial(_norm1_kernel, n_prompt_tiles=npt),
        grid=(npt + nst,),
        in_specs=[pl.BlockSpec((tr, d), lambda i: (jnp.minimum(i, npt - 1), 0)),
                  pl.BlockSpec((tr, d), lambda i: (jnp.maximum(i - npt, 0), 0)),
                  pl.BlockSpec((1, d), lambda i: (0, 0))],
        out_specs=[pl.BlockSpec((tr, d), lambda i: (i, 0)),
                   pl.BlockSpec((tr, d), lambda i: (i, 0))],
        out_shape=[jax.ShapeDtypeStruct((n_p + n_s, d), BF16),
                   jax.ShapeDtypeStruct((n_p + n_s, d), F32)],
        compiler_params=_cparams(("arbitrary",)),
        name="norm1",
    )(xp, xs, g.reshape(1, d))


def _mm_nt_kernel(a_ref, wt_ref, o_ref):
    o_ref[...] = lax.dot_general(a_ref[...], wt_ref[...].astype(BF16), _NT, preferred_element_type=F32)


def _matmul_nt(a, wt, row0, n_out, tm, tn, name):
    m, k = a.shape
    assert row0 % 8 == 0
    return pl.pallas_call(
        _mm_nt_kernel,
        grid=(m // tm, n_out // tn),
        in_specs=[pl.BlockSpec((tm, k), lambda i, j: (i, 0)),
                  pl.BlockSpec((pl.Element(tn), pl.Element(k)), lambda i, j: (pl.multiple_of(row0 + j * tn, 8), 0))],
        out_specs=pl.BlockSpec((tm, tn), lambda i, j: (i, j)),
        out_shape=jax.ShapeDtypeStruct((m, n_out), F32),
        compiler_params=_cparams(("arbitrary", "arbitrary")),
        name=name,
    )(a, wt)


def _merge_kernel(ya_ref, yb_ref, wa_ref, wb_ref, ga_ref, gb_ref, o_ref):
    a = _dot(ya_ref[...], wa_ref[...].astype(BF16))
    b = _dot(yb_ref[...], wb_ref[...].astype(BF16))
    o_ref[...] = (_sigmoid(ga_ref[...]) * a + _sigmoid(gb_ref[...]) * b).astype(BF16)


def _merge(ya, yb, wa, wb, p, gate_col, d, tm, tn):
    m, k = ya.shape
    ga0 = gate_col // tn
    gb0 = (gate_col + d) // tn
    return pl.pallas_call(
        _merge_kernel,
        grid=(m // tm, d // tn),
        in_specs=[pl.BlockSpec((tm, k), lambda i, j: (i, 0)),
                  pl.BlockSpec((tm, k), lambda i, j: (i, 0)),
                  pl.BlockSpec((k, tn), lambda i, j: (0, j)),
                  pl.BlockSpec((k, tn), lambda i, j: (0, j)),
                  pl.BlockSpec((tm, tn), lambda i, j: (i, ga0 + j)),
                  pl.BlockSpec((tm, tn), lambda i, j: (i, gb0 + j))],
        out_specs=pl.BlockSpec((tm, tn), lambda i, j: (i, j)),
        out_shape=jax.ShapeDtypeStruct((m, d), BF16),
        compiler_params=_cparams(("arbitrary", "arbitrary")),
        name="merge",
    )(ya, yb, wa, wb, p, p)


def _outproj_kernel(m_ref, w_ref, x_ref, o_ref):
    o_ref[...] = x_ref[...] + _dot(m_ref[...], w_ref[...].astype(BF16))


def _outproj(merged, w, x, tm, tn):
    m, k = merged.shape
    n = w.shape[1]
    return pl.pallas_call(
        _outproj_kernel,
        grid=(m // tm, n // tn),
        in_specs=[pl.BlockSpec((tm, k), lambda i, j: (i, 0)),
                  pl.BlockSpec((k, tn), lambda i, j: (0, j)),
                  pl.BlockSpec((tm, tn), lambda i, j: (i, j))],
        out_specs=pl.BlockSpec((tm, tn), lambda i, j: (i, j)),
        out_shape=jax.ShapeDtypeStruct((m, n), F32),
        compiler_params=_cparams(("arbitrary", "arbitrary")),
        name="out_proj",
    )(merged, w, x)


def _split2(x):
    hi = x.astype(BF16)
    lo = (x - hi.astype(F32)).astype(BF16)
    return hi, lo


def _split3(x):
    hi = x.astype(BF16)
    r1 = x - hi.astype(F32)
    mid = r1.astype(BF16)
    lo = (r1 - mid.astype(F32)).astype(BF16)
    return hi, mid, lo


def _dot_hp(a, b, dims=(((1,), (0,)), ((), ()))):
    (ca,), (cb,) = dims[0]
    ah, al = _split2(a)
    bh, bl = _split2(b)
    a3 = jnp.concatenate([ah, ah, al], axis=ca)
    b3 = jnp.concatenate([bh, bl, bh], axis=cb)
    return lax.dot_general(a3, b3, dims, preferred_element_type=F32)


_NT = (((1,), (1,)), ((), ()))
_TN = (((0,), (0,)), ((), ()))


def _seg_sum(x, bsum_ref):
    rows = x.shape[0]
    n_slabs = x.shape[1] // LANES
    xs = jnp.concatenate([x[:, q * LANES:(q + 1) * LANES] for q in range(n_slabs)], axis=0)
    hi, lo = _split2(xs)
    s = _dot(jnp.concatenate([hi, lo], axis=1), bsum_ref[...])
    return jnp.concatenate([s[q * rows:(q + 1) * rows] for q in range(n_slabs)], axis=1)


def _rwkv_prep(r, k, v, z, w0, a0, k_k, k_a, r_k, wd, wa, wg, bsum_ref):
    def lora_up(act, w_ref, first_row, n_rows):
        lo = first_row // LANES * LANES
        hi = -(-(first_row + n_rows) // LANES) * LANES
        return _dot(act(z[:, lo:hi]).astype(BF16), w_ref[lo:hi, :])

    lora_w = lora_up(jnp.tanh, wd, 0, DECAY_RANK)
    lora_a = lora_up(lambda t: t, wa, DECAY_RANK, ICLR_RANK)
    gate = lora_up(_sigmoid, wg, DECAY_RANK + ICLR_RANK, GATE_RANK)
    log_decay = -_sigmoid(w0 + lora_w) * math.exp(-0.5)
    a = _sigmoid(a0 + lora_a)
    kk = k * k_k
    kk = kk * jnp.minimum(lax.rsqrt(_seg_sum(kk * kk, bsum_ref)), 1.0 / L2_EPS)
    k2 = k * (1.0 + (a - 1.0) * k_a)
    bonus = _seg_sum(r * k2 * r_k, bsum_ref) * v
    return log_decay, a, gate, kk, k2, bonus


def _rwkv_finish(y, bonus, gate, ln_g, ln_b, bsum_ref):
    inv = 1.0 / HEAD_DIM
    mu = _seg_sum(y, bsum_ref) * inv
    yc = y - mu
    var = _seg_sum(yc * yc, bsum_ref) * inv
    yn = yc * lax.rsqrt(var + GN_EPS) * ln_g + ln_b
    return (yn + bonus) * gate


def _wkv_prompt_kernel(pr_ref, pk_ref, pv_ref, pz_ref, mur_ref, muk_ref, muv_ref, muz_ref,
                       w0_ref, a0_ref, kk_ref, ka_ref, rk_ref, lng_ref, lnb_ref,
                       wd_ref, wa_ref, wg_ref, bsum_ref, tri_ref,
                       ya_ref, sout_ref,
                       s_scr, cr_scr, ck_scr, cv_scr, cz_scr,
                       at_scr, bt_scr, kt_scr, rt_scr, bh_scr, kh_scr, v_scr, y_scr, gc_scr,
                       *, n_pairs, pair_unroll):
    c = pl.program_id(1)
    n_chunks = pl.num_programs(1)
    C = WKV_CHUNK

    @pl.when(c == 0)
    def _init():
        s_scr[...] = jnp.zeros_like(s_scr)
        cr_scr[...] = jnp.zeros_like(cr_scr)
        ck_scr[...] = jnp.zeros_like(ck_scr)
        cv_scr[...] = jnp.zeros_like(cv_scr)
        cz_scr[...] = jnp.zeros_like(cz_scr)

    def shift(p_ref, carry, mu_ref):
        p = p_ref[...]
        prev = pltpu.roll(p, 1, axis=0)
        row = lax.broadcasted_iota(jnp.int32, p.shape, 0)
        prev = jnp.where(row == 0, carry[0:1, :], prev)
        carry[0:1, :] = p[C - 1:C, :]
        return p + (prev - p) * mu_ref[...]

    r = shift(pr_ref, cr_scr, mur_ref)
    k = shift(pk_ref, ck_scr, muk_ref)
    v = shift(pv_ref, cv_scr, muv_ref)
    z = shift(pz_ref, cz_scr, muz_ref)
    lw, a, gate, kk, k2, bonus = _rwkv_prep(
        r, k, v, z, w0_ref[...], a0_ref[...], kk_ref[...], ka_ref[...], rk_ref[...],
        wd_ref, wa_ref, wg_ref, bsum_ref)

    cl = _dot(tri_ref[...], jnp.concatenate(_split3(lw), axis=0))
    cl_end = cl[C - 1:C, :]
    beta = kk * a
    e_neg = jnp.exp(-cl)
    e_hat = jnp.exp(cl_end - cl)
    def to_pairs(scr, x):
        for q in range(n_pairs):
            scr[q] = x[:, q * LANES:(q + 1) * LANES]

    to_pairs(at_scr, -kk * jnp.exp(cl - lw))
    to_pairs(bt_scr, beta * e_neg)
    to_pairs(kt_scr, k2 * e_neg)
    to_pairs(rt_scr, r * jnp.exp(cl))
    to_pairs(bh_scr, beta * e_hat)
    to_pairs(kh_scr, k2 * e_hat)
    to_pairs(v_scr, v)
    to_pairs(gc_scr, jnp.broadcast_to(jnp.exp(cl_end), (2 * C, cl.shape[1])))

    lane = lax.broadcasted_iota(jnp.int32, (C, LANES), 1)
    first = lane < HEAD_DIM
    row2 = lax.broadcasted_iota(jnp.int32, (2 * C, 2 * C), 0)
    col2 = lax.broadcasted_iota(jnp.int32, (2 * C, 2 * C), 1)
    cbits = C.bit_length() - 1
    same = (row2 >> cbits) == (col2 >> cbits)
    tri_strict = same & ((row2 & (C - 1)) > (col2 & (C - 1)))
    tri_incl = same & ((row2 & (C - 1)) >= (col2 & (C - 1)))
    eye = row2 == col2

    def stack(x):
        return jnp.concatenate([jnp.where(first, x, 0.0), jnp.where(first, 0.0, x)], axis=0)

    def load_pair(p):
        return (at_scr[p], bt_scr[p], kt_scr[p], rt_scr[p], v_scr[p], bh_scr[p], kh_scr[p], gc_scr[p], s_scr[p])

    def compute_pair(vals):
        a_s, b_s, k_s, r_s, v_s, bh_s, kh_s = (stack(x) for x in vals[:7])
        gc, s_t = vals[7], vals[8]
        bk = jnp.concatenate([b_s, k_s], axis=0)
        ga = _dot_hp(a_s, bk, _NT)
        yield
        gr = lax.dot_general(r_s.astype(BF16), bk.astype(BF16), _NT, preferred_element_type=F32)
        yield
        l_ba = jnp.where(tri_strict, ga[:, :2 * C], 0.0)
        l_ka = jnp.where(tri_strict, ga[:, 2 * C:], 0.0)
        m_br = jnp.where(tri_incl, gr[:, :2 * C], 0.0)
        m_kr = jnp.where(tri_incl, gr[:, 2 * C:], 0.0)
        lkv = _dot_hp(l_ka, v_s)
        yield
        pw = _dot_hp(l_ba, l_ba)
        yield
        t_inv = jnp.where(eye, 1.0, l_ba)
        steps = (C - 1).bit_length() - 1
        for step in range(1, steps + 1):
            if step < steps:
                both = _dot_hp(pw, jnp.concatenate([pw, t_inv], axis=1))
                pw, t_inv = both[:, :2 * C], t_inv + both[:, 2 * C:]
            else:
                t_inv = t_inv + _dot_hp(pw, t_inv)
            yield
        tx = _dot_hp(t_inv, jnp.concatenate([a_s, lkv], axis=1))
        yield
        a_hat, u_hat = tx[:, :LANES], tx[:, LANES:]
        u = _dot_hp(a_hat, s_t) + u_hat
        yield
        uv = jnp.concatenate([u, v_s], axis=0)
        ys = _dot(jnp.concatenate([r_s, m_br, m_kr], axis=1).astype(BF16),
                  jnp.concatenate([s_t, uv], axis=0).astype(BF16))
        yield
        s_new = s_t * gc.T + _dot_hp(jnp.concatenate([bh_s, kh_s], axis=0), uv, _TN)
        return ys[:C] + ys[C:], s_new

    def body(i, carry):
        pairs = [i * pair_unroll + j for j in range(pair_unroll)]
        gens = [compute_pair(load_pair(p)) for p in pairs]
        results = {}
        while len(results) < len(gens):
            for j, gen in enumerate(gens):
                if j not in results:
                    try:
                        next(gen)
                    except StopIteration as done:
                        results[j] = done.value
        for j, p in enumerate(pairs):
            y_scr[p], s_scr[p] = results[j]
        return carry

    lax.fori_loop(0, n_pairs // pair_unroll, body, 0)

    y = jnp.concatenate([y_scr[q] for q in range(n_pairs)], axis=1)
    out = _rwkv_finish(y, bonus, gate, lng_ref[...], lnb_ref[...], bsum_ref)
    ya_ref[...] = out.astype(BF16)

    @pl.when(c == n_chunks - 1)
    def _store_state():
        sout_ref[0] = s_scr[...]


def _wkv_prompt(p, n_b, seq, d_a, mu_rkv, mu_z, prm, lora, bsum):
    C = WKV_CHUNK
    n_chunks = seq // C
    n_pairs = d_a // LANES
    zb = (3 * d_a) // LORA_PAD
    row = lambda b, c: b * n_chunks + c
    vec = lambda w: pl.BlockSpec((1, w), lambda b, c: (0, 0))
    full = lambda s: pl.BlockSpec(s, lambda b, c: (0,) * len(s))
    tri = (lax.broadcasted_iota(jnp.int32, (C, C), 0) >= lax.broadcasted_iota(jnp.int32, (C, C), 1)).astype(BF16)
    tri = jnp.tile(tri, (1, 3))
    in_specs = [pl.BlockSpec((C, d_a), lambda b, c: (row(b, c), 0)),
                pl.BlockSpec((C, d_a), lambda b, c: (row(b, c), 1)),
                pl.BlockSpec((C, d_a), lambda b, c: (row(b, c), 2)),
                pl.BlockSpec((C, LORA_PAD), lambda b, c: (row(b, c), zb)),
                vec(d_a), vec(d_a), vec(d_a), vec(LORA_PAD)] + [vec(d_a)] * 7 + [
                full((LORA_PAD, d_a))] * 3 + [full((2 * LANES, LANES)), full((C, 3 * C))]
    scr = [pltpu.VMEM((n_pairs, LANES, LANES), F32),
           pltpu.VMEM((8, d_a), F32), pltpu.VMEM((8, d_a), F32), pltpu.VMEM((8, d_a), F32),
           pltpu.VMEM((8, LORA_PAD), F32)] + [pltpu.VMEM((n_pairs, C, LANES), F32)] * 8 + [
           pltpu.VMEM((n_pairs, 2 * C, LANES), F32)]
    pair_unroll = math.gcd(n_pairs, WKV_PAIR_UNROLL)
    ya, s_out = pl.pallas_call(
        functools.partial(_wkv_prompt_kernel, n_pairs=n_pairs, pair_unroll=pair_unroll),
        grid=(n_b, n_chunks),
        in_specs=in_specs,
        out_specs=[pl.BlockSpec((C, d_a), lambda b, c: (row(b, c), 0)),
                   pl.BlockSpec((1, n_pairs, LANES, LANES), lambda b, c: (b, 0, 0, 0))],
        out_shape=[jax.ShapeDtypeStruct((n_b * seq, d_a), BF16),
                   jax.ShapeDtypeStruct((n_b, n_pairs, LANES, LANES), F32)],
        scratch_shapes=scr,
        compiler_params=_cparams(("arbitrary", "arbitrary")),
        name="wkv_prompt",
    )(p, p, p, p, mu_rkv[0], mu_rkv[1], mu_rkv[2], mu_z, *prm, *lora, bsum, tri)
    return ya, s_out


def _wkv_sample_prep_kernel(pr_ref, pk_ref, pv_ref, pz_ref, sr_ref, sk_ref, sv_ref, sz_ref,
                            mur_ref, muk_ref, muv_ref, muz_ref,
                            w0_ref, a0_ref, kk_ref, ka_ref, rk_ref,
                            wd_ref, wa_ref, wg_ref, bsum_ref,
                            w_out, nkk_out, kka_out, k2_out, r_out, vt_out, bonus_out, gate_out):
    def shift(p_ref, s_ref, mu_ref):
        p = p_ref[...]
        return p + (s_ref[...] - p) * mu_ref[...]

    r = shift(pr_ref, sr_ref, mur_ref)
    k = shift(pk_ref, sk_ref, muk_ref)
    v = shift(pv_ref, sv_ref, muv_ref)
    z = shift(pz_ref, sz_ref, muz_ref)
    lw, a, gate, kk, k2, bonus = _rwkv_prep(
        r, k, v, z, w0_ref[...], a0_ref[...], kk_ref[...], ka_ref[...], rk_ref[...],
        wd_ref, wa_ref, wg_ref, bsum_ref)
    w_out[...] = jnp.exp(lw).T
    nkk_out[...] = (-kk).T
    kka_out[...] = (kk * a).T
    k2_out[...] = k2.T
    r_out[...] = r.T
    vt_out[...] = v.T
    bonus_out[...] = bonus
    gate_out[...] = gate


def _wkv_sample_step_kernel(s_ref, w_ref, nkk_ref, kka_ref, k2_ref, r_ref, vt_ref, snew_ref, yt_ref):
    def body(i, carry):
        s = s_ref[0, i]
        sa = jnp.sum(s * nkk_ref[...], axis=0, keepdims=True)
        s_new = s * w_ref[...] + sa * kka_ref[...] + vt_ref[pl.ds(i, 1), :] * k2_ref[...]
        snew_ref[0, i] = s_new
        yt_ref[pl.ds(i, 1), :] = jnp.sum(s_new * r_ref[...], axis=0, keepdims=True)
        return carry

    lax.fori_loop(0, HEAD_DIM, body, 0)


def _wkv_sample_finish_kernel(yt_ref, bonus_ref, gate_ref, lng_ref, lnb_ref, bsum_ref, ya_ref):
    y = yt_ref[...].T
    out = _rwkv_finish(y, bonus_ref[...], gate_ref[...], lng_ref[...], lnb_ref[...], bsum_ref)
    ya_ref[...] = out.astype(BF16)


def _wkv_sample(p, n_p, n_s, d_a, s_shift, state_hijb, mu_rkv, mu_z, prm, lora, bsum):
    assert n_p % n_s == 0
    rb = n_p // n_s
    zb = (3 * d_a) // LORA_PAD
    n_heads = d_a // HEAD_DIM
    w0, a0, k_k, k_a, r_k, ln_g, ln_b = prm
    full = lambda s: pl.BlockSpec(s, lambda i: (0,) * len(s))
    tok = jax.ShapeDtypeStruct((n_s, d_a), F32)
    outs = pl.pallas_call(
        _wkv_sample_prep_kernel,
        grid=(1,),
        in_specs=[pl.BlockSpec((n_s, d_a), lambda i: (rb, 0)),
                  pl.BlockSpec((n_s, d_a), lambda i: (rb, 1)),
                  pl.BlockSpec((n_s, d_a), lambda i: (rb, 2)),
                  pl.BlockSpec((n_s, LORA_PAD), lambda i: (rb, zb)),
                  full((n_s, d_a)), full((n_s, d_a)), full((n_s, d_a)), full((n_s, LORA_PAD)),
                  full((1, d_a)), full((1, d_a)), full((1, d_a)), full((1, LORA_PAD))]
                 + [full((1, d_a))] * 5 + [full((LORA_PAD, d_a))] * 3
                 + [full((2 * LANES, LANES))],
        out_specs=[full((d_a, n_s))] * 6 + [full((n_s, d_a))] * 2,
        out_shape=[jax.ShapeDtypeStruct((d_a, n_s), F32)] * 6 + [tok] * 2,
        compiler_params=_cparams(("arbitrary",)),
        name="wkv_sample_prep",
    )(p, p, p, p, *s_shift, mu_rkv[0], mu_rkv[1], mu_rkv[2], mu_z,
      w0, a0, k_k, k_a, r_k, *lora, bsum)
    w, nkk, kka, k2, r, vt, bonus, gate = outs
    headspec = pl.BlockSpec((HEAD_DIM, n_s), lambda h: (h, 0))
    stspec = pl.BlockSpec((1, HEAD_DIM, HEAD_DIM, n_s), lambda h: (h, 0, 0, 0))
    s_new, yt = pl.pallas_call(
        _wkv_sample_step_kernel,
        grid=(n_heads,),
        in_specs=[stspec] + [headspec] * 6,
        out_specs=[stspec, headspec],
        out_shape=[jax.ShapeDtypeStruct(state_hijb.shape, F32), jax.ShapeDtypeStruct((d_a, n_s), F32)],
        compiler_params=_cparams(("arbitrary",)),
        name="wkv_sample_step",
    )(state_hijb, w, nkk, kka, k2, r, vt)
    ya = pl.pallas_call(
        _wkv_sample_finish_kernel,
        grid=(1,),
        in_specs=[full((d_a, n_s)), full((n_s, d_a)), full((n_s, d_a)), full((1, d_a)), full((1, d_a)),
                  full((2 * LANES, LANES))],
        out_specs=full((n_s, d_a)),
        out_shape=jax.ShapeDtypeStruct((n_s, d_a), BF16),
        compiler_params=_cparams(("arbitrary",)),
        name="wkv_sample_finish",
    )(yt, bonus, gate, ln_g, ln_b, bsum)
    return ya, s_new


def _pool_prompt_kernel(u_ref, w_ref, sc_ref, o_ref, carry_scr, *, tt, gw):
    t = pl.program_id(1)

    @pl.when(t == 0)
    def _():
        carry_scr[...] = jnp.zeros_like(carry_scr)

    u = u_ref[...]
    pos = t * tt + lax.broadcasted_iota(jnp.int32, (tt, gw), 0)
    for gi, win in enumerate(POOL_WINDOWS):
        cols = slice(gi * gw, (gi + 1) * gw)
        ug = u[:, cols]
        cur = jnp.concatenate([carry_scr[:, cols], ug], axis=0)
        off = 0
        step = 1
        while step < win:
            cur = cur[step:] + cur[:-step]
            off += step
            step *= 2
        wsum = cur[POOL_CARRY - off:POOL_CARRY - off + tt]
        cnt = jnp.minimum(pos + 1, win).astype(F32)
        pooled = wsum / cnt - ug
        y = _dot(pooled.astype(BF16), w_ref[gi].astype(BF16)) * sc_ref[:, cols]
        o_ref[:, cols] = y.astype(BF16)
    carry_scr[...] = u[tt - POOL_CARRY:, :]


def _pool_prompt(p, n_b, seq, pw, w_pool, pool_scale):
    tt = _tile(seq, 256)
    nt = seq // tt
    gw = pw // len(POOL_WINDOWS)
    return pl.pallas_call(
        functools.partial(_pool_prompt_kernel, tt=tt, gw=gw),
        grid=(n_b, nt),
        in_specs=[pl.BlockSpec((tt, pw), lambda b, t: (b * nt + t, 0)),
                  pl.BlockSpec(w_pool.shape, lambda b, t: (0, 0, 0)),
                  pl.BlockSpec((1, pw), lambda b, t: (0, 0))],
        out_specs=pl.BlockSpec((tt, pw), lambda b, t: (b * nt + t, 0)),
        out_shape=jax.ShapeDtypeStruct((n_b * seq, pw), BF16),
        scratch_shapes=[pltpu.VMEM((POOL_CARRY, pw), F32)],
        compiler_params=_cparams(("arbitrary", "arbitrary")),
        name="pool_prompt",
    )(p, w_pool, pool_scale.reshape(1, pw))


def _pool_sample_kernel(u_ref, hist_ref, w_ref, sc_ref, o_ref, *, gw):
    u = u_ref[...]
    for gi, win in enumerate(POOL_WINDOWS):
        cols = slice(gi * gw, (gi + 1) * gw)
        ug = u[:, cols]
        wsum = ug
        for dback in range(1, win):
            wsum = wsum + hist_ref[POOL_BUF - dback, :, cols]
        pooled = wsum / float(win) - ug
        y = _dot(pooled.astype(BF16), w_ref[gi].astype(BF16)) * sc_ref[:, cols]
        o_ref[:, cols] = y.astype(BF16)


def _pool_sample(p, n_p, n_s, pw, hist_t, w_pool, pool_scale):
    gw = pw // len(POOL_WINDOWS)
    full = lambda s: pl.BlockSpec(s, lambda i: (0,) * len(s))
    return pl.pallas_call(
        functools.partial(_pool_sample_kernel, gw=gw),
        grid=(1,),
        in_specs=[pl.BlockSpec((n_s, pw), lambda i: (n_p // n_s, 0)),
                  full(hist_t.shape), full(w_pool.shape), full((1, pw))],
        out_specs=full((n_s, pw)),
        out_shape=jax.ShapeDtypeStruct((n_s, pw), BF16),
        compiler_params=_cparams(("arbitrary",)),
        name="pool_sample",
    )(p, hist_t, w_pool, pool_scale.reshape(1, pw))


def _router_kernel(x_ref, g_ref, wr_ref, h_ref, eid_ref, wt_ref):
    x = x_ref[...]
    h = x * lax.rsqrt(jnp.mean(x * x, axis=-1, keepdims=True) + RMS_EPS) * g_ref[...]
    h_ref[...] = h
    logits = _dot(h.astype(BF16), wr_ref[...])
    lane = lax.broadcasted_iota(jnp.int32, logits.shape, 1)
    neg = jnp.float32(-jnp.inf)
    big = jnp.int32(1 << 20)
    is_g = lane < N_GROUPS
    lg = jnp.where(is_g, logits, neg)
    mg = jnp.max(lg, axis=1, keepdims=True)
    g_sel = jnp.min(jnp.where(is_g & (lg == mg), lane, big), axis=1, keepdims=True)
    p_sel = 1.0 / jnp.sum(jnp.where(is_g, jnp.exp(lg - mg), 0.0), axis=1, keepdims=True)
    e_lane = lane - N_GROUPS
    in_grp = (e_lane >= 0) & (e_lane < N_EXPERTS) & ((e_lane >> 3) == g_sel)
    le = jnp.where(in_grp, logits, neg)
    m1 = jnp.max(le, axis=1, keepdims=True)
    i1 = jnp.min(jnp.where(in_grp & (le == m1), lane, big), axis=1, keepdims=True)
    le2 = jnp.where(lane == i1, neg, le)
    m2 = jnp.max(le2, axis=1, keepdims=True)
    i2 = jnp.min(jnp.where(in_grp & (lane != i1) & (le2 == m2), lane, big), axis=1, keepdims=True)
    e2 = jnp.exp(m2 - m1)
    w1 = p_sel / (1.0 + e2)
    w2 = p_sel * e2 / (1.0 + e2)
    eid_ref[...] = jnp.where(lane == 0, i1 - N_GROUPS, jnp.where(lane == 1, i2 - N_GROUPS, 0))
    wt_ref[...] = jnp.where(lane == 0, w1, jnp.where(lane == 1, w2, 0.0))


def _router(x1, g, w_router):
    n, d = x1.shape
    tr = _tile(n, 256)
    return pl.pallas_call(
        _router_kernel,
        grid=(n // tr,),
        in_specs=[pl.BlockSpec((tr, d), lambda i: (i, 0)),
                  pl.BlockSpec((1, d), lambda i: (0, 0)),
                  pl.BlockSpec((d, LANES), lambda i: (0, 0))],
        out_specs=[pl.BlockSpec((tr, d), lambda i: (i, 0)),
                   pl.BlockSpec((tr, LANES), lambda i: (i, 0)),
                   pl.BlockSpec((tr, LANES), lambda i: (i, 0))],
        out_shape=[jax.ShapeDtypeStruct((n, d), F32),
                   jax.ShapeDtypeStruct((n, LANES), jnp.int32),
                   jax.ShapeDtypeStruct((n, LANES), F32)],
        compiler_params=_cparams(("arbitrary",)),
        name="router",
    )(x1, g.reshape(1, d), w_router)


def _moe_kernel(ge_ref, gs_ref, gn_ref, tok_ref, dst_ref,
                h_hbm, wg_ref, wu_ref, wd_ref, out_hbm,
                xf_scr, xb_scr, acc_scr, sem_g, sem_s, *, rows, n_ftiles):
    g = pl.program_id(0)
    f = pl.program_id(1)
    nrows = gn_ref[g]
    start = gs_ref[g]

    def gather_copy(tok, i):
        return pltpu.make_async_copy(h_hbm.at[pl.ds(tok, 1)], xf_scr.at[pl.ds(i, 1)], sem_g)

    def scatter_copy(i, dst):
        return pltpu.make_async_copy(acc_scr.at[pl.ds(i, 1)], out_hbm.at[pl.ds(dst, 1)], sem_s)

    @pl.when(nrows > 0)
    def _group():
        @pl.when(f == 0)
        def _gather():
            def issue(i, carry):
                gather_copy(tok_ref[start + i], i).start()
                return carry
            lax.fori_loop(0, rows, issue, 0, unroll=8)
            pltpu.make_async_copy(h_hbm.at[pl.ds(0, rows)], xf_scr, sem_g).wait()
            xb_scr[...] = xf_scr[...].astype(BF16)
            acc_scr[...] = jnp.zeros_like(acc_scr)

        x = xb_scr[...]
        hg = _dot(x, wg_ref[0].astype(BF16))
        hu = _dot(x, wu_ref[0].astype(BF16))
        act = (hg * _sigmoid(hg) * hu).astype(BF16)
        acc_scr[...] += _dot(act, wd_ref[0].astype(BF16))

        @pl.when(f == n_ftiles - 1)
        def _scatter():
            def issue(i, carry):
                scatter_copy(i, dst_ref[start + i]).start()
                return carry
            lax.fori_loop(0, nrows, issue, 0)

            def wait(i, carry):
                scatter_copy(0, 0).wait()
                return carry
            lax.fori_loop(0, nrows, wait, 0)


def _moe(h2, eid, w_gate, w_up, w_down, rows, tf):
    n, d = h2.shape
    d_e = w_gate.shape[2]
    n_assign = n * TOP_K
    n_ftiles = d_e // tf
    max_groups = -(-n_assign // rows) + N_EXPERTS
    eflat = eid.reshape(-1)
    order = jnp.argsort(eflat, stable=True).astype(jnp.int32)
    tok_sorted = order // TOP_K
    dst_sorted = (order % TOP_K) * n + tok_sorted
    counts = jnp.zeros((N_EXPERTS,), jnp.int32).at[eflat].add(1)
    starts = jnp.cumsum(counts) - counts
    groups_per_e = (counts + rows - 1) // rows
    g_ends = jnp.cumsum(groups_per_e)
    gidx = jnp.arange(max_groups, dtype=jnp.int32)
    n_groups = g_ends[-1]
    g_e = jnp.minimum(jnp.searchsorted(g_ends, gidx, side="right"), N_EXPERTS - 1).astype(jnp.int32)
    local = gidx - (g_ends[g_e] - groups_per_e[g_e])
    g_start = starts[g_e] + local * rows
    g_n = jnp.clip(counts[g_e] - local * rows, 0, rows)
    valid = gidx < n_groups
    last_e = g_e[jnp.maximum(n_groups - 1, 0)]
    g_e = jnp.where(valid, g_e, last_e).astype(jnp.int32)
    g_start = jnp.where(valid, g_start, 0).astype(jnp.int32)
    g_n = jnp.where(valid, g_n, 0).astype(jnp.int32)
    pad = jnp.zeros((rows,), jnp.int32)
    tok_sorted = jnp.concatenate([tok_sorted, pad])
    dst_sorted = jnp.concatenate([dst_sorted, pad])

    def fsel(g, f, gn):
        return jnp.where(gn[g] > 0, f, n_ftiles - 1)

    grid_spec = pltpu.PrefetchScalarGridSpec(
        num_scalar_prefetch=5,
        grid=(max_groups, n_ftiles),
        in_specs=[pl.BlockSpec(memory_space=pl.ANY),
                  pl.BlockSpec((1, d, tf), lambda g, f, ge, gs, gn, tk, ds: (ge[g], 0, fsel(g, f, gn))),
                  pl.BlockSpec((1, d, tf), lambda g, f, ge, gs, gn, tk, ds: (ge[g], 0, fsel(g, f, gn))),
                  pl.BlockSpec((1, tf, d), lambda g, f, ge, gs, gn, tk, ds: (ge[g], fsel(g, f, gn), 0))],
        out_specs=pl.BlockSpec(memory_space=pl.ANY),
        scratch_shapes=[pltpu.VMEM((rows, d), F32), pltpu.VMEM((rows, d), BF16), pltpu.VMEM((rows, d), F32),
                        pltpu.SemaphoreType.DMA(()), pltpu.SemaphoreType.DMA(())],
    )
    return pl.pallas_call(
        functools.partial(_moe_kernel, rows=rows, n_ftiles=n_ftiles),
        grid_spec=grid_spec,
        out_shape=jax.ShapeDtypeStruct((TOP_K * n, d), F32),
        compiler_params=_cparams(("arbitrary", "arbitrary")),
        name="moe_experts",
    )(g_e, g_start, g_n, tok_sorted, dst_sorted, h2, w_gate, w_up, w_down)


def _combine_kernel(x_ref, y0_ref, y1_ref, wt_ref, g_ref, op_ref, os_ref, *, n_prompt_tiles):
    i = pl.program_id(0)
    wt = wt_ref[...]
    x = x_ref[...] + (y0_ref[...] * wt[:, 0:1] + y1_ref[...] * wt[:, 1:2])
    y = x * lax.rsqrt(jnp.mean(x * x, axis=-1, keepdims=True) + RMS_EPS) * g_ref[...]

    @pl.when(i < n_prompt_tiles)
    def _():
        op_ref[...] = y

    @pl.when(i >= n_prompt_tiles)
    def _():
        os_ref[...] = y


def _combine(x1, y2, wt, g, n_p, n_s):
    n, d = x1.shape
    tr = _tile(n_s, 256)
    npt, nst = n_p // tr, n_s // tr
    nt = npt + nst
    return pl.pallas_call(
        functools.partial(_combine_kernel, n_prompt_tiles=npt),
        grid=(nt,),
        in_specs=[pl.BlockSpec((tr, d), lambda i: (i, 0)),
                  pl.BlockSpec((tr, d), lambda i: (i, 0)),
                  pl.BlockSpec((tr, d), lambda i: (i + nt, 0)),
                  pl.BlockSpec((tr, LANES), lambda i: (i, 0)),
                  pl.BlockSpec((1, d), lambda i: (0, 0))],
        out_specs=[pl.BlockSpec((tr, d), lambda i: (jnp.minimum(i, npt - 1), 0)),
                   pl.BlockSpec((tr, d), lambda i: (jnp.maximum(i - npt, 0), 0))],
        out_shape=[jax.ShapeDtypeStruct((n_p, d), F32), jax.ShapeDtypeStruct((n_s, d), F32)],
        compiler_params=_cparams(("arbitrary",)),
        name="combine",
    )(x1, y2, y2, wt, g.reshape(1, d))


def _layer(xp, xs, st_wkv, st_shift, st_pool, norm_out_g, norm1_g, w_in, mu_shift, w0, w_decay_up, a0,
           w_iclr_up, w_gate_up, k_k, k_a, r_k, ln_x_g, ln_x_b, w_pool, pool_scale,
           w_branch_a, w_branch_b, w_out, norm2_g, w_router_group, w_router_expert,
           w_exp_gate, w_exp_up, w_exp_down):
    n_b, seq, d = xp.shape
    n_sb = xs.shape[0]
    n_p, n_s = n_b * seq, n_sb * xs.shape[1]
    n = n_p + n_s
    d_a = w_branch_a.shape[0]
    pw = w_branch_b.shape[0]
    sw = 3 * d_a + LORA_W
    assert d_a == pw and d == 2 * d_a and xs.shape[1] == 1 and w_in.shape[1] == sw + pw + 2 * d

    zpad = LORA_PAD - LORA_W
    w_in_t = w_in.T
    head_w = 3 * d_a + LORA_PAD
    pieces = lambda t: ([t[..., i * d_a:(i + 1) * d_a] for i in range(3)],
                        jnp.pad(t[..., 3 * d_a:sw], [(0, 0)] * (t.ndim - 1) + [(0, zpad)]))
    mu_rkv, mu_z = pieces(mu_shift.reshape(1, sw))
    ss_rkv, ss_z = pieces(st_shift)
    vec = lambda t: t.reshape(1, d_a)
    prm = (vec(w0), vec(a0), vec(k_k), vec(k_a), vec(r_k), vec(ln_x_g), vec(ln_x_b))
    lpad = lambda w, r0: jnp.zeros((LORA_PAD, d_a), F32).at[r0:r0 + w.shape[0]].set(w).astype(BF16)
    lora = (lpad(w_decay_up, 0), lpad(w_iclr_up, DECAY_RANK), lpad(w_gate_up, DECAY_RANK + ICLR_RANK))
    lane_head = jnp.arange(LANES, dtype=jnp.int32) // HEAD_DIM
    bsum = jnp.tile((lane_head[:, None] == lane_head[None, :]).astype(BF16), (2, 1))

    tm = _tile(n, 1664, 8)
    h, xcat = _norm1(xp.reshape(n_p, d), xs.reshape(n_s, d), norm1_g)
    tn = _tile(d, 256, LANES)
    p_head = _matmul_nt(h, w_in_t, 0, head_w, tm, tn, "in_proj_head")
    p_tail = _matmul_nt(h, w_in_t, sw, pw + 2 * d, tm, tn, "in_proj_tail")

    ya_p, s_pairs = _wkv_prompt(p_head, n_b, seq, d_a, mu_rkv, mu_z, prm, lora, bsum)
    ya_s, new_wkv_s = _wkv_sample(p_head, n_p, n_s, d_a, (*ss_rkv, ss_z), jnp.transpose(st_wkv, (1, 2, 3, 0)),
                                  mu_rkv, mu_z, prm, lora, bsum)
    yb_p = _pool_prompt(p_tail, n_b, seq, pw, w_pool, pool_scale)
    yb_s = _pool_sample(p_tail, n_p, n_s, pw, jnp.swapaxes(st_pool, 0, 1), w_pool, pool_scale)
    ya = jnp.concatenate([ya_p, ya_s], axis=0)
    yb = jnp.concatenate([yb_p, yb_s], axis=0)

    merged = _merge(ya, yb, w_branch_a, w_branch_b, p_tail, pw, d, tm, tn)
    x1 = _outproj(merged, w_out, xcat, tm, tn)

    w_router = jnp.concatenate([w_router_group, w_router_expert,
                                jnp.zeros((d, LANES - N_GROUPS - N_EXPERTS), F32)], axis=1).astype(BF16)
    h2, eid, wt = _router(x1, norm2_g, w_router)
    d_e = w_exp_gate.shape[2]
    y2 = _moe(h2, eid[:, :TOP_K], w_exp_gate, w_exp_up, w_exp_down,
              rows=512 if n * TOP_K >= 4096 else 64, tf=_tile(d_e, 256, LANES))
    y_p, y_s = _combine(x1, y2, wt, norm_out_g, n_p, n_s)

    n_heads = d_a // HEAD_DIM
    sp = s_pairs.reshape(n_b, n_heads // 2, 2, HEAD_DIM, 2, HEAD_DIM)
    new_wkv_p = jnp.stack([sp[:, :, 0, :, 0, :], sp[:, :, 1, :, 1, :]], axis=2)
    new_wkv_p = jnp.swapaxes(new_wkv_p.reshape(n_b, n_heads, HEAD_DIM, HEAD_DIM), -1, -2)
    new_wkv_s = jnp.transpose(new_wkv_s, (3, 0, 1, 2))
    new_shift_p = jnp.stack([p_head[b * seq + seq - 1, :sw] for b in range(n_b)])
    new_shift_s = p_head[n_p:, :sw]
    new_pool_p = jnp.stack([p_tail[b * seq + seq - POOL_BUF:(b + 1) * seq, :pw] for b in range(n_b)])
    new_pool_s = jnp.concatenate([st_pool[:, 1:], p_tail[n_p:, :pw][:, None, :]], axis=1)
    return (y_p.reshape(n_b, seq, d), y_s.reshape(n_sb, 1, d),
            new_wkv_p, new_shift_p, new_pool_p, new_wkv_s, new_shift_s, new_pool_s)


def kernel(x_prompt, x_sample, state_wkv, state_shift, state_pool, norm1_g, w_in, mu_shift, w0, w_decay_up, a0, w_iclr_up, w_gate_up, k_k, k_a, r_k, ln_x_g, ln_x_b, w_pool, pool_scale, w_branch_a, w_branch_b, w_out, norm2_g, w_router_group, w_router_expert, w_exp_gate, w_exp_up, w_exp_down, norm_f_g):
    assert norm1_g.shape[0] == 1, "single-layer trunk"
    outs = _layer(x_prompt, x_sample, state_wkv[0], state_shift[0], state_pool[0], norm_f_g,
                  norm1_g[0], w_in[0], mu_shift[0], w0[0], w_decay_up[0], a0[0], w_iclr_up[0], w_gate_up[0],
                  k_k[0], k_a[0], r_k[0], ln_x_g[0], ln_x_b[0], w_pool[0], pool_scale[0],
                  w_branch_a[0], w_branch_b[0], w_out[0], norm2_g[0], w_router_group[0], w_router_expert[0],
                  w_exp_gate[0], w_exp_up[0], w_exp_down[0])
    y_p, y_s = outs[0], outs[1]
    return (y_p, y_s) + tuple(o[None] for o in outs[2:])
```

```python
import functools
import math

import jax
import jax.numpy as jnp
from jax import lax
from jax.experimental import pallas as pl
from jax.experimental.pallas import tpu as pltpu

F32 = jnp.float32
BF16 = jnp.bfloat16

HEAD_DIM = 64
LANES = 128
DECAY_RANK = 96
ICLR_RANK = 96
GATE_RANK = 256
LORA_W = DECAY_RANK + ICLR_RANK + GATE_RANK
LORA_PAD = 512
POOL_WINDOWS = (2, 4, 8, 16)
POOL_BUF = 15
POOL_CARRY = 16
N_GROUPS = 4
EXPERTS_PER_GROUP = 8
N_EXPERTS = N_GROUPS * EXPERTS_PER_GROUP
TOP_K = 2
RMS_EPS = 1e-6
GN_EPS = 6.4e-4
L2_EPS = 1e-12
WKV_CHUNK = 64
SCATTER_BLOCK = 8
MOE_GROUP_ROWS = 608
WKV_PAIR_UNROLL = 8
VMEM_LIMIT = 56 * 1024 * 1024


def _cparams(sem, vmem=VMEM_LIMIT):
    return pltpu.CompilerParams(dimension_semantics=sem, vmem_limit_bytes=vmem)


def _tile(n, target, mult=8):
    best = None
    for t in range(mult, min(n, target) + 1, mult):
        if n % t == 0:
            best = t
    return best if best is not None else n


def _dot(a, b):
    return jnp.dot(a, b, preferred_element_type=F32)


def _sigmoid(x):
    return 0.5 * jnp.tanh(0.5 * x) + 0.5


def _norm1_kernel(xp_ref, xs_ref, g_ref, h_ref, xc_ref, *, n_prompt_tiles):
    i = pl.program_id(0)

    def body(x):
        xn = x * lax.rsqrt(jnp.mean(x * x, axis=-1, keepdims=True) + RMS_EPS)
        h_ref[...] = (xn * g_ref[...]).astype(BF16)
        xc_ref[...] = x

    @pl.when(i < n_prompt_tiles)
    def _():
        body(xp_ref[...])

    @pl.when(i >= n_prompt_tiles)
    def _():
        body(xs_ref[...])


def _norm1(xp, xs, g):
    n_p, d = xp.shape
    n_s = xs.shape[0]
    tr = _tile(n_s, 256)
    assert n_p % tr == 0
    npt, nst = n_p // tr, n_s // tr
    return pl.pallas_call(
        functools.partial(_norm1_kernel, n_prompt_tiles=npt),
        grid=(npt + nst,),
        in_specs=[pl.BlockSpec((tr, d), lambda i: (jnp.minimum(i, npt - 1), 0)),
                  pl.BlockSpec((tr, d), lambda i: (jnp.maximum(i - npt, 0), 0)),
                  pl.BlockSpec((1, d), lambda i: (0, 0))],
        out_specs=[pl.BlockSpec((tr, d), lambda i: (i, 0)),
                   pl.BlockSpec((tr, d), lambda i: (i, 0))],
        out_shape=[jax.ShapeDtypeStruct((n_p + n_s, d), BF16),
                   jax.ShapeDtypeStruct((n_p + n_s, d), F32)],
        compiler_params=_cparams(("arbitrary",)),
        name="norm1",
    )(xp, xs, g.reshape(1, d))


def _mm_nt_kernel(a_ref, wt_ref, o_ref):
    o_ref[...] = lax.dot_general(a_ref[...], wt_ref[...].astype(BF16), _NT, preferred_element_type=F32)


def _matmul_nt(a, wt, row0, n_out, tm, tn, name):
    m, k = a.shape
    assert row0 % 8 == 0
    return pl.pallas_call(
        _mm_nt_kernel,
        grid=(m // tm, n_out // tn),
        in_specs=[pl.BlockSpec((tm, k), lambda i, j: (i, 0)),
                  pl.BlockSpec((pl.Element(tn), pl.Element(k)), lambda i, j: (pl.multiple_of(row0 + j * tn, 8), 0))],
        out_specs=pl.BlockSpec((tm, tn), lambda i, j: (i, j)),
        out_shape=jax.ShapeDtypeStruct((m, n_out), F32),
        compiler_params=_cparams(("arbitrary", "arbitrary")),
        name=name,
    )(a, wt)


def _merge_kernel(ya_ref, yb_ref, wa_ref, wb_ref, ga_ref, gb_ref, o_ref):
    a = _dot(ya_ref[...], wa_ref[...].astype(BF16))
    b = _dot(yb_ref[...], wb_ref[...].astype(BF16))
    o_ref[...] = (_sigmoid(ga_ref[...]) * a + _sigmoid(gb_ref[...]) * b).astype(BF16)


def _merge(ya, yb, wa, wb, p, gate_col, d, tm, tn):
    m, k = ya.shape
    ga0 = gate_col // tn
    gb0 = (gate_col + d) // tn
    return pl.pallas_call(
        _merge_kernel,
        grid=(m // tm, d // tn),
        in_specs=[pl.BlockSpec((tm, k), lambda i, j: (i, 0)),
                  pl.BlockSpec((tm, k), lambda i, j: (i, 0)),
                  pl.BlockSpec((k, tn), lambda i, j: (0, j)),
                  pl.BlockSpec((k, tn), lambda i, j: (0, j)),
                  pl.BlockSpec((tm, tn), lambda i, j: (i, ga0 + j)),
                  pl.BlockSpec((tm, tn), lambda i, j: (i, gb0 + j))],
        out_specs=pl.BlockSpec((tm, tn), lambda i, j: (i, j)),
        out_shape=jax.ShapeDtypeStruct((m, d), BF16),
        compiler_params=_cparams(("arbitrary", "arbitrary")),
        name="merge",
    )(ya, yb, wa, wb, p, p)


def _outproj_kernel(m_ref, w_ref, x_ref, o_ref):
    o_ref[...] = x_ref[...] + _dot(m_ref[...], w_ref[...].astype(BF16))


def _outproj(merged, w, x, tm, tn):
    m, k = merged.shape
    n = w.shape[1]
    return pl.pallas_call(
        _outproj_kernel,
        grid=(m // tm, n // tn),
        in_specs=[pl.BlockSpec((tm, k), lambda i, j: (i, 0)),
                  pl.BlockSpec((k, tn), lambda i, j: (0, j)),
                  pl.BlockSpec((tm, tn), lambda i, j: (i, j))],
        out_specs=pl.BlockSpec((tm, tn), lambda i, j: (i, j)),
        out_shape=jax.ShapeDtypeStruct((m, n), F32),
        compiler_params=_cparams(("arbitrary", "arbitrary")),
        name="out_proj",
    )(merged, w, x)


def _split2(x):
    hi = x.astype(BF16)
    lo = (x - hi.astype(F32)).astype(BF16)
    return hi, lo


def _split3(x):
    hi = x.astype(BF16)
    r1 = x - hi.astype(F32)
    mid = r1.astype(BF16)
    lo = (r1 - mid.astype(F32)).astype(BF16)
    return hi, mid, lo


def _dot_hp(a, b, dims=(((1,), (0,)), ((), ()))):
    (ca,), (cb,) = dims[0]
    ah, al = _split2(a)
    bh = b.astype(BF16)
    return lax.dot_general(jnp.concatenate([ah, al], axis=ca), jnp.concatenate([bh, bh], axis=cb), dims,
                           preferred_element_type=F32)


_NT = (((1,), (1,)), ((), ()))
_TN = (((0,), (0,)), ((), ()))


def _seg_sum(x, bsum_ref):
    rows = x.shape[0]
    n_slabs = x.shape[1] // LANES
    xs = jnp.concatenate([x[:, q * LANES:(q + 1) * LANES] for q in range(n_slabs)], axis=0)
    hi, lo = _split2(xs)
    s = _dot(jnp.concatenate([hi, lo], axis=1), bsum_ref[...])
    return jnp.concatenate([s[q * rows:(q + 1) * rows] for q in range(n_slabs)], axis=1)


def _rwkv_prep(r, k, v, z, w0, a0, k_k, k_a, r_k, wd, wa, wg, bsum_ref):
    def lora_up(act, w_ref, first_row, n_rows):
        lo = first_row // LANES * LANES
        hi = -(-(first_row + n_rows) // LANES) * LANES
        return _dot(act(z[:, lo:hi]).astype(BF16), w_ref[lo:hi, :])

    lora_w = lora_up(jnp.tanh, wd, 0, DECAY_RANK)
    lora_a = lora_up(lambda t: t, wa, DECAY_RANK, ICLR_RANK)
    gate = lora_up(_sigmoid, wg, DECAY_RANK + ICLR_RANK, GATE_RANK)
    log_decay = -_sigmoid(w0 + lora_w) * math.exp(-0.5)
    a = _sigmoid(a0 + lora_a)
    kk = k * k_k
    kk = kk * jnp.minimum(lax.rsqrt(_seg_sum(kk * kk, bsum_ref)), 1.0 / L2_EPS)
    k2 = k * (1.0 + (a - 1.0) * k_a)
    bonus = _seg_sum(r * k2 * r_k, bsum_ref) * v
    return log_decay, a, gate, kk, k2, bonus


def _rwkv_finish(y, bonus, gate, ln_g, ln_b, bsum_ref):
    inv = 1.0 / HEAD_DIM
    mu = _seg_sum(y, bsum_ref) * inv
    yc = y - mu
    var = _seg_sum(yc * yc, bsum_ref) * inv
    yn = yc * lax.rsqrt(var + GN_EPS) * ln_g + ln_b
    return (yn + bonus) * gate


def _wkv_prompt_kernel(pr_ref, pk_ref, pv_ref, pz_ref, mur_ref, muk_ref, muv_ref, muz_ref,
                       w0_ref, a0_ref, kk_ref, ka_ref, rk_ref, lng_ref, lnb_ref,
                       wd_ref, wa_ref, wg_ref, bsum_ref, tri_ref,
                       ya_ref, sout_ref,
                       s_scr, cr_scr, ck_scr, cv_scr, cz_scr,
                       at_scr, bt_scr, kt_scr, rt_scr, bh_scr, kh_scr, v_scr, y_scr, gc_scr,
                       *, n_pairs, pair_unroll):
    c = pl.program_id(1)
    n_chunks = pl.num_programs(1)
    C = WKV_CHUNK

    @pl.when(c == 0)
    def _init():
        s_scr[...] = jnp.zeros_like(s_scr)
        cr_scr[...] = jnp.zeros_like(cr_scr)
        ck_scr[...] = jnp.zeros_like(ck_scr)
        cv_scr[...] = jnp.zeros_like(cv_scr)
        cz_scr[...] = jnp.zeros_like(cz_scr)

    def shift(p_ref, carry, mu_ref):
        p = p_ref[...]
        prev = pltpu.roll(p, 1, axis=0)
        row = lax.broadcasted_iota(jnp.int32, p.shape, 0)
        prev = jnp.where(row == 0, carry[0:1, :], prev)
        carry[0:1, :] = p[C - 1:C, :]
        return p + (prev - p) * mu_ref[...]

    r = shift(pr_ref, cr_scr, mur_ref)
    k = shift(pk_ref, ck_scr, muk_ref)
    v = shift(pv_ref, cv_scr, muv_ref)
    z = shift(pz_ref, cz_scr, muz_ref)
    lw, a, gate, kk, k2, bonus = _rwkv_prep(
        r, k, v, z, w0_ref[...], a0_ref[...], kk_ref[...], ka_ref[...], rk_ref[...],
        wd_ref, wa_ref, wg_ref, bsum_ref)

    cl = _dot(tri_ref[...], jnp.concatenate(_split3(lw), axis=0))
    cl_end = cl[C - 1:C, :]
    beta = kk * a
    e_neg = jnp.exp(-cl)
    e_hat = jnp.exp(cl_end - cl)
    def to_pairs(scr, x):
        for q in range(n_pairs):
            scr[q] = x[:, q * LANES:(q + 1) * LANES]

    to_pairs(at_scr, -kk * jnp.exp(cl - lw))
    to_pairs(bt_scr, beta * e_neg)
    to_pairs(kt_scr, k2 * e_neg)
    to_pairs(rt_scr, r * jnp.exp(cl))
    to_pairs(bh_scr, beta * e_hat)
    to_pairs(kh_scr, k2 * e_hat)
    to_pairs(v_scr, v)
    to_pairs(gc_scr, jnp.broadcast_to(jnp.exp(cl_end), (2 * C, cl.shape[1])))

    lane = lax.broadcasted_iota(jnp.int32, (C, LANES), 1)
    first = lane < HEAD_DIM
    row2 = lax.broadcasted_iota(jnp.int32, (2 * C, 2 * C), 0)
    col2 = lax.broadcasted_iota(jnp.int32, (2 * C, 2 * C), 1)
    cbits = C.bit_length() - 1
    same = (row2 >> cbits) == (col2 >> cbits)
    tri_strict = same & ((row2 & (C - 1)) > (col2 & (C - 1)))
    tri_incl = same & ((row2 & (C - 1)) >= (col2 & (C - 1)))
    eye = row2 == col2

    def stack(x):
        return jnp.concatenate([jnp.where(first, x, 0.0), jnp.where(first, 0.0, x)], axis=0)

    def load_pair(p):
        return (at_scr[p], bt_scr[p], kt_scr[p], rt_scr[p], v_scr[p], bh_scr[p], kh_scr[p], gc_scr[p], s_scr[p])

    def compute_pair(vals):
        a_s, b_s, k_s, r_s, v_s, bh_s, kh_s = (stack(x) for x in vals[:7])
        gc, s_t = vals[7], vals[8]
        bk = jnp.concatenate([b_s, k_s], axis=0)
        ga = _dot_hp(a_s, bk, _NT)
        yield
        gr = lax.dot_general(r_s.astype(BF16), bk.astype(BF16), _NT, preferred_element_type=F32)
        yield
        l_ba = jnp.where(tri_strict, ga[:, :2 * C], 0.0)
        l_ka = jnp.where(tri_strict, ga[:, 2 * C:], 0.0)
        m_br = jnp.where(tri_incl, gr[:, :2 * C], 0.0)
        m_kr = jnp.where(tri_incl, gr[:, 2 * C:], 0.0)
        lkv = _dot_hp(l_ka, v_s)
        yield
        pw = _dot_hp(l_ba, l_ba)
        yield
        t_inv = jnp.where(eye, 1.0, l_ba)
        steps = (C - 1).bit_length() - 1
        for step in range(1, steps + 1):
            if step < steps:
                both = _dot_hp(pw, jnp.concatenate([pw, t_inv], axis=1))
                pw, t_inv = both[:, :2 * C], t_inv + both[:, 2 * C:]
            else:
                t_inv = t_inv + _dot_hp(pw, t_inv)
            yield
        tx = _dot_hp(t_inv, jnp.concatenate([a_s, lkv], axis=1))
        yield
        a_hat, u_hat = tx[:, :LANES], tx[:, LANES:]
        u = _dot_hp(a_hat, s_t) + u_hat
        yield
        uv = jnp.concatenate([u, v_s], axis=0)
        ys = _dot(jnp.concatenate([r_s, m_br, m_kr], axis=1).astype(BF16),
                  jnp.concatenate([s_t, uv], axis=0).astype(BF16))
        yield
        s_new = s_t * gc.T + _dot_hp(jnp.concatenate([bh_s, kh_s], axis=0), uv, _TN)
        return ys[:C] + ys[C:], s_new

    def body(i, carry):
        pairs = [i * pair_unroll + j for j in range(pair_unroll)]
        gens = [compute_pair(load_pair(p)) for p in pairs]
        results = {}
        while len(results) < len(gens):
            for j, gen in enumerate(gens):
                if j not in results:
                    try:
                        next(gen)
                    except StopIteration as done:
                        results[j] = done.value
        for j, p in enumerate(pairs):
            y_scr[p], s_scr[p] = results[j]
        return carry

    lax.fori_loop(0, n_pairs // pair_unroll, body, 0)

    y = jnp.concatenate([y_scr[q] for q in range(n_pairs)], axis=1)
    out = _rwkv_finish(y, bonus, gate, lng_ref[...], lnb_ref[...], bsum_ref)
    ya_ref[...] = out.astype(BF16)

    @pl.when(c == n_chunks - 1)
    def _store_state():
        sout_ref[0] = s_scr[...]


def _wkv_prompt(p, n_b, seq, d_a, mu_rkv, mu_z, prm, lora, bsum):
    C = WKV_CHUNK
    n_chunks = seq // C
    n_pairs = d_a // LANES
    zb = (3 * d_a) // LORA_PAD
    row = lambda b, c: b * n_chunks + c
    vec = lambda w: pl.BlockSpec((1, w), lambda b, c: (0, 0))
    full = lambda s: pl.BlockSpec(s, lambda b, c: (0,) * len(s))
    tri = (lax.broadcasted_iota(jnp.int32, (C, C), 0) >= lax.broadcasted_iota(jnp.int32, (C, C), 1)).astype(BF16)
    tri = jnp.tile(tri, (1, 3))
    in_specs = [pl.BlockSpec((C, d_a), lambda b, c: (row(b, c), 0)),
                pl.BlockSpec((C, d_a), lambda b, c: (row(b, c), 1)),
                pl.BlockSpec((C, d_a), lambda b, c: (row(b, c), 2)),
                pl.BlockSpec((C, LORA_PAD), lambda b, c: (row(b, c), zb)),
                vec(d_a), vec(d_a), vec(d_a), vec(LORA_PAD)] + [vec(d_a)] * 7 + [
                full((LORA_PAD, d_a))] * 3 + [full((2 * LANES, LANES)), full((C, 3 * C))]
    scr = [pltpu.VMEM((n_pairs, LANES, LANES), F32),
           pltpu.VMEM((8, d_a), F32), pltpu.VMEM((8, d_a), F32), pltpu.VMEM((8, d_a), F32),
           pltpu.VMEM((8, LORA_PAD), F32)] + [pltpu.VMEM((n_pairs, C, LANES), F32)] * 8 + [
           pltpu.VMEM((n_pairs, 2 * C, LANES), F32)]
    pair_unroll = math.gcd(n_pairs, WKV_PAIR_UNROLL)
    ya, s_out = pl.pallas_call(
        functools.partial(_wkv_prompt_kernel, n_pairs=n_pairs, pair_unroll=pair_unroll),
        grid=(n_b, n_chunks),
        in_specs=in_specs,
        out_specs=[pl.BlockSpec((C, d_a), lambda b, c: (row(b, c), 0)),
                   pl.BlockSpec((1, n_pairs, LANES, LANES), lambda b, c: (b, 0, 0, 0))],
        out_shape=[jax.ShapeDtypeStruct((n_b * seq, d_a), BF16),
                   jax.ShapeDtypeStruct((n_b, n_pairs, LANES, LANES), F32)],
        scratch_shapes=scr,
        compiler_params=_cparams(("arbitrary", "arbitrary")),
        name="wkv_prompt",
    )(p, p, p, p, mu_rkv[0], mu_rkv[1], mu_rkv[2], mu_z, *prm, *lora, bsum, tri)
    return ya, s_out


def _wkv_sample_prep_kernel(pr_ref, pk_ref, pv_ref, pz_ref, sr_ref, sk_ref, sv_ref, sz_ref,
                            mur_ref, muk_ref, muv_ref, muz_ref,
                            w0_ref, a0_ref, kk_ref, ka_ref, rk_ref,
                            wd_ref, wa_ref, wg_ref, bsum_ref,
                            w_out, nkk_out, kka_out, k2_out, r_out, vt_out, bonus_out, gate_out):
    def shift(p_ref, s_ref, mu_ref):
        p = p_ref[...]
        return p + (s_ref[...] - p) * mu_ref[...]

    r = shift(pr_ref, sr_ref, mur_ref)
    k = shift(pk_ref, sk_ref, muk_ref)
    v = shift(pv_ref, sv_ref, muv_ref)
    z = shift(pz_ref, sz_ref, muz_ref)
    lw, a, gate, kk, k2, bonus = _rwkv_prep(
        r, k, v, z, w0_ref[...], a0_ref[...], kk_ref[...], ka_ref[...], rk_ref[...],
        wd_ref, wa_ref, wg_ref, bsum_ref)
    w_out[...] = jnp.exp(lw).T
    nkk_out[...] = (-kk).T
    kka_out[...] = (kk * a).T
    k2_out[...] = k2.T
    r_out[...] = r.T
    vt_out[...] = v.T
    bonus_out[...] = bonus
    gate_out[...] = gate


def _wkv_sample_step_kernel(s_ref, w_ref, nkk_ref, kka_ref, k2_ref, r_ref, vt_ref, snew_ref, yt_ref):
    def body(i, carry):
        s = s_ref[0, i]
        sa = jnp.sum(s * nkk_ref[...], axis=0, keepdims=True)
        s_new = s * w_ref[...] + sa * kka_ref[...] + vt_ref[pl.ds(i, 1), :] * k2_ref[...]
        snew_ref[0, i] = s_new
        yt_ref[pl.ds(i, 1), :] = jnp.sum(s_new * r_ref[...], axis=0, keepdims=True)
        return carry

    lax.fori_loop(0, HEAD_DIM, body, 0)


def _wkv_sample_finish_kernel(yt_ref, bonus_ref, gate_ref, lng_ref, lnb_ref, bsum_ref, ya_ref):
    y = yt_ref[...].T
    out = _rwkv_finish(y, bonus_ref[...], gate_ref[...], lng_ref[...], lnb_ref[...], bsum_ref)
    ya_ref[...] = out.astype(BF16)


def _wkv_sample(p, n_p, n_s, d_a, s_shift, state_hijb, mu_rkv, mu_z, prm, lora, bsum):
    assert n_p % n_s == 0
    rb = n_p // n_s
    zb = (3 * d_a) // LORA_PAD
    n_heads = d_a // HEAD_DIM
    w0, a0, k_k, k_a, r_k, ln_g, ln_b = prm
    full = lambda s: pl.BlockSpec(s, lambda i: (0,) * len(s))
    tok = jax.ShapeDtypeStruct((n_s, d_a), F32)
    outs = pl.pallas_call(
        _wkv_sample_prep_kernel,
        grid=(1,),
        in_specs=[pl.BlockSpec((n_s, d_a), lambda i: (rb, 0)),
                  pl.BlockSpec((n_s, d_a), lambda i: (rb, 1)),
                  pl.BlockSpec((n_s, d_a), lambda i: (rb, 2)),
                  pl.BlockSpec((n_s, LORA_PAD), lambda i: (rb, zb)),
                  full((n_s, d_a)), full((n_s, d_a)), full((n_s, d_a)), full((n_s, LORA_PAD)),
                  full((1, d_a)), full((1, d_a)), full((1, d_a)), full((1, LORA_PAD))]
                 + [full((1, d_a))] * 5 + [full((LORA_PAD, d_a))] * 3
                 + [full((2 * LANES, LANES))],
        out_specs=[full((d_a, n_s))] * 6 + [full((n_s, d_a))] * 2,
        out_shape=[jax.ShapeDtypeStruct((d_a, n_s), F32)] * 6 + [tok] * 2,
        compiler_params=_cparams(("arbitrary",)),
        name="wkv_sample_prep",
    )(p, p, p, p, *s_shift, mu_rkv[0], mu_rkv[1], mu_rkv[2], mu_z,
      w0, a0, k_k, k_a, r_k, *lora, bsum)
    w, nkk, kka, k2, r, vt, bonus, gate = outs
    headspec = pl.BlockSpec((HEAD_DIM, n_s), lambda h: (h, 0))
    stspec = pl.BlockSpec((1, HEAD_DIM, HEAD_DIM, n_s), lambda h: (h, 0, 0, 0))
    s_new, yt = pl.pallas_call(
        _wkv_sample_step_kernel,
        grid=(n_heads,),
        in_specs=[stspec] + [headspec] * 6,
        out_specs=[stspec, headspec],
        out_shape=[jax.ShapeDtypeStruct(state_hijb.shape, F32), jax.ShapeDtypeStruct((d_a, n_s), F32)],
        compiler_params=_cparams(("arbitrary",)),
        name="wkv_sample_step",
    )(state_hijb, w, nkk, kka, k2, r, vt)
    ya = pl.pallas_call(
        _wkv_sample_finish_kernel,
        grid=(1,),
        in_specs=[full((d_a, n_s)), full((n_s, d_a)), full((n_s, d_a)), full((1, d_a)), full((1, d_a)),
                  full((2 * LANES, LANES))],
        out_specs=full((n_s, d_a)),
        out_shape=jax.ShapeDtypeStruct((n_s, d_a), BF16),
        compiler_params=_cparams(("arbitrary",)),
        name="wkv_sample_finish",
    )(yt, bonus, gate, ln_g, ln_b, bsum)
    return ya, s_new


def _pool_prompt_kernel(u_ref, w_ref, sc_ref, o_ref, carry_scr, *, tt, gw):
    t = pl.program_id(1)

    @pl.when(t == 0)
    def _():
        carry_scr[...] = jnp.zeros_like(carry_scr)

    u = u_ref[...]
    pos = t * tt + lax.broadcasted_iota(jnp.int32, (tt, gw), 0)
    for gi, win in enumerate(POOL_WINDOWS):
        cols = slice(gi * gw, (gi + 1) * gw)
        ug = u[:, cols]
        cur = jnp.concatenate([carry_scr[:, cols], ug], axis=0)
        off = 0
        step = 1
        while step < win:
            cur = cur[step:] + cur[:-step]
            off += step
            step *= 2
        wsum = cur[POOL_CARRY - off:POOL_CARRY - off + tt]
        cnt = jnp.minimum(pos + 1, win).astype(F32)
        pooled = wsum / cnt - ug
        y = _dot(pooled.astype(BF16), w_ref[gi].astype(BF16)) * sc_ref[:, cols]
        o_ref[:, cols] = y.astype(BF16)
    carry_scr[...] = u[tt - POOL_CARRY:, :]


def _pool_prompt(p, n_b, seq, pw, w_pool, pool_scale):
    tt = _tile(seq, 256)
    nt = seq // tt
    gw = pw // len(POOL_WINDOWS)
    return pl.pallas_call(
        functools.partial(_pool_prompt_kernel, tt=tt, gw=gw),
        grid=(n_b, nt),
        in_specs=[pl.BlockSpec((tt, pw), lambda b, t: (b * nt + t, 0)),
                  pl.BlockSpec(w_pool.shape, lambda b, t: (0, 0, 0)),
                  pl.BlockSpec((1, pw), lambda b, t: (0, 0))],
        out_specs=pl.BlockSpec((tt, pw), lambda b, t: (b * nt + t, 0)),
        out_shape=jax.ShapeDtypeStruct((n_b * seq, pw), BF16),
        scratch_shapes=[pltpu.VMEM((POOL_CARRY, pw), F32)],
        compiler_params=_cparams(("arbitrary", "arbitrary")),
        name="pool_prompt",
    )(p, w_pool, pool_scale.reshape(1, pw))


def _pool_sample_kernel(u_ref, hist_ref, w_ref, sc_ref, o_ref, *, gw):
    u = u_ref[...]
    for gi, win in enumerate(POOL_WINDOWS):
        cols = slice(gi * gw, (gi + 1) * gw)
        ug = u[:, cols]
        wsum = ug
        for dback in range(1, win):
            wsum = wsum + hist_ref[POOL_BUF - dback, :, cols]
        pooled = wsum / float(win) - ug
        y = _dot(pooled.astype(BF16), w_ref[gi].astype(BF16)) * sc_ref[:, cols]
        o_ref[:, cols] = y.astype(BF16)


def _pool_sample(p, n_p, n_s, pw, hist_t, w_pool, pool_scale):
    gw = pw // len(POOL_WINDOWS)
    full = lambda s: pl.BlockSpec(s, lambda i: (0,) * len(s))
    return pl.pallas_call(
        functools.partial(_pool_sample_kernel, gw=gw),
        grid=(1,),
        in_specs=[pl.BlockSpec((n_s, pw), lambda i: (n_p // n_s, 0)),
                  full(hist_t.shape), full(w_pool.shape), full((1, pw))],
        out_specs=full((n_s, pw)),
        out_shape=jax.ShapeDtypeStruct((n_s, pw), BF16),
        compiler_params=_cparams(("arbitrary",)),
        name="pool_sample",
    )(p, hist_t, w_pool, pool_scale.reshape(1, pw))


def _router_kernel(x_ref, g_ref, wr_ref, h_ref, eid_ref, wt_ref):
    x = x_ref[...]
    h = x * lax.rsqrt(jnp.mean(x * x, axis=-1, keepdims=True) + RMS_EPS) * g_ref[...]
    h_ref[...] = h
    logits = _dot(h.astype(BF16), wr_ref[...])
    lane = lax.broadcasted_iota(jnp.int32, logits.shape, 1)
    neg = jnp.float32(-jnp.inf)
    big = jnp.int32(1 << 20)
    is_g = lane < N_GROUPS
    lg = jnp.where(is_g, logits, neg)
    mg = jnp.max(lg, axis=1, keepdims=True)
    g_sel = jnp.min(jnp.where(is_g & (lg == mg), lane, big), axis=1, keepdims=True)
    p_sel = 1.0 / jnp.sum(jnp.where(is_g, jnp.exp(lg - mg), 0.0), axis=1, keepdims=True)
    e_lane = lane - N_GROUPS
    in_grp = (e_lane >= 0) & (e_lane < N_EXPERTS) & ((e_lane >> 3) == g_sel)
    le = jnp.where(in_grp, logits, neg)
    m1 = jnp.max(le, axis=1, keepdims=True)
    i1 = jnp.min(jnp.where(in_grp & (le == m1), lane, big), axis=1, keepdims=True)
    le2 = jnp.where(lane == i1, neg, le)
    m2 = jnp.max(le2, axis=1, keepdims=True)
    i2 = jnp.min(jnp.where(in_grp & (lane != i1) & (le2 == m2), lane, big), axis=1, keepdims=True)
    e2 = jnp.exp(m2 - m1)
    w1 = p_sel / (1.0 + e2)
    w2 = p_sel * e2 / (1.0 + e2)
    eid_ref[...] = jnp.where(lane == 0, i1 - N_GROUPS, jnp.where(lane == 1, i2 - N_GROUPS, 0))
    wt_ref[...] = jnp.where(lane == 0, w1, jnp.where(lane == 1, w2, 0.0))


def _router(x1, g, w_router):
    n, d = x1.shape
    tr = _tile(n, 256)
    return pl.pallas_call(
        _router_kernel,
        grid=(n // tr,),
        in_specs=[pl.BlockSpec((tr, d), lambda i: (i, 0)),
                  pl.BlockSpec((1, d), lambda i: (0, 0)),
                  pl.BlockSpec((d, LANES), lambda i: (0, 0))],
        out_specs=[pl.BlockSpec((tr, d), lambda i: (i, 0)),
                   pl.BlockSpec((tr, LANES), lambda i: (i, 0)),
                   pl.BlockSpec((tr, LANES), lambda i: (i, 0))],
        out_shape=[jax.ShapeDtypeStruct((n, d), F32),
                   jax.ShapeDtypeStruct((n, LANES), jnp.int32),
                   jax.ShapeDtypeStruct((n, LANES), F32)],
        compiler_params=_cparams(("arbitrary",)),
        name="router",
    )(x1, g.reshape(1, d), w_router)


def _moe_kernel(ge_ref, gs_ref, gn_ref, tok_ref, dst_ref,
                h_hbm, wg_ref, wu_ref, wd_ref, out_hbm,
                xf_scr, xb_scr, acc_scr, sem_g, sem_s, *, rows, n_ftiles):
    g = pl.program_id(0)
    f = pl.program_id(1)
    nrows = gn_ref[g]
    start = gs_ref[g]

    def gather_copy(tok, i):
        return pltpu.make_async_copy(h_hbm.at[pl.ds(tok, 1)], xf_scr.at[pl.ds(i, 1)], sem_g)

    def scatter_copy(i, dst):
        return pltpu.make_async_copy(acc_scr.at[pl.ds(i, 1)], out_hbm.at[pl.ds(dst, 1)], sem_s)

    @pl.when(nrows > 0)
    def _group():
        def issue_gather(first):
            def issue(i, carry):
                gather_copy(tok_ref[first + i], i).start()
                return carry
            lax.fori_loop(0, rows, issue, 0, unroll=8)

        @pl.when(f == 0)
        def _rows_in():
            @pl.when(g == 0)
            def _():
                issue_gather(start)
            pltpu.make_async_copy(h_hbm.at[pl.ds(0, rows)], xf_scr, sem_g).wait()
            xb_scr[...] = xf_scr[...].astype(BF16)
            acc_scr[...] = jnp.zeros_like(acc_scr)

        @pl.when((f == 1) & (gn_ref[g + 1] > 0))
        def _prefetch():
            issue_gather(gs_ref[g + 1])

        x = xb_scr[...]
        hg = _dot(x, wg_ref[0].astype(BF16))
        hu = _dot(x, wu_ref[0].astype(BF16))
        act = (hg * _sigmoid(hg) * hu).astype(BF16)
        acc_scr[...] += _dot(act, wd_ref[0].astype(BF16))

        @pl.when(f == n_ftiles - 1)
        def _scatter():
            nblk = nrows // SCATTER_BLOCK

            def issue_block(b, carry):
                for j in range(SCATTER_BLOCK):
                    i = b * SCATTER_BLOCK + j
                    scatter_copy(i, dst_ref[start + i]).start()
                return carry
            lax.fori_loop(0, nblk, issue_block, 0)

            def issue_one(i, carry):
                scatter_copy(i, dst_ref[start + i]).start()
                return carry
            lax.fori_loop(nblk * SCATTER_BLOCK, nrows, issue_one, 0)

            def wait_block(b, carry):
                pltpu.make_async_copy(acc_scr.at[pl.ds(0, SCATTER_BLOCK)], out_hbm.at[pl.ds(0, SCATTER_BLOCK)],
                                      sem_s).wait()
                return carry
            lax.fori_loop(0, nblk, wait_block, 0)

            def wait_one(i, carry):
                scatter_copy(0, 0).wait()
                return carry
            lax.fori_loop(nblk * SCATTER_BLOCK, nrows, wait_one, 0)


def _moe(h2, eid, w_gate, w_up, w_down, rows, tf):
    n, d = h2.shape
    d_e = w_gate.shape[2]
    n_assign = n * TOP_K
    n_ftiles = d_e // tf
    assert n_ftiles >= 2
    max_groups = -(-n_assign // rows) + N_EXPERTS
    eflat = eid.reshape(-1)
    order = jnp.argsort(eflat, stable=True).astype(jnp.int32)
    tok_sorted = order // TOP_K
    dst_sorted = (order % TOP_K) * n + tok_sorted
    counts = jnp.zeros((N_EXPERTS,), jnp.int32).at[eflat].add(1)
    starts = jnp.cumsum(counts) - counts
    groups_per_e = (counts + rows - 1) // rows
    g_ends = jnp.cumsum(groups_per_e)
    gidx = jnp.arange(max_groups, dtype=jnp.int32)
    n_groups = g_ends[-1]
    g_e = jnp.minimum(jnp.searchsorted(g_ends, gidx, side="right"), N_EXPERTS - 1).astype(jnp.int32)
    local = gidx - (g_ends[g_e] - groups_per_e[g_e])
    g_start = starts[g_e] + local * rows
    g_n = jnp.clip(counts[g_e] - local * rows, 0, rows)
    valid = gidx < n_groups
    last_e = g_e[jnp.maximum(n_groups - 1, 0)]
    g_e = jnp.where(valid, g_e, last_e).astype(jnp.int32)
    one = jnp.zeros((1,), jnp.int32)
    g_start = jnp.concatenate([jnp.where(valid, g_start, 0).astype(jnp.int32), one])
    g_n = jnp.concatenate([jnp.where(valid, g_n, 0).astype(jnp.int32), one])
    pad = jnp.zeros((rows,), jnp.int32)
    tok_sorted = jnp.concatenate([tok_sorted, pad])
    dst_sorted = jnp.concatenate([dst_sorted, pad])

    def fsel(g, f, gn):
        return jnp.where(gn[g] > 0, f, n_ftiles - 1)

    grid_spec = pltpu.PrefetchScalarGridSpec(
        num_scalar_prefetch=5,
        grid=(max_groups, n_ftiles),
        in_specs=[pl.BlockSpec(memory_space=pl.ANY),
                  pl.BlockSpec((1, d, tf), lambda g, f, ge, gs, gn, tk, ds: (ge[g], 0, fsel(g, f, gn))),
                  pl.BlockSpec((1, d, tf), lambda g, f, ge, gs, gn, tk, ds: (ge[g], 0, fsel(g, f, gn))),
                  pl.BlockSpec((1, tf, d), lambda g, f, ge, gs, gn, tk, ds: (ge[g], fsel(g, f, gn), 0))],
        out_specs=pl.BlockSpec(memory_space=pl.ANY),
        scratch_shapes=[pltpu.VMEM((rows, d), F32), pltpu.VMEM((rows, d), BF16), pltpu.VMEM((rows, d), F32),
                        pltpu.SemaphoreType.DMA(()), pltpu.SemaphoreType.DMA(())],
    )
    return pl.pallas_call(
        functools.partial(_moe_kernel, rows=rows, n_ftiles=n_ftiles),
        grid_spec=grid_spec,
        out_shape=jax.ShapeDtypeStruct((TOP_K * n, d), F32),
        compiler_params=_cparams(("arbitrary", "arbitrary")),
        name="moe_experts",
    )(g_e, g_start, g_n, tok_sorted, dst_sorted, h2, w_gate, w_up, w_down)


def _combine_kernel(x_ref, y0_ref, y1_ref, wt_ref, g_ref, op_ref, os_ref, *, n_prompt_tiles):
    i = pl.program_id(0)
    wt = wt_ref[...]
    x = x_ref[...] + (y0_ref[...] * wt[:, 0:1] + y1_ref[...] * wt[:, 1:2])
    y = x * lax.rsqrt(jnp.mean(x * x, axis=-1, keepdims=True) + RMS_EPS) * g_ref[...]

    @pl.when(i < n_prompt_tiles)
    def _():
        op_ref[...] = y

    @pl.when(i >= n_prompt_tiles)
    def _():
        os_ref[...] = y


def _combine(x1, y2, wt, g, n_p, n_s):
    n, d = x1.shape
    tr = _tile(n_s, 256)
    npt, nst = n_p // tr, n_s // tr
    nt = npt + nst
    return pl.pallas_call(
        functools.partial(_combine_kernel, n_prompt_tiles=npt),
        grid=(nt,),
        in_specs=[pl.BlockSpec((tr, d), lambda i: (i, 0)),
                  pl.BlockSpec((tr, d), lambda i: (i, 0)),
                  pl.BlockSpec((tr, d), lambda i: (i + nt, 0)),
                  pl.BlockSpec((tr, LANES), lambda i: (i, 0)),
                  pl.BlockSpec((1, d), lambda i: (0, 0))],
        out_specs=[pl.BlockSpec((tr, d), lambda i: (jnp.minimum(i, npt - 1), 0)),
                   pl.BlockSpec((tr, d), lambda i: (jnp.maximum(i - npt, 0), 0))],
        out_shape=[jax.ShapeDtypeStruct((n_p, d), F32), jax.ShapeDtypeStruct((n_s, d), F32)],
        compiler_params=_cparams(("arbitrary",)),
        name="combine",
    )(x1, y2, y2, wt, g.reshape(1, d))


def _layer(xp, xs, st_wkv, st_shift, st_pool, norm_out_g, norm1_g, w_in, mu_shift, w0, w_decay_up, a0,
           w_iclr_up, w_gate_up, k_k, k_a, r_k, ln_x_g, ln_x_b, w_pool, pool_scale,
           w_branch_a, w_branch_b, w_out, norm2_g, w_router_group, w_router_expert,
           w_exp_gate, w_exp_up, w_exp_down):
    n_b, seq, d = xp.shape
    n_sb = xs.shape[0]
    n_p, n_s = n_b * seq, n_sb * xs.shape[1]
    n = n_p + n_s
    d_a = w_branch_a.shape[0]
    pw = w_branch_b.shape[0]
    sw = 3 * d_a + LORA_W
    assert d_a == pw and d == 2 * d_a and xs.shape[1] == 1 and w_in.shape[1] == sw + pw + 2 * d

    zpad = LORA_PAD - LORA_W
    w_in_t = w_in.T
    head_w = 3 * d_a + LORA_PAD
    pieces = lambda t: ([t[..., i * d_a:(i + 1) * d_a] for i in range(3)],
                        jnp.pad(t[..., 3 * d_a:sw], [(0, 0)] * (t.ndim - 1) + [(0, zpad)]))
    mu_rkv, mu_z = pieces(mu_shift.reshape(1, sw))
    ss_rkv, ss_z = pieces(st_shift)
    vec = lambda t: t.reshape(1, d_a)
    prm = (vec(w0), vec(a0), vec(k_k), vec(k_a), vec(r_k), vec(ln_x_g), vec(ln_x_b))
    lpad = lambda w, r0: jnp.zeros((LORA_PAD, d_a), F32).at[r0:r0 + w.shape[0]].set(w).astype(BF16)
    lora = (lpad(w_decay_up, 0), lpad(w_iclr_up, DECAY_RANK), lpad(w_gate_up, DECAY_RANK + ICLR_RANK))
    lane_head = jnp.arange(LANES, dtype=jnp.int32) // HEAD_DIM
    bsum = jnp.tile((lane_head[:, None] == lane_head[None, :]).astype(BF16), (2, 1))

    tm = _tile(n, 1664, 8)
    h, xcat = _norm1(xp.reshape(n_p, d), xs.reshape(n_s, d), norm1_g)
    tn = _tile(d, 256, LANES)
    p_head = _matmul_nt(h, w_in_t, 0, head_w, tm, tn, "in_proj_head")
    p_tail = _matmul_nt(h, w_in_t, sw, pw + 2 * d, tm, tn, "in_proj_tail")

    ya_p, s_pairs = _wkv_prompt(p_head, n_b, seq, d_a, mu_rkv, mu_z, prm, lora, bsum)
    ya_s, new_wkv_s = _wkv_sample(p_head, n_p, n_s, d_a, (*ss_rkv, ss_z), jnp.transpose(st_wkv, (1, 2, 3, 0)),
                                  mu_rkv, mu_z, prm, lora, bsum)
    yb_p = _pool_prompt(p_tail, n_b, seq, pw, w_pool, pool_scale)
    yb_s = _pool_sample(p_tail, n_p, n_s, pw, jnp.swapaxes(st_pool, 0, 1), w_pool, pool_scale)
    ya = jnp.concatenate([ya_p, ya_s], axis=0)
    yb = jnp.concatenate([yb_p, yb_s], axis=0)

    merged = _merge(ya, yb, w_branch_a, w_branch_b, p_tail, pw, d, tm, tn)
    x1 = _outproj(merged, w_out, xcat, tm, tn)

    w_router = jnp.concatenate([w_router_group, w_router_expert,
                                jnp.zeros((d, LANES - N_GROUPS - N_EXPERTS), F32)], axis=1).astype(BF16)
    h2, eid, wt = _router(x1, norm2_g, w_router)
    d_e = w_exp_gate.shape[2]
    y2 = _moe(h2, eid[:, :TOP_K], w_exp_gate, w_exp_up, w_exp_down,
              rows=MOE_GROUP_ROWS if n * TOP_K >= 4096 else 64, tf=_tile(d_e, min(256, d_e // 2), LANES))
    y_p, y_s = _combine(x1, y2, wt, norm_out_g, n_p, n_s)

    n_heads = d_a // HEAD_DIM
    sp = s_pairs.reshape(n_b, n_heads // 2, 2, HEAD_DIM, 2, HEAD_DIM)
    new_wkv_p = jnp.stack([sp[:, :, 0, :, 0, :], sp[:, :, 1, :, 1, :]], axis=2)
    new_wkv_p = jnp.swapaxes(new_wkv_p.reshape(n_b, n_heads, HEAD_DIM, HEAD_DIM), -1, -2)
    new_wkv_s = jnp.transpose(new_wkv_s, (3, 0, 1, 2))
    new_shift_p = jnp.stack([p_head[b * seq + seq - 1, :sw] for b in range(n_b)])
    new_shift_s = p_head[n_p:, :sw]
    new_pool_p = jnp.stack([p_tail[b * seq + seq - POOL_BUF:(b + 1) * seq, :pw] for b in range(n_b)])
    new_pool_s = jnp.concatenate([st_pool[:, 1:], p_tail[n_p:, :pw][:, None, :]], axis=1)
    return (y_p.reshape(n_b, seq, d), y_s.reshape(n_sb, 1, d),
            new_wkv_p, new_shift_p, new_pool_p, new_wkv_s, new_shift_s, new_pool_s)


def kernel(x_prompt, x_sample, state_wkv, state_shift, state_pool, norm1_g, w_in, mu_shift, w0, w_decay_up, a0, w_iclr_up, w_gate_up, k_k, k_a, r_k, ln_x_g, ln_x_b, w_pool, pool_scale, w_branch_a, w_branch_b, w_out, norm2_g, w_router_group, w_router_expert, w_exp_gate, w_exp_up, w_exp_down, norm_f_g):
    assert norm1_g.shape[0] == 1, "single-layer trunk"
    outs = _layer(x_prompt, x_sample, state_wkv[0], state_shift[0], state_pool[0], norm_f_g,
                  norm1_g[0], w_in[0], mu_shift[0], w0[0], w_decay_up[0], a0[0], w_iclr_up[0], w_gate_up[0],
                  k_k[0], k_a[0], r_k[0], ln_x_g[0], ln_x_b[0], w_pool[0], pool_scale[0],
                  w_branch_a[0], w_branch_b[0], w_out[0], norm2_g[0], w_router_group[0], w_router_expert[0],
                  w_exp_gate[0], w_exp_up[0], w_exp_down[0])
    y_p, y_s = outs[0], outs[1]
    return (y_p, y_s) + tuple(o[None] for o in outs[2:])
```

```python
import functools
import math

import jax
import jax.numpy as jnp
import numpy as np
from jax import lax
from jax.experimental import pallas as pl
from jax.experimental.pallas import tpu as pltpu

F32 = jnp.float32
BF16 = jnp.bfloat16

HEAD_DIM = 64
LANES = 128
DECAY_RANK = 96
ICLR_RANK = 96
GATE_RANK = 256
LORA_W = DECAY_RANK + ICLR_RANK + GATE_RANK
LORA_PAD = 512
POOL_WINDOWS = (2, 4, 8, 16)
POOL_BUF = 15
POOL_CARRY = 16
N_GROUPS = 4
EXPERTS_PER_GROUP = 8
N_EXPERTS = N_GROUPS * EXPERTS_PER_GROUP
TOP_K = 2
RMS_EPS = 1e-6
GN_EPS = 6.4e-4
L2_EPS = 1e-12
WKV_CHUNK = 64
SCATTER_BLOCK = 8
MOE_GROUP_ROWS = 608
WKV_PAIR_UNROLL = 8
VMEM_LIMIT = 56 * 1024 * 1024


def _cparams(sem, vmem=VMEM_LIMIT):
    return pltpu.CompilerParams(dimension_semantics=sem, vmem_limit_bytes=vmem)


def _tile(n, target, mult=8):
    best = None
    for t in range(mult, min(n, target) + 1, mult):
        if n % t == 0:
            best = t
    return best if best is not None else n


def _dot(a, b):
    return jnp.dot(a, b, preferred_element_type=F32)


def _sigmoid(x):
    return 0.5 * jnp.tanh(0.5 * x) + 0.5


def _norm1_kernel(xp_ref, xs_ref, g_ref, h_ref, *, n_prompt_tiles):
    i = pl.program_id(0)

    def body(x):
        xn = x * lax.rsqrt(jnp.mean(x * x, axis=-1, keepdims=True) + RMS_EPS)
        h_ref[...] = (xn * g_ref[...]).astype(BF16)

    @pl.when(i < n_prompt_tiles)
    def _():
        body(xp_ref[...])

    @pl.when(i >= n_prompt_tiles)
    def _():
        body(xs_ref[...])


def _norm1(xp, xs, g):
    n_p, d = xp.shape
    n_s = xs.shape[0]
    tr = _tile(n_s, 256)
    assert n_p % tr == 0
    npt, nst = n_p // tr, n_s // tr
    return pl.pallas_call(
        functools.partial(_norm1_kernel, n_prompt_tiles=npt),
        grid=(npt + nst,),
        in_specs=[pl.BlockSpec((tr, d), lambda i: (jnp.minimum(i, npt - 1), 0)),
                  pl.BlockSpec((tr, d), lambda i: (jnp.maximum(i - npt, 0), 0)),
                  pl.BlockSpec((1, d), lambda i: (0, 0))],
        out_specs=pl.BlockSpec((tr, d), lambda i: (i, 0)),
        out_shape=jax.ShapeDtypeStruct((n_p + n_s, d), BF16),
        compiler_params=_cparams(("arbitrary",)),
        name="norm1",
    )(xp, xs, g.reshape(1, d))


def _mm_nt_kernel(a_ref, wt_ref, o_ref):
    o_ref[...] = lax.dot_general(a_ref[...], wt_ref[...].astype(BF16), _NT, preferred_element_type=F32)


def _matmul_nt(a, wt, row0, n_out, tm, tn, name):
    m, k = a.shape
    assert row0 % 8 == 0
    return pl.pallas_call(
        _mm_nt_kernel,
        grid=(m // tm, n_out // tn),
        in_specs=[pl.BlockSpec((tm, k), lambda i, j: (i, 0)),
                  pl.BlockSpec((pl.Element(tn), pl.Element(k)), lambda i, j: (pl.multiple_of(row0 + j * tn, 8), 0))],
        out_specs=pl.BlockSpec((tm, tn), lambda i, j: (i, j)),
        out_shape=jax.ShapeDtypeStruct((m, n_out), F32),
        compiler_params=_cparams(("arbitrary", "arbitrary")),
        name=name,
    )(a, wt)


def _merge_kernel(ya_ref, yb_ref, wa_ref, wb_ref, ga_ref, gb_ref, o_ref):
    a = _dot(ya_ref[...], wa_ref[...].astype(BF16))
    b = _dot(yb_ref[...], wb_ref[...].astype(BF16))
    o_ref[...] = (_sigmoid(ga_ref[...]) * a + _sigmoid(gb_ref[...]) * b).astype(BF16)


def _merge(ya, yb, wa, wb, p, gate_col, d, tm, tn):
    m, k = ya.shape
    ga0 = gate_col // tn
    gb0 = (gate_col + d) // tn
    return pl.pallas_call(
        _merge_kernel,
        grid=(m // tm, d // tn),
        in_specs=[pl.BlockSpec((tm, k), lambda i, j: (i, 0)),
                  pl.BlockSpec((tm, k), lambda i, j: (i, 0)),
                  pl.BlockSpec((k, tn), lambda i, j: (0, j)),
                  pl.BlockSpec((k, tn), lambda i, j: (0, j)),
                  pl.BlockSpec((tm, tn), lambda i, j: (i, ga0 + j)),
                  pl.BlockSpec((tm, tn), lambda i, j: (i, gb0 + j))],
        out_specs=pl.BlockSpec((tm, tn), lambda i, j: (i, j)),
        out_shape=jax.ShapeDtypeStruct((m, d), BF16),
        compiler_params=_cparams(("arbitrary", "arbitrary")),
        name="merge",
    )(ya, yb, wa, wb, p, p)


def _outproj_kernel(m_ref, w_ref, xp_ref, xs_ref, o_ref, *, n_prompt_tiles, n_prompt_tail):
    i = pl.program_id(0)
    y = _dot(m_ref[...], w_ref[...].astype(BF16))

    @pl.when(i < n_prompt_tiles)
    def _():
        o_ref[...] = xp_ref[...] + y

    @pl.when(i == n_prompt_tiles)
    def _():
        if n_prompt_tail:
            o_ref[:n_prompt_tail, :] = xp_ref[:n_prompt_tail, :] + y[:n_prompt_tail]
        o_ref[n_prompt_tail:, :] = xs_ref[...] + y[n_prompt_tail:]


def _outproj(merged, w, xp, xs, tm, tn):
    m, k = merged.shape
    n = w.shape[1]
    n_p, n_s = xp.shape[0], xs.shape[0]
    full, tail = divmod(n_p, tm)
    assert m == n_p + n_s and tail + n_s == tm and tail % 8 == 0
    return pl.pallas_call(
        functools.partial(_outproj_kernel, n_prompt_tiles=full, n_prompt_tail=tail),
        grid=(m // tm, n // tn),
        in_specs=[pl.BlockSpec((tm, k), lambda i, j: (i, 0)),
                  pl.BlockSpec((k, tn), lambda i, j: (0, j)),
                  pl.BlockSpec((tm, tn), lambda i, j: (jnp.minimum(i, (n_p - 1) // tm), j)),
                  pl.BlockSpec((n_s, tn), lambda i, j: (0, j))],
        out_specs=pl.BlockSpec((tm, tn), lambda i, j: (i, j)),
        out_shape=jax.ShapeDtypeStruct((m, n), F32),
        compiler_params=_cparams(("arbitrary", "arbitrary")),
        name="out_proj",
    )(merged, w, xp, xs)


def _split2(x):
    hi = x.astype(BF16)
    lo = (x - hi.astype(F32)).astype(BF16)
    return hi, lo


def _split3(x):
    hi = x.astype(BF16)
    r1 = x - hi.astype(F32)
    mid = r1.astype(BF16)
    lo = (r1 - mid.astype(F32)).astype(BF16)
    return hi, mid, lo


def _dot_hp(a, b, dims=(((1,), (0,)), ((), ()))):
    (ca,), (cb,) = dims[0]
    ah, al = _split2(a)
    bh = b.astype(BF16)
    return lax.dot_general(jnp.concatenate([ah, al], axis=ca), jnp.concatenate([bh, bh], axis=cb), dims,
                           preferred_element_type=F32)


_NT = (((1,), (1,)), ((), ()))
_TN = (((0,), (0,)), ((), ()))


def _seg_sum(x, bsum_ref):
    rows = x.shape[0]
    n_slabs = x.shape[1] // LANES
    xs = jnp.concatenate([x[:, q * LANES:(q + 1) * LANES] for q in range(n_slabs)], axis=0)
    hi, lo = _split2(xs)
    s = _dot(jnp.concatenate([hi, lo], axis=1), bsum_ref[...])
    return jnp.concatenate([s[q * rows:(q + 1) * rows] for q in range(n_slabs)], axis=1)


def _rwkv_prep(r, k, v, z, w0, a0, k_k, k_a, r_k, wd, wa, wg, bsum_ref):
    def lora_up(act, w_ref, first_row, n_rows):
        lo = first_row // LANES * LANES
        hi = -(-(first_row + n_rows) // LANES) * LANES
        return _dot(act(z[:, lo:hi]).astype(BF16), w_ref[lo:hi, :])

    lora_w = lora_up(jnp.tanh, wd, 0, DECAY_RANK)
    lora_a = lora_up(lambda t: t, wa, DECAY_RANK, ICLR_RANK)
    gate = lora_up(_sigmoid, wg, DECAY_RANK + ICLR_RANK, GATE_RANK)
    log_decay = -_sigmoid(w0 + lora_w) * math.exp(-0.5)
    a = _sigmoid(a0 + lora_a)
    kk = k * k_k
    kk = kk * jnp.minimum(lax.rsqrt(_seg_sum(kk * kk, bsum_ref)), 1.0 / L2_EPS)
    k2 = k * (1.0 + (a - 1.0) * k_a)
    bonus = _seg_sum(r * k2 * r_k, bsum_ref) * v
    return log_decay, a, gate, kk, k2, bonus


def _rwkv_finish(y, bonus, gate, ln_g, ln_b, bsum_ref):
    inv = 1.0 / HEAD_DIM
    mu = _seg_sum(y, bsum_ref) * inv
    yc = y - mu
    var = _seg_sum(yc * yc, bsum_ref) * inv
    yn = yc * lax.rsqrt(var + GN_EPS) * ln_g + ln_b
    return (yn + bonus) * gate


def _wkv_prompt_kernel(pr_ref, pk_ref, pv_ref, pz_ref, mur_ref, muk_ref, muv_ref, muz_ref,
                       w0_ref, a0_ref, kk_ref, ka_ref, rk_ref, lng_ref, lnb_ref,
                       wd_ref, wa_ref, wg_ref, bsum_ref, tri_ref,
                       ya_ref, sout_ref,
                       s_scr, cr_scr, ck_scr, cv_scr, cz_scr,
                       at_scr, bt_scr, kt_scr, rt_scr, bh_scr, kh_scr, v_scr, y_scr, gc_scr,
                       *, n_pairs, pair_unroll):
    c = pl.program_id(1)
    n_chunks = pl.num_programs(1)
    C = WKV_CHUNK

    @pl.when(c == 0)
    def _init():
        s_scr[...] = jnp.zeros_like(s_scr)
        cr_scr[...] = jnp.zeros_like(cr_scr)
        ck_scr[...] = jnp.zeros_like(ck_scr)
        cv_scr[...] = jnp.zeros_like(cv_scr)
        cz_scr[...] = jnp.zeros_like(cz_scr)

    def shift(p_ref, carry, mu_ref):
        p = p_ref[...]
        prev = pltpu.roll(p, 1, axis=0)
        row = lax.broadcasted_iota(jnp.int32, p.shape, 0)
        prev = jnp.where(row == 0, carry[0:1, :], prev)
        carry[0:1, :] = p[C - 1:C, :]
        return p + (prev - p) * mu_ref[...]

    r = shift(pr_ref, cr_scr, mur_ref)
    k = shift(pk_ref, ck_scr, muk_ref)
    v = shift(pv_ref, cv_scr, muv_ref)
    z = shift(pz_ref, cz_scr, muz_ref)
    lw, a, gate, kk, k2, bonus = _rwkv_prep(
        r, k, v, z, w0_ref[...], a0_ref[...], kk_ref[...], ka_ref[...], rk_ref[...],
        wd_ref, wa_ref, wg_ref, bsum_ref)

    cl = _dot(tri_ref[...], jnp.concatenate(_split3(lw), axis=0))
    cl_end = cl[C - 1:C, :]
    beta = kk * a
    e_neg = jnp.exp(-cl)
    e_hat = jnp.exp(cl_end - cl)
    def to_pairs(scr, x):
        for q in range(n_pairs):
            scr[q] = x[:, q * LANES:(q + 1) * LANES]

    to_pairs(at_scr, -kk * jnp.exp(cl - lw))
    to_pairs(bt_scr, beta * e_neg)
    to_pairs(kt_scr, k2 * e_neg)
    to_pairs(rt_scr, r * jnp.exp(cl))
    to_pairs(bh_scr, beta * e_hat)
    to_pairs(kh_scr, k2 * e_hat)
    to_pairs(v_scr, v)
    to_pairs(gc_scr, jnp.broadcast_to(jnp.exp(cl_end), (2 * C, cl.shape[1])))

    lane = lax.broadcasted_iota(jnp.int32, (C, LANES), 1)
    first = lane < HEAD_DIM
    row2 = lax.broadcasted_iota(jnp.int32, (2 * C, 2 * C), 0)
    col2 = lax.broadcasted_iota(jnp.int32, (2 * C, 2 * C), 1)
    cbits = C.bit_length() - 1
    same = (row2 >> cbits) == (col2 >> cbits)
    tri_strict = same & ((row2 & (C - 1)) > (col2 & (C - 1)))
    tri_incl = same & ((row2 & (C - 1)) >= (col2 & (C - 1)))
    eye = row2 == col2

    def stack(x):
        return jnp.concatenate([jnp.where(first, x, 0.0), jnp.where(first, 0.0, x)], axis=0)

    def load_pair(p):
        return (at_scr[p], bt_scr[p], kt_scr[p], rt_scr[p], v_scr[p], bh_scr[p], kh_scr[p], gc_scr[p], s_scr[p])

    def compute_pair(vals):
        a_s, b_s, k_s, r_s, v_s, bh_s, kh_s = (stack(x) for x in vals[:7])
        gc, s_t = vals[7], vals[8]
        bk = jnp.concatenate([b_s, k_s], axis=0)
        ga = _dot_hp(a_s, bk, _NT)
        yield
        gr = lax.dot_general(r_s.astype(BF16), bk.astype(BF16), _NT, preferred_element_type=F32)
        yield
        l_ba = jnp.where(tri_strict, ga[:, :2 * C], 0.0)
        l_ka = jnp.where(tri_strict, ga[:, 2 * C:], 0.0)
        m_br = jnp.where(tri_incl, gr[:, :2 * C], 0.0)
        m_kr = jnp.where(tri_incl, gr[:, 2 * C:], 0.0)
        lkv = _dot_hp(l_ka, v_s)
        yield
        pw = _dot_hp(l_ba, l_ba)
        yield
        t_inv = jnp.where(eye, 1.0, l_ba)
        steps = (C - 1).bit_length() - 1
        for step in range(1, steps + 1):
            if step < steps:
                both = _dot_hp(pw, jnp.concatenate([pw, t_inv], axis=1))
                pw, t_inv = both[:, :2 * C], t_inv + both[:, 2 * C:]
            else:
                t_inv = t_inv + _dot_hp(pw, t_inv)
            yield
        tx = _dot_hp(t_inv, jnp.concatenate([a_s, lkv], axis=1))
        yield
        a_hat, u_hat = tx[:, :LANES], tx[:, LANES:]
        u = _dot_hp(a_hat, s_t) + u_hat
        yield
        uv = jnp.concatenate([u, v_s], axis=0)
        ys = _dot(jnp.concatenate([r_s, m_br, m_kr], axis=1).astype(BF16),
                  jnp.concatenate([s_t, uv], axis=0).astype(BF16))
        yield
        s_new = s_t * gc.T + _dot_hp(jnp.concatenate([bh_s, kh_s], axis=0), uv, _TN)
        return ys[:C] + ys[C:], s_new

    def body(i, carry):
        pairs = [i * pair_unroll + j for j in range(pair_unroll)]
        gens = [compute_pair(load_pair(p)) for p in pairs]
        results = {}
        while len(results) < len(gens):
            for j, gen in enumerate(gens):
                if j not in results:
                    try:
                        next(gen)
                    except StopIteration as done:
                        results[j] = done.value
        for j, p in enumerate(pairs):
            y_scr[p], s_scr[p] = results[j]
        return carry

    lax.fori_loop(0, n_pairs // pair_unroll, body, 0)

    y = jnp.concatenate([y_scr[q] for q in range(n_pairs)], axis=1)
    out = _rwkv_finish(y, bonus, gate, lng_ref[...], lnb_ref[...], bsum_ref)
    ya_ref[...] = out.astype(BF16)

    @pl.when(c == n_chunks - 1)
    def _store_state():
        sout_ref[0] = s_scr[...]


def _wkv_prompt(p, n_b, seq, d_a, mu_rkv, mu_z, prm, lora, bsum):
    C = WKV_CHUNK
    n_chunks = seq // C
    n_pairs = d_a // LANES
    zb = (3 * d_a) // LORA_PAD
    row = lambda b, c: b * n_chunks + c
    vec = lambda w: pl.BlockSpec((1, w), lambda b, c: (0, 0))
    full = lambda s: pl.BlockSpec(s, lambda b, c: (0,) * len(s))
    tri = jnp.asarray(np.tile(np.tril(np.ones((C, C), np.float32)), (1, 3)), BF16)
    in_specs = [pl.BlockSpec((C, d_a), lambda b, c: (row(b, c), 0)),
                pl.BlockSpec((C, d_a), lambda b, c: (row(b, c), 1)),
                pl.BlockSpec((C, d_a), lambda b, c: (row(b, c), 2)),
                pl.BlockSpec((C, LORA_PAD), lambda b, c: (row(b, c), zb)),
                vec(d_a), vec(d_a), vec(d_a), vec(LORA_PAD)] + [vec(d_a)] * 7 + [
                full((LORA_PAD, d_a))] * 3 + [full((2 * LANES, LANES)), full((C, 3 * C))]
    scr = [pltpu.VMEM((n_pairs, LANES, LANES), F32),
           pltpu.VMEM((8, d_a), F32), pltpu.VMEM((8, d_a), F32), pltpu.VMEM((8, d_a), F32),
           pltpu.VMEM((8, LORA_PAD), F32)] + [pltpu.VMEM((n_pairs, C, LANES), F32)] * 8 + [
           pltpu.VMEM((n_pairs, 2 * C, LANES), F32)]
    pair_unroll = math.gcd(n_pairs, WKV_PAIR_UNROLL)
    ya, s_out = pl.pallas_call(
        functools.partial(_wkv_prompt_kernel, n_pairs=n_pairs, pair_unroll=pair_unroll),
        grid=(n_b, n_chunks),
        in_specs=in_specs,
        out_specs=[pl.BlockSpec((C, d_a), lambda b, c: (row(b, c), 0)),
                   pl.BlockSpec((1, n_pairs, LANES, LANES), lambda b, c: (b, 0, 0, 0))],
        out_shape=[jax.ShapeDtypeStruct((n_b * seq, d_a), BF16),
                   jax.ShapeDtypeStruct((n_b, n_pairs, LANES, LANES), F32)],
        scratch_shapes=scr,
        compiler_params=_cparams(("arbitrary", "arbitrary")),
        name="wkv_prompt",
    )(p, p, p, p, mu_rkv[0], mu_rkv[1], mu_rkv[2], mu_z, *prm, *lora, bsum, tri)
    return ya, s_out


def _wkv_sample_prep_kernel(pr_ref, pk_ref, pv_ref, pz_ref, sr_ref, sk_ref, sv_ref, sz_ref,
                            mur_ref, muk_ref, muv_ref, muz_ref,
                            w0_ref, a0_ref, kk_ref, ka_ref, rk_ref,
                            wd_ref, wa_ref, wg_ref, bsum_ref,
                            w_out, nkk_out, kka_out, k2_out, r_out, vt_out, bonus_out, gate_out):
    def shift(p_ref, s_ref, mu_ref):
        p = p_ref[...]
        return p + (s_ref[...] - p) * mu_ref[...]

    r = shift(pr_ref, sr_ref, mur_ref)
    k = shift(pk_ref, sk_ref, muk_ref)
    v = shift(pv_ref, sv_ref, muv_ref)
    z = shift(pz_ref, sz_ref, muz_ref)
    lw, a, gate, kk, k2, bonus = _rwkv_prep(
        r, k, v, z, w0_ref[...], a0_ref[...], kk_ref[...], ka_ref[...], rk_ref[...],
        wd_ref, wa_ref, wg_ref, bsum_ref)
    w_out[...] = jnp.exp(lw).T
    nkk_out[...] = (-kk).T
    kka_out[...] = (kk * a).T
    k2_out[...] = k2.T
    r_out[...] = r.T
    vt_out[...] = v.T
    bonus_out[...] = bonus
    gate_out[...] = gate


def _wkv_sample_step_kernel(s_ref, w_ref, nkk_ref, kka_ref, k2_ref, r_ref, vt_ref, snew_ref, yt_ref):
    def body(i, carry):
        s = s_ref[0, i]
        sa = jnp.sum(s * nkk_ref[...], axis=0, keepdims=True)
        s_new = s * w_ref[...] + sa * kka_ref[...] + vt_ref[pl.ds(i, 1), :] * k2_ref[...]
        snew_ref[0, i] = s_new
        yt_ref[pl.ds(i, 1), :] = jnp.sum(s_new * r_ref[...], axis=0, keepdims=True)
        return carry

    lax.fori_loop(0, HEAD_DIM, body, 0, unroll=8)


def _wkv_sample_finish_kernel(yt_ref, bonus_ref, gate_ref, lng_ref, lnb_ref, bsum_ref, ya_ref):
    y = yt_ref[...].T
    out = _rwkv_finish(y, bonus_ref[...], gate_ref[...], lng_ref[...], lnb_ref[...], bsum_ref)
    ya_ref[...] = out.astype(BF16)


def _wkv_sample(p, n_p, n_s, d_a, s_shift, state_hijb, mu_rkv, mu_z, prm, lora, bsum):
    assert n_p % n_s == 0
    rb = n_p // n_s
    zb = (3 * d_a) // LORA_PAD
    n_heads = d_a // HEAD_DIM
    w0, a0, k_k, k_a, r_k, ln_g, ln_b = prm
    full = lambda s: pl.BlockSpec(s, lambda i: (0,) * len(s))
    tok = jax.ShapeDtypeStruct((n_s, d_a), F32)
    outs = pl.pallas_call(
        _wkv_sample_prep_kernel,
        grid=(1,),
        in_specs=[pl.BlockSpec((n_s, d_a), lambda i: (rb, 0)),
                  pl.BlockSpec((n_s, d_a), lambda i: (rb, 1)),
                  pl.BlockSpec((n_s, d_a), lambda i: (rb, 2)),
                  pl.BlockSpec((n_s, LORA_PAD), lambda i: (rb, zb)),
                  full((n_s, d_a)), full((n_s, d_a)), full((n_s, d_a)), full((n_s, LORA_PAD)),
                  full((1, d_a)), full((1, d_a)), full((1, d_a)), full((1, LORA_PAD))]
                 + [full((1, d_a))] * 5 + [full((LORA_PAD, d_a))] * 3
                 + [full((2 * LANES, LANES))],
        out_specs=[full((d_a, n_s))] * 6 + [full((n_s, d_a))] * 2,
        out_shape=[jax.ShapeDtypeStruct((d_a, n_s), F32)] * 6 + [tok] * 2,
        compiler_params=_cparams(("arbitrary",)),
        name="wkv_sample_prep",
    )(p, p, p, p, *s_shift, mu_rkv[0], mu_rkv[1], mu_rkv[2], mu_z,
      w0, a0, k_k, k_a, r_k, *lora, bsum)
    w, nkk, kka, k2, r, vt, bonus, gate = outs
    headspec = pl.BlockSpec((HEAD_DIM, n_s), lambda h: (h, 0))
    stspec = pl.BlockSpec((1, HEAD_DIM, HEAD_DIM, n_s), lambda h: (h, 0, 0, 0))
    s_new, yt = pl.pallas_call(
        _wkv_sample_step_kernel,
        grid=(n_heads,),
        in_specs=[stspec] + [headspec] * 6,
        out_specs=[stspec, headspec],
        out_shape=[jax.ShapeDtypeStruct(state_hijb.shape, F32), jax.ShapeDtypeStruct((d_a, n_s), F32)],
        compiler_params=_cparams(("arbitrary",)),
        name="wkv_sample_step",
    )(state_hijb, w, nkk, kka, k2, r, vt)
    ya = pl.pallas_call(
        _wkv_sample_finish_kernel,
        grid=(1,),
        in_specs=[full((d_a, n_s)), full((n_s, d_a)), full((n_s, d_a)), full((1, d_a)), full((1, d_a)),
                  full((2 * LANES, LANES))],
        out_specs=full((n_s, d_a)),
        out_shape=jax.ShapeDtypeStruct((n_s, d_a), BF16),
        compiler_params=_cparams(("arbitrary",)),
        name="wkv_sample_finish",
    )(yt, bonus, gate, ln_g, ln_b, bsum)
    return ya, s_new


def _pool_prompt_kernel(u_ref, w_ref, sc_ref, o_ref, carry_scr, *, tt, gw):
    t = pl.program_id(1)

    @pl.when(t == 0)
    def _():
        carry_scr[...] = jnp.zeros_like(carry_scr)

    u = u_ref[...]
    pos = t * tt + lax.broadcasted_iota(jnp.int32, (tt, gw), 0)
    for gi, win in enumerate(POOL_WINDOWS):
        cols = slice(gi * gw, (gi + 1) * gw)
        ug = u[:, cols]
        cur = jnp.concatenate([carry_scr[:, cols], ug], axis=0)
        off = 0
        step = 1
        while step < win:
            cur = cur[step:] + cur[:-step]
            off += step
            step *= 2
        wsum = cur[POOL_CARRY - off:POOL_CARRY - off + tt]
        cnt = jnp.minimum(pos + 1, win).astype(F32)
        pooled = wsum / cnt - ug
        y = _dot(pooled.astype(BF16), w_ref[gi].astype(BF16)) * sc_ref[:, cols]
        o_ref[:, cols] = y.astype(BF16)
    carry_scr[...] = u[tt - POOL_CARRY:, :]


def _pool_prompt(p, n_b, seq, pw, w_pool, pool_scale):
    tt = _tile(seq, 256)
    nt = seq // tt
    gw = pw // len(POOL_WINDOWS)
    return pl.pallas_call(
        functools.partial(_pool_prompt_kernel, tt=tt, gw=gw),
        grid=(n_b, nt),
        in_specs=[pl.BlockSpec((tt, pw), lambda b, t: (b * nt + t, 0)),
                  pl.BlockSpec(w_pool.shape, lambda b, t: (0, 0, 0)),
                  pl.BlockSpec((1, pw), lambda b, t: (0, 0))],
        out_specs=pl.BlockSpec((tt, pw), lambda b, t: (b * nt + t, 0)),
        out_shape=jax.ShapeDtypeStruct((n_b * seq, pw), BF16),
        scratch_shapes=[pltpu.VMEM((POOL_CARRY, pw), F32)],
        compiler_params=_cparams(("arbitrary", "arbitrary")),
        name="pool_prompt",
    )(p, w_pool, pool_scale.reshape(1, pw))


def _pool_sample_kernel(u_ref, hist_ref, w_ref, sc_ref, o_ref, *, gw):
    u = u_ref[...]
    for gi, win in enumerate(POOL_WINDOWS):
        cols = slice(gi * gw, (gi + 1) * gw)
        ug = u[:, cols]
        wsum = ug
        for dback in range(1, win):
            wsum = wsum + hist_ref[POOL_BUF - dback, :, cols]
        pooled = wsum / float(win) - ug
        y = _dot(pooled.astype(BF16), w_ref[gi].astype(BF16)) * sc_ref[:, cols]
        o_ref[:, cols] = y.astype(BF16)


def _pool_sample(p, n_p, n_s, pw, hist_t, w_pool, pool_scale):
    gw = pw // len(POOL_WINDOWS)
    full = lambda s: pl.BlockSpec(s, lambda i: (0,) * len(s))
    return pl.pallas_call(
        functools.partial(_pool_sample_kernel, gw=gw),
        grid=(1,),
        in_specs=[pl.BlockSpec((n_s, pw), lambda i: (n_p // n_s, 0)),
                  full(hist_t.shape), full(w_pool.shape), full((1, pw))],
        out_specs=full((n_s, pw)),
        out_shape=jax.ShapeDtypeStruct((n_s, pw), BF16),
        compiler_params=_cparams(("arbitrary",)),
        name="pool_sample",
    )(p, hist_t, w_pool, pool_scale.reshape(1, pw))


def _router_kernel(x_ref, g_ref, wr_ref, h_ref, eid_ref, wt_ref):
    x = x_ref[...]
    h = x * lax.rsqrt(jnp.mean(x * x, axis=-1, keepdims=True) + RMS_EPS) * g_ref[...]
    h_ref[...] = h
    logits = _dot(h.astype(BF16), wr_ref[...])
    lane = lax.broadcasted_iota(jnp.int32, logits.shape, 1)
    neg = jnp.float32(-jnp.inf)
    big = jnp.int32(1 << 20)
    is_g = lane < N_GROUPS
    lg = jnp.where(is_g, logits, neg)
    mg = jnp.max(lg, axis=1, keepdims=True)
    g_sel = jnp.min(jnp.where(is_g & (lg == mg), lane, big), axis=1, keepdims=True)
    p_sel = 1.0 / jnp.sum(jnp.where(is_g, jnp.exp(lg - mg), 0.0), axis=1, keepdims=True)
    e_lane = lane - N_GROUPS
    in_grp = (e_lane >= 0) & (e_lane < N_EXPERTS) & ((e_lane >> 3) == g_sel)
    le = jnp.where(in_grp, logits, neg)
    m1 = jnp.max(le, axis=1, keepdims=True)
    i1 = jnp.min(jnp.where(in_grp & (le == m1), lane, big), axis=1, keepdims=True)
    le2 = jnp.where(lane == i1, neg, le)
    m2 = jnp.max(le2, axis=1, keepdims=True)
    i2 = jnp.min(jnp.where(in_grp & (lane != i1) & (le2 == m2), lane, big), axis=1, keepdims=True)
    e2 = jnp.exp(m2 - m1)
    w1 = p_sel / (1.0 + e2)
    w2 = p_sel * e2 / (1.0 + e2)
    eid_ref[...] = jnp.where(lane == 0, i1 - N_GROUPS, jnp.where(lane == 1, i2 - N_GROUPS, 0))
    wt_ref[...] = jnp.where(lane == 0, w1, jnp.where(lane == 1, w2, 0.0))


def _router(x1, g, w_router):
    n, d = x1.shape
    tr = _tile(n, 256)
    return pl.pallas_call(
        _router_kernel,
        grid=(n // tr,),
        in_specs=[pl.BlockSpec((tr, d), lambda i: (i, 0)),
                  pl.BlockSpec((1, d), lambda i: (0, 0)),
                  pl.BlockSpec((d, LANES), lambda i: (0, 0))],
        out_specs=[pl.BlockSpec((tr, d), lambda i: (i, 0)),
                   pl.BlockSpec((tr, LANES), lambda i: (i, 0)),
                   pl.BlockSpec((tr, LANES), lambda i: (i, 0))],
        out_shape=[jax.ShapeDtypeStruct((n, d), F32),
                   jax.ShapeDtypeStruct((n, LANES), jnp.int32),
                   jax.ShapeDtypeStruct((n, LANES), F32)],
        compiler_params=_cparams(("arbitrary",)),
        name="router",
    )(x1, g.reshape(1, d), w_router)


def _moe_kernel(ge_ref, gs_ref, gn_ref, tok_ref, dst_ref,
                h_hbm, wg_ref, wu_ref, wd_ref, out_hbm,
                xf_scr, xb_scr, acc_scr, sem_g, sem_s, *, rows, n_ftiles):
    g = pl.program_id(0)
    f = pl.program_id(1)
    nrows = gn_ref[g]
    start = gs_ref[g]

    def gather_copy(tok, i):
        return pltpu.make_async_copy(h_hbm.at[pl.ds(tok, 1)], xf_scr.at[pl.ds(i, 1)], sem_g)

    def scatter_copy(i, dst):
        return pltpu.make_async_copy(acc_scr.at[pl.ds(i, 1)], out_hbm.at[pl.ds(dst, 1)], sem_s)

    @pl.when(nrows > 0)
    def _group():
        def issue_gather(first):
            def issue(i, carry):
                gather_copy(tok_ref[first + i], i).start()
                return carry
            lax.fori_loop(0, rows, issue, 0, unroll=8)

        @pl.when(f == 0)
        def _rows_in():
            @pl.when(g == 0)
            def _():
                issue_gather(start)
            pltpu.make_async_copy(h_hbm.at[pl.ds(0, rows)], xf_scr, sem_g).wait()
            xb_scr[...] = xf_scr[...].astype(BF16)

        @pl.when((f == 1) & (gn_ref[g + 1] > 0))
        def _prefetch():
            issue_gather(gs_ref[g + 1])

        def ffn_slice(first):
            x = xb_scr[...]
            hg = _dot(x, wg_ref[0].astype(BF16))
            hu = _dot(x, wu_ref[0].astype(BF16))
            act = (hg * _sigmoid(hg) * hu).astype(BF16)
            part = _dot(act, wd_ref[0].astype(BF16))
            if first:
                acc_scr[...] = part
            else:
                acc_scr[...] += part

        @pl.when(f == 0)
        def _():
            ffn_slice(True)

        @pl.when(f > 0)
        def _():
            ffn_slice(False)

        @pl.when(f == n_ftiles - 1)
        def _scatter():
            nblk = nrows // SCATTER_BLOCK

            def issue_block(b, carry):
                for j in range(SCATTER_BLOCK):
                    i = b * SCATTER_BLOCK + j
                    scatter_copy(i, dst_ref[start + i]).start()
                return carry
            lax.fori_loop(0, nblk, issue_block, 0)

            def issue_one(i, carry):
                scatter_copy(i, dst_ref[start + i]).start()
                return carry
            lax.fori_loop(nblk * SCATTER_BLOCK, nrows, issue_one, 0)

            def wait_block(b, carry):
                pltpu.make_async_copy(acc_scr.at[pl.ds(0, SCATTER_BLOCK)], out_hbm.at[pl.ds(0, SCATTER_BLOCK)],
                                      sem_s).wait()
                return carry
            lax.fori_loop(0, nblk, wait_block, 0)

            def wait_one(i, carry):
                scatter_copy(0, 0).wait()
                return carry
            lax.fori_loop(nblk * SCATTER_BLOCK, nrows, wait_one, 0)


def _moe(h2, eid, w_gate, w_up, w_down, rows, tf):
    n, d = h2.shape
    d_e = w_gate.shape[2]
    n_assign = n * TOP_K
    n_ftiles = d_e // tf
    assert n_ftiles >= 2
    max_groups = -(-n_assign // rows) + N_EXPERTS
    eflat = eid.reshape(-1)
    order = jnp.argsort(eflat, stable=True).astype(jnp.int32)
    tok_sorted = order // TOP_K
    dst_sorted = (order % TOP_K) * n + tok_sorted
    counts = jnp.zeros((N_EXPERTS,), jnp.int32).at[eflat].add(1)
    starts = jnp.cumsum(counts) - counts
    groups_per_e = (counts + rows - 1) // rows
    g_ends = jnp.cumsum(groups_per_e)
    gidx = jnp.arange(max_groups, dtype=jnp.int32)
    n_groups = g_ends[-1]
    g_e = jnp.minimum(jnp.searchsorted(g_ends, gidx, side="right"), N_EXPERTS - 1).astype(jnp.int32)
    local = gidx - (g_ends[g_e] - groups_per_e[g_e])
    g_start = starts[g_e] + local * rows
    g_n = jnp.clip(counts[g_e] - local * rows, 0, rows)
    valid = gidx < n_groups
    last_e = g_e[jnp.maximum(n_groups - 1, 0)]
    g_e = jnp.where(valid, g_e, last_e).astype(jnp.int32)
    one = jnp.zeros((1,), jnp.int32)
    g_start = jnp.concatenate([jnp.where(valid, g_start, 0).astype(jnp.int32), one])
    g_n = jnp.concatenate([jnp.where(valid, g_n, 0).astype(jnp.int32), one])
    pad = jnp.zeros((rows,), jnp.int32)
    tok_sorted = jnp.concatenate([tok_sorted, pad])
    dst_sorted = jnp.concatenate([dst_sorted, pad])

    def fsel(g, f, gn):
        return jnp.where(gn[g] > 0, f, n_ftiles - 1)

    grid_spec = pltpu.PrefetchScalarGridSpec(
        num_scalar_prefetch=5,
        grid=(max_groups, n_ftiles),
        in_specs=[pl.BlockSpec(memory_space=pl.ANY),
                  pl.BlockSpec((1, d, tf), lambda g, f, ge, gs, gn, tk, ds: (ge[g], 0, fsel(g, f, gn))),
                  pl.BlockSpec((1, d, tf), lambda g, f, ge, gs, gn, tk, ds: (ge[g], 0, fsel(g, f, gn))),
                  pl.BlockSpec((1, tf, d), lambda g, f, ge, gs, gn, tk, ds: (ge[g], fsel(g, f, gn), 0))],
        out_specs=pl.BlockSpec(memory_space=pl.ANY),
        scratch_shapes=[pltpu.VMEM((rows, d), F32), pltpu.VMEM((rows, d), BF16), pltpu.VMEM((rows, d), F32),
                        pltpu.SemaphoreType.DMA(()), pltpu.SemaphoreType.DMA(())],
    )
    return pl.pallas_call(
        functools.partial(_moe_kernel, rows=rows, n_ftiles=n_ftiles),
        grid_spec=grid_spec,
        out_shape=jax.ShapeDtypeStruct((TOP_K * n, d), F32),
        compiler_params=_cparams(("arbitrary", "arbitrary")),
        name="moe_experts",
    )(g_e, g_start, g_n, tok_sorted, dst_sorted, h2, w_gate, w_up, w_down)


def _combine_kernel(x_ref, y0_ref, y1_ref, wt_ref, g_ref, op_ref, os_ref, *, n_prompt_tiles):
    i = pl.program_id(0)
    wt = wt_ref[...]
    x = x_ref[...] + (y0_ref[...] * wt[:, 0:1] + y1_ref[...] * wt[:, 1:2])
    y = x * lax.rsqrt(jnp.mean(x * x, axis=-1, keepdims=True) + RMS_EPS) * g_ref[...]

    @pl.when(i < n_prompt_tiles)
    def _():
        op_ref[...] = y

    @pl.when(i >= n_prompt_tiles)
    def _():
        os_ref[...] = y


def _combine(x1, y2, wt, g, n_p, n_s):
    n, d = x1.shape
    tr = _tile(n_s, 256)
    npt, nst = n_p // tr, n_s // tr
    nt = npt + nst
    return pl.pallas_call(
        functools.partial(_combine_kernel, n_prompt_tiles=npt),
        grid=(nt,),
        in_specs=[pl.BlockSpec((tr, d), lambda i: (i, 0)),
                  pl.BlockSpec((tr, d), lambda i: (i, 0)),
                  pl.BlockSpec((tr, d), lambda i: (i + nt, 0)),
                  pl.BlockSpec((tr, LANES), lambda i: (i, 0)),
                  pl.BlockSpec((1, d), lambda i: (0, 0))],
        out_specs=[pl.BlockSpec((tr, d), lambda i: (jnp.minimum(i, npt - 1), 0)),
                   pl.BlockSpec((tr, d), lambda i: (jnp.maximum(i - npt, 0), 0))],
        out_shape=[jax.ShapeDtypeStruct((n_p, d), F32), jax.ShapeDtypeStruct((n_s, d), F32)],
        compiler_params=_cparams(("arbitrary",)),
        name="combine",
    )(x1, y2, y2, wt, g.reshape(1, d))


def _layer(xp, xs, st_wkv, st_shift, st_pool, norm_out_g, norm1_g, w_in, mu_shift, w0, w_decay_up, a0,
           w_iclr_up, w_gate_up, k_k, k_a, r_k, ln_x_g, ln_x_b, w_pool, pool_scale,
           w_branch_a, w_branch_b, w_out, norm2_g, w_router_group, w_router_expert,
           w_exp_gate, w_exp_up, w_exp_down):
    n_b, seq, d = xp.shape
    n_sb = xs.shape[0]
    n_p, n_s = n_b * seq, n_sb * xs.shape[1]
    n = n_p + n_s
    d_a = w_branch_a.shape[0]
    pw = w_branch_b.shape[0]
    sw = 3 * d_a + LORA_W
    assert d_a == pw and d == 2 * d_a and xs.shape[1] == 1 and w_in.shape[1] == sw + pw + 2 * d

    zpad = LORA_PAD - LORA_W
    w_in_t = w_in.T
    head_w = 3 * d_a + LORA_PAD
    pieces = lambda t: ([t[..., i * d_a:(i + 1) * d_a] for i in range(3)],
                        jnp.pad(t[..., 3 * d_a:sw], [(0, 0)] * (t.ndim - 1) + [(0, zpad)]))
    mu_rkv, mu_z = pieces(mu_shift.reshape(1, sw))
    ss_rkv, ss_z = pieces(st_shift)
    vec = lambda t: t.reshape(1, d_a)
    prm = (vec(w0), vec(a0), vec(k_k), vec(k_a), vec(r_k), vec(ln_x_g), vec(ln_x_b))
    lpad = lambda w, r0: jnp.zeros((LORA_PAD, d_a), F32).at[r0:r0 + w.shape[0]].set(w).astype(BF16)
    lora = (lpad(w_decay_up, 0), lpad(w_iclr_up, DECAY_RANK), lpad(w_gate_up, DECAY_RANK + ICLR_RANK))
    lane_head = np.arange(LANES) // HEAD_DIM
    bsum = jnp.asarray(np.tile((lane_head[:, None] == lane_head[None, :]).astype(np.float32), (2, 1)), BF16)

    tm = _tile(n, 1664, 8)
    xp2, xs2 = xp.reshape(n_p, d), xs.reshape(n_s, d)
    h = _norm1(xp2, xs2, norm1_g)
    tn = _tile(d, 256, LANES)
    p_head = _matmul_nt(h, w_in_t, 0, head_w, tm, tn, "in_proj_head")
    p_tail = _matmul_nt(h, w_in_t, sw, pw + 2 * d, tm, tn, "in_proj_tail")

    ya_p, s_pairs = _wkv_prompt(p_head, n_b, seq, d_a, mu_rkv, mu_z, prm, lora, bsum)
    ya_s, new_wkv_s = _wkv_sample(p_head, n_p, n_s, d_a, (*ss_rkv, ss_z), jnp.transpose(st_wkv, (1, 2, 3, 0)),
                                  mu_rkv, mu_z, prm, lora, bsum)
    yb_p = _pool_prompt(p_tail, n_b, seq, pw, w_pool, pool_scale)
    yb_s = _pool_sample(p_tail, n_p, n_s, pw, jnp.swapaxes(st_pool, 0, 1), w_pool, pool_scale)
    ya = jnp.concatenate([ya_p, ya_s], axis=0)
    yb = jnp.concatenate([yb_p, yb_s], axis=0)

    merged = _merge(ya, yb, w_branch_a, w_branch_b, p_tail, pw, d, tm, tn)
    x1 = _outproj(merged, w_out, xp2, xs2, tm, tn)

    w_router = jnp.concatenate([w_router_group, w_router_expert,
                                jnp.zeros((d, LANES - N_GROUPS - N_EXPERTS), F32)], axis=1).astype(BF16)
    h2, eid, wt = _router(x1, norm2_g, w_router)
    d_e = w_exp_gate.shape[2]
    y2 = _moe(h2, eid[:, :TOP_K], w_exp_gate, w_exp_up, w_exp_down,
              rows=MOE_GROUP_ROWS if n * TOP_K >= 4096 else 64, tf=_tile(d_e, min(256, d_e // 2), LANES))
    y_p, y_s = _combine(x1, y2, wt, norm_out_g, n_p, n_s)

    n_heads = d_a // HEAD_DIM
    sp = s_pairs.reshape(n_b, n_heads // 2, 2, HEAD_DIM, 2, HEAD_DIM)
    new_wkv_p = jnp.stack([sp[:, :, 0, :, 0, :], sp[:, :, 1, :, 1, :]], axis=2)
    new_wkv_p = jnp.swapaxes(new_wkv_p.reshape(n_b, n_heads, HEAD_DIM, HEAD_DIM), -1, -2)
    new_wkv_s = jnp.transpose(new_wkv_s, (3, 0, 1, 2))
    new_shift_p = jnp.stack([p_head[b * seq + seq - 1, :sw] for b in range(n_b)])
    new_shift_s = p_head[n_p:, :sw]
    new_pool_p = jnp.stack([p_tail[b * seq + seq - POOL_BUF:(b + 1) * seq, :pw] for b in range(n_b)])
    new_pool_s = jnp.concatenate([st_pool[:, 1:], p_tail[n_p:, :pw][:, None, :]], axis=1)
    return (y_p.reshape(n_b, seq, d), y_s.reshape(n_sb, 1, d),
            new_wkv_p, new_shift_p, new_pool_p, new_wkv_s, new_shift_s, new_pool_s)


def kernel(x_prompt, x_sample, state_wkv, state_shift, state_pool, norm1_g, w_in, mu_shift, w0, w_decay_up, a0, w_iclr_up, w_gate_up, k_k, k_a, r_k, ln_x_g, ln_x_b, w_pool, pool_scale, w_branch_a, w_branch_b, w_out, norm2_g, w_router_group, w_router_expert, w_exp_gate, w_exp_up, w_exp_down, norm_f_g):
    assert norm1_g.shape[0] == 1, "single-layer trunk"
    outs = _layer(x_prompt, x_sample, state_wkv[0], state_shift[0], state_pool[0], norm_f_g,
                  norm1_g[0], w_in[0], mu_shift[0], w0[0], w_decay_up[0], a0[0], w_iclr_up[0], w_gate_up[0],
                  k_k[0], k_a[0], r_k[0], ln_x_g[0], ln_x_b[0], w_pool[0], pool_scale[0],
                  w_branch_a[0], w_branch_b[0], w_out[0], norm2_g[0], w_router_group[0], w_router_expert[0],
                  w_exp_gate[0], w_exp_up[0], w_exp_down[0])
    y_p, y_s = outs[0], outs[1]
    return (y_p, y_s) + tuple(o[None] for o in outs[2:])
```

```python
import functools
import math

import jax
import jax.numpy as jnp
import numpy as np
from jax import lax
from jax.experimental import pallas as pl
from jax.experimental.pallas import tpu as pltpu

F32 = jnp.float32
BF16 = jnp.bfloat16

HEAD_DIM = 64
LANES = 128
DECAY_RANK = 96
ICLR_RANK = 96
GATE_RANK = 256
LORA_W = DECAY_RANK + ICLR_RANK + GATE_RANK
LORA_PAD = 512
POOL_WINDOWS = (2, 4, 8, 16)
POOL_BUF = 15
POOL_CARRY = 16
N_GROUPS = 4
EXPERTS_PER_GROUP = 8
N_EXPERTS = N_GROUPS * EXPERTS_PER_GROUP
TOP_K = 2
RMS_EPS = 1e-6
GN_EPS = 6.4e-4
L2_EPS = 1e-12
WKV_CHUNK = 64
SCATTER_BLOCK = 8
MOE_GROUP_ROWS = 576
WKV_PAIR_UNROLL = 8
VMEM_LIMIT = 56 * 1024 * 1024


def _cparams(sem, vmem=VMEM_LIMIT):
    return pltpu.CompilerParams(dimension_semantics=sem, vmem_limit_bytes=vmem)


def _tile(n, target, mult=8):
    best = None
    for t in range(mult, min(n, target) + 1, mult):
        if n % t == 0:
            best = t
    return best if best is not None else n


def _dot(a, b):
    return jnp.dot(a, b, preferred_element_type=F32)


def _sigmoid(x):
    return 0.5 * jnp.tanh(0.5 * x) + 0.5


def _norm1_kernel(xp_ref, xs_ref, g_ref, h_ref, *, n_prompt_tiles):
    i = pl.program_id(0)

    def body(x):
        xn = x * lax.rsqrt(jnp.mean(x * x, axis=-1, keepdims=True) + RMS_EPS)
        h_ref[...] = (xn * g_ref[...]).astype(BF16)

    @pl.when(i < n_prompt_tiles)
    def _():
        body(xp_ref[...])

    @pl.when(i >= n_prompt_tiles)
    def _():
        body(xs_ref[...])


def _norm1(xp, xs, g):
    n_p, d = xp.shape
    n_s = xs.shape[0]
    tr = _tile(n_s, 256)
    assert n_p % tr == 0
    npt, nst = n_p // tr, n_s // tr
    return pl.pallas_call(
        functools.partial(_norm1_kernel, n_prompt_tiles=npt),
        grid=(npt + nst,),
        in_specs=[pl.BlockSpec((tr, d), lambda i: (jnp.minimum(i, npt - 1), 0)),
                  pl.BlockSpec((tr, d), lambda i: (jnp.maximum(i - npt, 0), 0)),
                  pl.BlockSpec((1, d), lambda i: (0, 0))],
        out_specs=pl.BlockSpec((tr, d), lambda i: (i, 0)),
        out_shape=jax.ShapeDtypeStruct((n_p + n_s, d), BF16),
        compiler_params=_cparams(("arbitrary",)),
        name="norm1",
    )(xp, xs, g.reshape(1, d))


def _mm_nt_kernel(a_ref, wt_ref, o_ref):
    o_ref[...] = lax.dot_general(a_ref[...], wt_ref[...].astype(BF16), _NT, preferred_element_type=F32)


def _matmul_nt(a, wt, row0, n_out, tm, tn, name):
    m, k = a.shape
    assert row0 % 8 == 0
    return pl.pallas_call(
        _mm_nt_kernel,
        grid=(m // tm, n_out // tn),
        in_specs=[pl.BlockSpec((tm, k), lambda i, j: (i, 0)),
                  pl.BlockSpec((pl.Element(tn), pl.Element(k)), lambda i, j: (pl.multiple_of(row0 + j * tn, 8), 0))],
        out_specs=pl.BlockSpec((tm, tn), lambda i, j: (i, j)),
        out_shape=jax.ShapeDtypeStruct((m, n_out), F32),
        compiler_params=_cparams(("arbitrary", "arbitrary")),
        name=name,
    )(a, wt)


def _merge_kernel(ya_ref, yb_ref, wa_ref, wb_ref, ga_ref, gb_ref, o_ref):
    a = _dot(ya_ref[...], wa_ref[...].astype(BF16))
    b = _dot(yb_ref[...], wb_ref[...].astype(BF16))
    o_ref[...] = (_sigmoid(ga_ref[...]) * a + _sigmoid(gb_ref[...]) * b).astype(BF16)


def _merge(ya, yb, wa, wb, p, gate_col, d, tm, tn):
    m, k = ya.shape
    ga0 = gate_col // tn
    gb0 = (gate_col + d) // tn
    return pl.pallas_call(
        _merge_kernel,
        grid=(m // tm, d // tn),
        in_specs=[pl.BlockSpec((tm, k), lambda i, j: (i, 0)),
                  pl.BlockSpec((tm, k), lambda i, j: (i, 0)),
                  pl.BlockSpec((k, tn), lambda i, j: (0, j)),
                  pl.BlockSpec((k, tn), lambda i, j: (0, j)),
                  pl.BlockSpec((tm, tn), lambda i, j: (i, ga0 + j)),
                  pl.BlockSpec((tm, tn), lambda i, j: (i, gb0 + j))],
        out_specs=pl.BlockSpec((tm, tn), lambda i, j: (i, j)),
        out_shape=jax.ShapeDtypeStruct((m, d), BF16),
        compiler_params=_cparams(("arbitrary", "arbitrary")),
        name="merge",
    )(ya, yb, wa, wb, p, p)


def _outproj_kernel(m_ref, w_ref, xp_ref, xs_ref, o_ref, *, n_prompt_tiles, n_prompt_tail):
    i = pl.program_id(0)
    y = _dot(m_ref[...], w_ref[...].astype(BF16))

    @pl.when(i < n_prompt_tiles)
    def _():
        o_ref[...] = xp_ref[...] + y

    @pl.when(i == n_prompt_tiles)
    def _():
        if n_prompt_tail:
            o_ref[:n_prompt_tail, :] = xp_ref[:n_prompt_tail, :] + y[:n_prompt_tail]
        o_ref[n_prompt_tail:, :] = xs_ref[...] + y[n_prompt_tail:]


def _outproj(merged, w, xp, xs, tm, tn):
    m, k = merged.shape
    n = w.shape[1]
    n_p, n_s = xp.shape[0], xs.shape[0]
    full, tail = divmod(n_p, tm)
    assert m == n_p + n_s and tail + n_s == tm and tail % 8 == 0
    return pl.pallas_call(
        functools.partial(_outproj_kernel, n_prompt_tiles=full, n_prompt_tail=tail),
        grid=(m // tm, n // tn),
        in_specs=[pl.BlockSpec((tm, k), lambda i, j: (i, 0)),
                  pl.BlockSpec((k, tn), lambda i, j: (0, j)),
                  pl.BlockSpec((tm, tn), lambda i, j: (jnp.minimum(i, (n_p - 1) // tm), j)),
                  pl.BlockSpec((n_s, tn), lambda i, j: (0, j))],
        out_specs=pl.BlockSpec((tm, tn), lambda i, j: (i, j)),
        out_shape=jax.ShapeDtypeStruct((m, n), F32),
        compiler_params=_cparams(("arbitrary", "arbitrary")),
        name="out_proj",
    )(merged, w, xp, xs)


def _split2(x):
    hi = x.astype(BF16)
    lo = (x - hi.astype(F32)).astype(BF16)
    return hi, lo


def _split3(x):
    hi = x.astype(BF16)
    r1 = x - hi.astype(F32)
    mid = r1.astype(BF16)
    lo = (r1 - mid.astype(F32)).astype(BF16)
    return hi, mid, lo


def _dot_bf16(a, b, dims=(((1,), (0,)), ((), ()))):
    return lax.dot_general(a.astype(BF16), b.astype(BF16), dims, preferred_element_type=F32)


_NT = (((1,), (1,)), ((), ()))
_TN = (((0,), (0,)), ((), ()))


def _seg_sum(x, bsum_ref):
    rows = x.shape[0]
    n_slabs = x.shape[1] // LANES
    xs = jnp.concatenate([x[:, q * LANES:(q + 1) * LANES] for q in range(n_slabs)], axis=0)
    hi, lo = _split2(xs)
    s = _dot(jnp.concatenate([hi, lo], axis=1), bsum_ref[...])
    return jnp.concatenate([s[q * rows:(q + 1) * rows] for q in range(n_slabs)], axis=1)


def _rwkv_prep(r, k, v, z, w0, a0, k_k, k_a, r_k, wd, wa, wg, bsum_ref):
    def lora_up(act, w_ref, first_row, n_rows):
        lo = first_row // LANES * LANES
        hi = -(-(first_row + n_rows) // LANES) * LANES
        return _dot(act(z[:, lo:hi]).astype(BF16), w_ref[lo:hi, :])

    lora_w = lora_up(jnp.tanh, wd, 0, DECAY_RANK)
    lora_a = lora_up(lambda t: t, wa, DECAY_RANK, ICLR_RANK)
    gate = lora_up(_sigmoid, wg, DECAY_RANK + ICLR_RANK, GATE_RANK)
    log_decay = -_sigmoid(w0 + lora_w) * math.exp(-0.5)
    a = _sigmoid(a0 + lora_a)
    kk = k * k_k
    kk = kk * jnp.minimum(lax.rsqrt(_seg_sum(kk * kk, bsum_ref)), 1.0 / L2_EPS)
    k2 = k * (1.0 + (a - 1.0) * k_a)
    bonus = _seg_sum(r * k2 * r_k, bsum_ref) * v
    return log_decay, a, gate, kk, k2, bonus


def _rwkv_finish(y, bonus, gate, ln_g, ln_b, bsum_ref):
    inv = 1.0 / HEAD_DIM
    mu = _seg_sum(y, bsum_ref) * inv
    yc = y - mu
    var = _seg_sum(yc * yc, bsum_ref) * inv
    yn = yc * lax.rsqrt(var + GN_EPS) * ln_g + ln_b
    return (yn + bonus) * gate


def _wkv_prompt_kernel(pr_ref, pk_ref, pv_ref, pz_ref, mur_ref, muk_ref, muv_ref, muz_ref,
                       w0_ref, a0_ref, kk_ref, ka_ref, rk_ref, lng_ref, lnb_ref,
                       wd_ref, wa_ref, wg_ref, bsum_ref, tri_ref,
                       ya_ref, sout_ref,
                       s_scr, cr_scr, ck_scr, cv_scr, cz_scr,
                       at_scr, bt_scr, kt_scr, rt_scr, bh_scr, kh_scr, v_scr, y_scr, gc_scr,
                       *, n_pairs, pair_unroll):
    c = pl.program_id(1)
    n_chunks = pl.num_programs(1)
    C = WKV_CHUNK

    @pl.when(c == 0)
    def _init():
        s_scr[...] = jnp.zeros_like(s_scr)
        cr_scr[...] = jnp.zeros_like(cr_scr)
        ck_scr[...] = jnp.zeros_like(ck_scr)
        cv_scr[...] = jnp.zeros_like(cv_scr)
        cz_scr[...] = jnp.zeros_like(cz_scr)

    def shift(p_ref, carry, mu_ref):
        p = p_ref[...]
        prev = pltpu.roll(p, 1, axis=0)
        row = lax.broadcasted_iota(jnp.int32, p.shape, 0)
        prev = jnp.where(row == 0, carry[0:1, :], prev)
        carry[0:1, :] = p[C - 1:C, :]
        return p + (prev - p) * mu_ref[...]

    r = shift(pr_ref, cr_scr, mur_ref)
    k = shift(pk_ref, ck_scr, muk_ref)
    v = shift(pv_ref, cv_scr, muv_ref)
    z = shift(pz_ref, cz_scr, muz_ref)
    lw, a, gate, kk, k2, bonus = _rwkv_prep(
        r, k, v, z, w0_ref[...], a0_ref[...], kk_ref[...], ka_ref[...], rk_ref[...],
        wd_ref, wa_ref, wg_ref, bsum_ref)

    cl = _dot(tri_ref[...], jnp.concatenate(_split3(lw), axis=0))
    cl_end = cl[C - 1:C, :]
    beta = kk * a
    e_neg = jnp.exp(-cl)
    e_hat = jnp.exp(cl_end - cl)
    def to_pairs(scr, x):
        for q in range(n_pairs):
            scr[q] = x[:, q * LANES:(q + 1) * LANES]

    to_pairs(at_scr, -kk * jnp.exp(cl - lw))
    to_pairs(bt_scr, beta * e_neg)
    to_pairs(kt_scr, k2 * e_neg)
    to_pairs(rt_scr, r * jnp.exp(cl))
    to_pairs(bh_scr, beta * e_hat)
    to_pairs(kh_scr, k2 * e_hat)
    to_pairs(v_scr, v)
    to_pairs(gc_scr, jnp.broadcast_to(jnp.exp(cl_end), (2 * C, cl.shape[1])))

    lane = lax.broadcasted_iota(jnp.int32, (C, LANES), 1)
    first = lane < HEAD_DIM
    row2 = lax.broadcasted_iota(jnp.int32, (2 * C, 2 * C), 0)
    col2 = lax.broadcasted_iota(jnp.int32, (2 * C, 2 * C), 1)
    cbits = C.bit_length() - 1
    same = (row2 >> cbits) == (col2 >> cbits)
    tri_strict = same & ((row2 & (C - 1)) > (col2 & (C - 1)))
    tri_incl = same & ((row2 & (C - 1)) >= (col2 & (C - 1)))
    eye = row2 == col2

    def stack(x):
        return jnp.concatenate([jnp.where(first, x, 0.0), jnp.where(first, 0.0, x)], axis=0)

    def load_pair(p):
        return (at_scr[p], bt_scr[p], kt_scr[p], rt_scr[p], v_scr[p], bh_scr[p], kh_scr[p], gc_scr[p], s_scr[p])

    def compute_pair(vals):
        a_s, b_s, k_s, r_s, v_s, bh_s, kh_s = (stack(x) for x in vals[:7])
        gc, s_t = vals[7], vals[8]
        bk = jnp.concatenate([b_s, k_s], axis=0)
        ga = _dot_bf16(a_s, bk, _NT)
        yield
        gr = lax.dot_general(r_s.astype(BF16), bk.astype(BF16), _NT, preferred_element_type=F32)
        yield
        l_ba = jnp.where(tri_strict, ga[:, :2 * C], 0.0)
        l_ka = jnp.where(tri_strict, ga[:, 2 * C:], 0.0)
        m_br = jnp.where(tri_incl, gr[:, :2 * C], 0.0)
        m_kr = jnp.where(tri_incl, gr[:, 2 * C:], 0.0)
        lkv = _dot_bf16(l_ka, v_s)
        yield
        pw = _dot_bf16(l_ba, l_ba)
        yield
        t_inv = jnp.where(eye, 1.0, l_ba)
        steps = (C - 1).bit_length() - 1
        for step in range(1, steps + 1):
            if step < steps:
                both = _dot_bf16(pw, jnp.concatenate([pw, t_inv], axis=1))
                pw, t_inv = both[:, :2 * C], t_inv + both[:, 2 * C:]
            else:
                t_inv = t_inv + _dot_bf16(pw, t_inv)
            yield
        tx = _dot_bf16(t_inv, jnp.concatenate([a_s, lkv], axis=1))
        yield
        a_hat, u_hat = tx[:, :LANES], tx[:, LANES:]
        u = _dot_bf16(a_hat, s_t) + u_hat
        yield
        uv = jnp.concatenate([u, v_s], axis=0)
        ys = _dot(jnp.concatenate([r_s, m_br, m_kr], axis=1).astype(BF16),
                  jnp.concatenate([s_t, uv], axis=0).astype(BF16))
        yield
        s_new = s_t * gc.T + _dot_bf16(jnp.concatenate([bh_s, kh_s], axis=0), uv, _TN)
        return ys[:C] + ys[C:], s_new

    def body(i, carry):
        pairs = [i * pair_unroll + j for j in range(pair_unroll)]
        gens = [compute_pair(load_pair(p)) for p in pairs]
        results = {}
        while len(results) < len(gens):
            for j, gen in enumerate(gens):
                if j not in results:
                    try:
                        next(gen)
                    except StopIteration as done:
                        results[j] = done.value
        for j, p in enumerate(pairs):
            y_scr[p], s_scr[p] = results[j]
        return carry

    lax.fori_loop(0, n_pairs // pair_unroll, body, 0)

    y = jnp.concatenate([y_scr[q] for q in range(n_pairs)], axis=1)
    out = _rwkv_finish(y, bonus, gate, lng_ref[...], lnb_ref[...], bsum_ref)
    ya_ref[...] = out.astype(BF16)

    @pl.when(c == n_chunks - 1)
    def _store_state():
        sout_ref[0] = s_scr[...]


def _wkv_prompt(p, n_b, seq, d_a, mu_rkv, mu_z, prm, lora, bsum):
    C = WKV_CHUNK
    n_chunks = seq // C
    n_pairs = d_a // LANES
    zb = (3 * d_a) // LORA_PAD
    row = lambda b, c: b * n_chunks + c
    vec = lambda w: pl.BlockSpec((1, w), lambda b, c: (0, 0))
    full = lambda s: pl.BlockSpec(s, lambda b, c: (0,) * len(s))
    tri = jnp.asarray(np.tile(np.tril(np.ones((C, C), np.float32)), (1, 3)), BF16)
    in_specs = [pl.BlockSpec((C, d_a), lambda b, c: (row(b, c), 0)),
                pl.BlockSpec((C, d_a), lambda b, c: (row(b, c), 1)),
                pl.BlockSpec((C, d_a), lambda b, c: (row(b, c), 2)),
                pl.BlockSpec((C, LORA_PAD), lambda b, c: (row(b, c), zb)),
                vec(d_a), vec(d_a), vec(d_a), vec(LORA_PAD)] + [vec(d_a)] * 7 + [
                full((LORA_PAD, d_a))] * 3 + [full((2 * LANES, LANES)), full((C, 3 * C))]
    scr = [pltpu.VMEM((n_pairs, LANES, LANES), F32),
           pltpu.VMEM((8, d_a), F32), pltpu.VMEM((8, d_a), F32), pltpu.VMEM((8, d_a), F32),
           pltpu.VMEM((8, LORA_PAD), F32)] + [pltpu.VMEM((n_pairs, C, LANES), F32)] * 8 + [
           pltpu.VMEM((n_pairs, 2 * C, LANES), F32)]
    pair_unroll = math.gcd(n_pairs, WKV_PAIR_UNROLL)
    ya, s_out = pl.pallas_call(
        functools.partial(_wkv_prompt_kernel, n_pairs=n_pairs, pair_unroll=pair_unroll),
        grid=(n_b, n_chunks),
        in_specs=in_specs,
        out_specs=[pl.BlockSpec((C, d_a), lambda b, c: (row(b, c), 0)),
                   pl.BlockSpec((1, n_pairs, LANES, LANES), lambda b, c: (b, 0, 0, 0))],
        out_shape=[jax.ShapeDtypeStruct((n_b * seq, d_a), BF16),
                   jax.ShapeDtypeStruct((n_b, n_pairs, LANES, LANES), F32)],
        scratch_shapes=scr,
        compiler_params=_cparams(("arbitrary", "arbitrary")),
        name="wkv_prompt",
    )(p, p, p, p, mu_rkv[0], mu_rkv[1], mu_rkv[2], mu_z, *prm, *lora, bsum, tri)
    return ya, s_out


def _wkv_sample_prep_kernel(pr_ref, pk_ref, pv_ref, pz_ref, sr_ref, sk_ref, sv_ref, sz_ref,
                            mur_ref, muk_ref, muv_ref, muz_ref,
                            w0_ref, a0_ref, kk_ref, ka_ref, rk_ref,
                            wd_ref, wa_ref, wg_ref, bsum_ref,
                            w_out, nkk_out, kka_out, k2_out, r_out, vt_out, bonus_out, gate_out):
    def shift(p_ref, s_ref, mu_ref):
        p = p_ref[...]
        return p + (s_ref[...] - p) * mu_ref[...]

    r = shift(pr_ref, sr_ref, mur_ref)
    k = shift(pk_ref, sk_ref, muk_ref)
    v = shift(pv_ref, sv_ref, muv_ref)
    z = shift(pz_ref, sz_ref, muz_ref)
    lw, a, gate, kk, k2, bonus = _rwkv_prep(
        r, k, v, z, w0_ref[...], a0_ref[...], kk_ref[...], ka_ref[...], rk_ref[...],
        wd_ref, wa_ref, wg_ref, bsum_ref)
    w_out[...] = jnp.exp(lw).T
    nkk_out[...] = (-kk).T
    kka_out[...] = (kk * a).T
    k2_out[...] = k2.T
    r_out[...] = r.T
    vt_out[...] = v.T
    bonus_out[...] = bonus
    gate_out[...] = gate


def _wkv_sample_step_kernel(s_ref, w_ref, nkk_ref, kka_ref, k2_ref, r_ref, vt_ref, snew_ref, yt_ref):
    def body(i, carry):
        s = s_ref[0, i]
        sa = jnp.sum(s * nkk_ref[...], axis=0, keepdims=True)
        s_new = s * w_ref[...] + sa * kka_ref[...] + vt_ref[pl.ds(i, 1), :] * k2_ref[...]
        snew_ref[0, i] = s_new
        yt_ref[pl.ds(i, 1), :] = jnp.sum(s_new * r_ref[...], axis=0, keepdims=True)
        return carry

    lax.fori_loop(0, HEAD_DIM, body, 0, unroll=8)


def _wkv_sample_finish_kernel(yt_ref, bonus_ref, gate_ref, lng_ref, lnb_ref, bsum_ref, ya_ref):
    y = yt_ref[...].T
    out = _rwkv_finish(y, bonus_ref[...], gate_ref[...], lng_ref[...], lnb_ref[...], bsum_ref)
    ya_ref[...] = out.astype(BF16)


def _wkv_sample(p, n_p, n_s, d_a, s_shift, state_hijb, mu_rkv, mu_z, prm, lora, bsum):
    assert n_p % n_s == 0
    rb = n_p // n_s
    zb = (3 * d_a) // LORA_PAD
    n_heads = d_a // HEAD_DIM
    w0, a0, k_k, k_a, r_k, ln_g, ln_b = prm
    full = lambda s: pl.BlockSpec(s, lambda i: (0,) * len(s))
    tok = jax.ShapeDtypeStruct((n_s, d_a), F32)
    outs = pl.pallas_call(
        _wkv_sample_prep_kernel,
        grid=(1,),
        in_specs=[pl.BlockSpec((n_s, d_a), lambda i: (rb, 0)),
                  pl.BlockSpec((n_s, d_a), lambda i: (rb, 1)),
                  pl.BlockSpec((n_s, d_a), lambda i: (rb, 2)),
                  pl.BlockSpec((n_s, LORA_PAD), lambda i: (rb, zb)),
                  full((n_s, d_a)), full((n_s, d_a)), full((n_s, d_a)), full((n_s, LORA_PAD)),
                  full((1, d_a)), full((1, d_a)), full((1, d_a)), full((1, LORA_PAD))]
                 + [full((1, d_a))] * 5 + [full((LORA_PAD, d_a))] * 3
                 + [full((2 * LANES, LANES))],
        out_specs=[full((d_a, n_s))] * 6 + [full((n_s, d_a))] * 2,
        out_shape=[jax.ShapeDtypeStruct((d_a, n_s), F32)] * 6 + [tok] * 2,
        compiler_params=_cparams(("arbitrary",)),
        name="wkv_sample_prep",
    )(p, p, p, p, *s_shift, mu_rkv[0], mu_rkv[1], mu_rkv[2], mu_z,
      w0, a0, k_k, k_a, r_k, *lora, bsum)
    w, nkk, kka, k2, r, vt, bonus, gate = outs
    headspec = pl.BlockSpec((HEAD_DIM, n_s), lambda h: (h, 0))
    stspec = pl.BlockSpec((1, HEAD_DIM, HEAD_DIM, n_s), lambda h: (h, 0, 0, 0))
    s_new, yt = pl.pallas_call(
        _wkv_sample_step_kernel,
        grid=(n_heads,),
        in_specs=[stspec] + [headspec] * 6,
        out_specs=[stspec, headspec],
        out_shape=[jax.ShapeDtypeStruct(state_hijb.shape, F32), jax.ShapeDtypeStruct((d_a, n_s), F32)],
        compiler_params=_cparams(("arbitrary",)),
        name="wkv_sample_step",
    )(state_hijb, w, nkk, kka, k2, r, vt)
    ya = pl.pallas_call(
        _wkv_sample_finish_kernel,
        grid=(1,),
        in_specs=[full((d_a, n_s)), full((n_s, d_a)), full((n_s, d_a)), full((1, d_a)), full((1, d_a)),
                  full((2 * LANES, LANES))],
        out_specs=full((n_s, d_a)),
        out_shape=jax.ShapeDtypeStruct((n_s, d_a), BF16),
        compiler_params=_cparams(("arbitrary",)),
        name="wkv_sample_finish",
    )(yt, bonus, gate, ln_g, ln_b, bsum)
    return ya, s_new


def _pool_prompt_kernel(u_ref, w_ref, sc_ref, o_ref, carry_scr, *, tt, gw):
    t = pl.program_id(1)

    @pl.when(t == 0)
    def _():
        carry_scr[...] = jnp.zeros_like(carry_scr)

    u = u_ref[...]
    pos = t * tt + lax.broadcasted_iota(jnp.int32, (tt, gw), 0)
    for gi, win in enumerate(POOL_WINDOWS):
        cols = slice(gi * gw, (gi + 1) * gw)
        ug = u[:, cols]
        cur = jnp.concatenate([carry_scr[:, cols], ug], axis=0)
        off = 0
        step = 1
        while step < win:
            cur = cur[step:] + cur[:-step]
            off += step
            step *= 2
        wsum = cur[POOL_CARRY - off:POOL_CARRY - off + tt]
        cnt = jnp.minimum(pos + 1, win).astype(F32)
        pooled = wsum / cnt - ug
        y = _dot(pooled.astype(BF16), w_ref[gi].astype(BF16)) * sc_ref[:, cols]
        o_ref[:, cols] = y.astype(BF16)
    carry_scr[...] = u[tt - POOL_CARRY:, :]


def _pool_prompt(p, n_b, seq, pw, w_pool, pool_scale):
    tt = _tile(seq, 256)
    nt = seq // tt
    gw = pw // len(POOL_WINDOWS)
    return pl.pallas_call(
        functools.partial(_pool_prompt_kernel, tt=tt, gw=gw),
        grid=(n_b, nt),
        in_specs=[pl.BlockSpec((tt, pw), lambda b, t: (b * nt + t, 0)),
                  pl.BlockSpec(w_pool.shape, lambda b, t: (0, 0, 0)),
                  pl.BlockSpec((1, pw), lambda b, t: (0, 0))],
        out_specs=pl.BlockSpec((tt, pw), lambda b, t: (b * nt + t, 0)),
        out_shape=jax.ShapeDtypeStruct((n_b * seq, pw), BF16),
        scratch_shapes=[pltpu.VMEM((POOL_CARRY, pw), F32)],
        compiler_params=_cparams(("arbitrary", "arbitrary")),
        name="pool_prompt",
    )(p, w_pool, pool_scale.reshape(1, pw))


def _pool_sample_kernel(u_ref, hist_ref, w_ref, sc_ref, o_ref, *, gw):
    u = u_ref[...]
    for gi, win in enumerate(POOL_WINDOWS):
        cols = slice(gi * gw, (gi + 1) * gw)
        ug = u[:, cols]
        wsum = ug
        for dback in range(1, win):
            wsum = wsum + hist_ref[POOL_BUF - dback, :, cols]
        pooled = wsum / float(win) - ug
        y = _dot(pooled.astype(BF16), w_ref[gi].astype(BF16)) * sc_ref[:, cols]
        o_ref[:, cols] = y.astype(BF16)


def _pool_sample(p, n_p, n_s, pw, hist_t, w_pool, pool_scale):
    gw = pw // len(POOL_WINDOWS)
    full = lambda s: pl.BlockSpec(s, lambda i: (0,) * len(s))
    return pl.pallas_call(
        functools.partial(_pool_sample_kernel, gw=gw),
        grid=(1,),
        in_specs=[pl.BlockSpec((n_s, pw), lambda i: (n_p // n_s, 0)),
                  full(hist_t.shape), full(w_pool.shape), full((1, pw))],
        out_specs=full((n_s, pw)),
        out_shape=jax.ShapeDtypeStruct((n_s, pw), BF16),
        compiler_params=_cparams(("arbitrary",)),
        name="pool_sample",
    )(p, hist_t, w_pool, pool_scale.reshape(1, pw))


def _router_kernel(x_ref, g_ref, wr_ref, h_ref, eid_ref, wt_ref):
    x = x_ref[...]
    h = x * lax.rsqrt(jnp.mean(x * x, axis=-1, keepdims=True) + RMS_EPS) * g_ref[...]
    h_ref[...] = h
    logits = _dot(h.astype(BF16), wr_ref[...])
    lane = lax.broadcasted_iota(jnp.int32, logits.shape, 1)
    neg = jnp.float32(-jnp.inf)
    big = jnp.int32(1 << 20)
    is_g = lane < N_GROUPS
    lg = jnp.where(is_g, logits, neg)
    mg = jnp.max(lg, axis=1, keepdims=True)
    g_sel = jnp.min(jnp.where(is_g & (lg == mg), lane, big), axis=1, keepdims=True)
    p_sel = 1.0 / jnp.sum(jnp.where(is_g, jnp.exp(lg - mg), 0.0), axis=1, keepdims=True)
    e_lane = lane - N_GROUPS
    in_grp = (e_lane >= 0) & (e_lane < N_EXPERTS) & ((e_lane >> 3) == g_sel)
    le = jnp.where(in_grp, logits, neg)
    m1 = jnp.max(le, axis=1, keepdims=True)
    i1 = jnp.min(jnp.where(in_grp & (le == m1), lane, big), axis=1, keepdims=True)
    le2 = jnp.where(lane == i1, neg, le)
    m2 = jnp.max(le2, axis=1, keepdims=True)
    i2 = jnp.min(jnp.where(in_grp & (lane != i1) & (le2 == m2), lane, big), axis=1, keepdims=True)
    e2 = jnp.exp(m2 - m1)
    w1 = p_sel / (1.0 + e2)
    w2 = p_sel * e2 / (1.0 + e2)
    eid_ref[...] = jnp.where(lane == 0, i1 - N_GROUPS, jnp.where(lane == 1, i2 - N_GROUPS, 0))
    wt_ref[...] = jnp.where(lane == 0, w1, jnp.where(lane == 1, w2, 0.0))


def _router(x1, g, w_router):
    n, d = x1.shape
    tr = _tile(n, 256)
    return pl.pallas_call(
        _router_kernel,
        grid=(n // tr,),
        in_specs=[pl.BlockSpec((tr, d), lambda i: (i, 0)),
                  pl.BlockSpec((1, d), lambda i: (0, 0)),
                  pl.BlockSpec((d, LANES), lambda i: (0, 0))],
        out_specs=[pl.BlockSpec((tr, d), lambda i: (i, 0)),
                   pl.BlockSpec((tr, LANES), lambda i: (i, 0)),
                   pl.BlockSpec((tr, LANES), lambda i: (i, 0))],
        out_shape=[jax.ShapeDtypeStruct((n, d), F32),
                   jax.ShapeDtypeStruct((n, LANES), jnp.int32),
                   jax.ShapeDtypeStruct((n, LANES), F32)],
        compiler_params=_cparams(("arbitrary",)),
        name="router",
    )(x1, g.reshape(1, d), w_router)


def _moe_kernel(ge_ref, gs_ref, gn_ref, tok_ref, dst_ref,
                h_hbm, wg_ref, wu_ref, wd_ref, out_hbm,
                xf_scr, xb_scr, acc_scr, sem_g, sem_s, *, rows, n_ftiles):
    g = pl.program_id(0)
    f = pl.program_id(1)
    nrows = gn_ref[g]
    start = gs_ref[g]

    def gather_copy(tok, i):
        return pltpu.make_async_copy(h_hbm.at[pl.ds(tok, 1)], xf_scr.at[pl.ds(i, 1)], sem_g)

    def scatter_copy(i, dst):
        return pltpu.make_async_copy(acc_scr.at[pl.ds(i, 1)], out_hbm.at[pl.ds(dst, 1)], sem_s)

    @pl.when(nrows > 0)
    def _group():
        def issue_gather(first):
            def issue(i, carry):
                gather_copy(tok_ref[first + i], i).start()
                return carry
            lax.fori_loop(0, rows, issue, 0, unroll=8)

        @pl.when(f == 0)
        def _rows_in():
            @pl.when(g == 0)
            def _():
                issue_gather(start)
            pltpu.make_async_copy(h_hbm.at[pl.ds(0, rows)], xf_scr, sem_g).wait()
            xb_scr[...] = xf_scr[...].astype(BF16)

        @pl.when((f == 1) & (gn_ref[g + 1] > 0))
        def _prefetch():
            issue_gather(gs_ref[g + 1])

        def ffn_slice(first):
            x = xb_scr[...]
            hg = _dot(x, wg_ref[0].astype(BF16))
            hu = _dot(x, wu_ref[0].astype(BF16))
            act = (hg * _sigmoid(hg) * hu).astype(BF16)
            part = _dot(act, wd_ref[0].astype(BF16))
            if first:
                acc_scr[...] = part
            else:
                acc_scr[...] += part

        @pl.when(f == 0)
        def _():
            ffn_slice(True)

        @pl.when(f > 0)
        def _():
            ffn_slice(False)

        @pl.when(f == n_ftiles - 1)
        def _scatter():
            nblk = nrows // SCATTER_BLOCK

            def issue_block(b, carry):
                for j in range(SCATTER_BLOCK):
                    i = b * SCATTER_BLOCK + j
                    scatter_copy(i, dst_ref[start + i]).start(priority=j % 2)
                return carry
            lax.fori_loop(0, nblk, issue_block, 0)

            def issue_one(i, carry):
                scatter_copy(i, dst_ref[start + i]).start()
                return carry
            lax.fori_loop(nblk * SCATTER_BLOCK, nrows, issue_one, 0)

            def wait_block(b, carry):
                pltpu.make_async_copy(acc_scr.at[pl.ds(0, SCATTER_BLOCK)], out_hbm.at[pl.ds(0, SCATTER_BLOCK)],
                                      sem_s).wait()
                return carry
            lax.fori_loop(0, nblk, wait_block, 0)

            def wait_one(i, carry):
                scatter_copy(0, 0).wait()
                return carry
            lax.fori_loop(nblk * SCATTER_BLOCK, nrows, wait_one, 0)


def _moe(h2, eid, w_gate, w_up, w_down, rows, tf):
    n, d = h2.shape
    d_e = w_gate.shape[2]
    n_assign = n * TOP_K
    n_ftiles = d_e // tf
    assert n_ftiles >= 2
    max_groups = -(-n_assign // rows) + N_EXPERTS
    eflat = eid.reshape(-1)
    order = jnp.argsort(eflat, stable=True).astype(jnp.int32)
    tok_sorted = order // TOP_K
    dst_sorted = (order % TOP_K) * n + tok_sorted
    counts = jnp.zeros((N_EXPERTS,), jnp.int32).at[eflat].add(1)
    starts = jnp.cumsum(counts) - counts
    groups_per_e = (counts + rows - 1) // rows
    g_ends = jnp.cumsum(groups_per_e)
    gidx = jnp.arange(max_groups, dtype=jnp.int32)
    n_groups = g_ends[-1]
    g_e = jnp.minimum(jnp.searchsorted(g_ends, gidx, side="right"), N_EXPERTS - 1).astype(jnp.int32)
    local = gidx - (g_ends[g_e] - groups_per_e[g_e])
    g_start = starts[g_e] + local * rows
    g_n = jnp.clip(counts[g_e] - local * rows, 0, rows)
    valid = gidx < n_groups
    last_e = g_e[jnp.maximum(n_groups - 1, 0)]
    g_e = jnp.where(valid, g_e, last_e).astype(jnp.int32)
    one = jnp.zeros((1,), jnp.int32)
    g_start = jnp.concatenate([jnp.where(valid, g_start, 0).astype(jnp.int32), one])
    g_n = jnp.concatenate([jnp.where(valid, g_n, 0).astype(jnp.int32), one])
    pad = jnp.zeros((rows,), jnp.int32)
    tok_sorted = jnp.concatenate([tok_sorted, pad])
    dst_sorted = jnp.concatenate([dst_sorted, pad])

    def fsel(g, f, gn):
        return jnp.where(gn[g] > 0, f, n_ftiles - 1)

    grid_spec = pltpu.PrefetchScalarGridSpec(
        num_scalar_prefetch=5,
        grid=(max_groups, n_ftiles),
        in_specs=[pl.BlockSpec(memory_space=pl.ANY),
                  pl.BlockSpec((1, d, tf), lambda g, f, ge, gs, gn, tk, ds: (ge[g], 0, fsel(g, f, gn))),
                  pl.BlockSpec((1, d, tf), lambda g, f, ge, gs, gn, tk, ds: (ge[g], 0, fsel(g, f, gn))),
                  pl.BlockSpec((1, tf, d), lambda g, f, ge, gs, gn, tk, ds: (ge[g], fsel(g, f, gn), 0))],
        out_specs=pl.BlockSpec(memory_space=pl.ANY),
        scratch_shapes=[pltpu.VMEM((rows, d), F32), pltpu.VMEM((rows, d), BF16), pltpu.VMEM((rows, d), F32),
                        pltpu.SemaphoreType.DMA(()), pltpu.SemaphoreType.DMA(())],
    )
    return pl.pallas_call(
        functools.partial(_moe_kernel, rows=rows, n_ftiles=n_ftiles),
        grid_spec=grid_spec,
        out_shape=jax.ShapeDtypeStruct((TOP_K * n, d), F32),
        compiler_params=_cparams(("arbitrary", "arbitrary")),
        name="moe_experts",
    )(g_e, g_start, g_n, tok_sorted, dst_sorted, h2, w_gate, w_up, w_down)


def _combine_kernel(x_ref, y0_ref, y1_ref, wt_ref, g_ref, op_ref, os_ref, *, n_prompt_tiles):
    i = pl.program_id(0)
    wt = wt_ref[...]
    x = x_ref[...] + (y0_ref[...] * wt[:, 0:1] + y1_ref[...] * wt[:, 1:2])
    y = x * lax.rsqrt(jnp.mean(x * x, axis=-1, keepdims=True) + RMS_EPS) * g_ref[...]

    @pl.when(i < n_prompt_tiles)
    def _():
        op_ref[...] = y

    @pl.when(i >= n_prompt_tiles)
    def _():
        os_ref[...] = y


def _combine(x1, y2, wt, g, n_p, n_s):
    n, d = x1.shape
    tr = _tile(n_s, 256)
    npt, nst = n_p // tr, n_s // tr
    nt = npt + nst
    return pl.pallas_call(
        functools.partial(_combine_kernel, n_prompt_tiles=npt),
        grid=(nt,),
        in_specs=[pl.BlockSpec((tr, d), lambda i: (i, 0)),
                  pl.BlockSpec((tr, d), lambda i: (i, 0)),
                  pl.BlockSpec((tr, d), lambda i: (i + nt, 0)),
                  pl.BlockSpec((tr, LANES), lambda i: (i, 0)),
                  pl.BlockSpec((1, d), lambda i: (0, 0))],
        out_specs=[pl.BlockSpec((tr, d), lambda i: (jnp.minimum(i, npt - 1), 0)),
                   pl.BlockSpec((tr, d), lambda i: (jnp.maximum(i - npt, 0), 0))],
        out_shape=[jax.ShapeDtypeStruct((n_p, d), F32), jax.ShapeDtypeStruct((n_s, d), F32)],
        compiler_params=_cparams(("arbitrary",)),
        name="combine",
    )(x1, y2, y2, wt, g.reshape(1, d))


def _layer(xp, xs, st_wkv, st_shift, st_pool, norm_out_g, norm1_g, w_in, mu_shift, w0, w_decay_up, a0,
           w_iclr_up, w_gate_up, k_k, k_a, r_k, ln_x_g, ln_x_b, w_pool, pool_scale,
           w_branch_a, w_branch_b, w_out, norm2_g, w_router_group, w_router_expert,
           w_exp_gate, w_exp_up, w_exp_down):
    n_b, seq, d = xp.shape
    n_sb = xs.shape[0]
    n_p, n_s = n_b * seq, n_sb * xs.shape[1]
    n = n_p + n_s
    d_a = w_branch_a.shape[0]
    pw = w_branch_b.shape[0]
    sw = 3 * d_a + LORA_W
    assert d_a == pw and d == 2 * d_a and xs.shape[1] == 1 and w_in.shape[1] == sw + pw + 2 * d

    zpad = LORA_PAD - LORA_W
    w_in_t = w_in.T
    head_w = 3 * d_a + LORA_PAD
    pieces = lambda t: ([t[..., i * d_a:(i + 1) * d_a] for i in range(3)],
                        jnp.pad(t[..., 3 * d_a:sw], [(0, 0)] * (t.ndim - 1) + [(0, zpad)]))
    mu_rkv, mu_z = pieces(mu_shift.reshape(1, sw))
    ss_rkv, ss_z = pieces(st_shift)
    vec = lambda t: t.reshape(1, d_a)
    prm = (vec(w0), vec(a0), vec(k_k), vec(k_a), vec(r_k), vec(ln_x_g), vec(ln_x_b))
    lpad = lambda w, r0: jnp.zeros((LORA_PAD, d_a), F32).at[r0:r0 + w.shape[0]].set(w).astype(BF16)
    lora = (lpad(w_decay_up, 0), lpad(w_iclr_up, DECAY_RANK), lpad(w_gate_up, DECAY_RANK + ICLR_RANK))
    lane_head = np.arange(LANES) // HEAD_DIM
    bsum = jnp.asarray(np.tile((lane_head[:, None] == lane_head[None, :]).astype(np.float32), (2, 1)), BF16)

    tm = _tile(n, 1664, 8)
    xp2, xs2 = xp.reshape(n_p, d), xs.reshape(n_s, d)
    h = _norm1(xp2, xs2, norm1_g)
    tn = _tile(d, 256, LANES)
    p_head = _matmul_nt(h, w_in_t, 0, head_w, tm, tn, "in_proj_head")
    p_tail = _matmul_nt(h, w_in_t, sw, pw + 2 * d, tm, tn, "in_proj_tail")

    ya_p, s_pairs = _wkv_prompt(p_head, n_b, seq, d_a, mu_rkv, mu_z, prm, lora, bsum)
    ya_s, new_wkv_s = _wkv_sample(p_head, n_p, n_s, d_a, (*ss_rkv, ss_z), jnp.transpose(st_wkv, (1, 2, 3, 0)),
                                  mu_rkv, mu_z, prm, lora, bsum)
    yb_p = _pool_prompt(p_tail, n_b, seq, pw, w_pool, pool_scale)
    yb_s = _pool_sample(p_tail, n_p, n_s, pw, jnp.swapaxes(st_pool, 0, 1), w_pool, pool_scale)
    ya = jnp.concatenate([ya_p, ya_s], axis=0)
    yb = jnp.concatenate([yb_p, yb_s], axis=0)

    merged = _merge(ya, yb, w_branch_a, w_branch_b, p_tail, pw, d, tm, tn)
    x1 = _outproj(merged, w_out, xp2, xs2, tm, tn)

    w_router = jnp.concatenate([w_router_group, w_router_expert,
                                jnp.zeros((d, LANES - N_GROUPS - N_EXPERTS), F32)], axis=1).astype(BF16)
    h2, eid, wt = _router(x1, norm2_g, w_router)
    d_e = w_exp_gate.shape[2]
    y2 = _moe(h2, eid[:, :TOP_K], w_exp_gate, w_exp_up, w_exp_down,
              rows=MOE_GROUP_ROWS if n * TOP_K >= 4096 else 64, tf=_tile(d_e, min(256, d_e // 2), LANES))
    y_p, y_s = _combine(x1, y2, wt, norm_out_g, n_p, n_s)

    n_heads = d_a // HEAD_DIM
    sp = s_pairs.reshape(n_b, n_heads // 2, 2, HEAD_DIM, 2, HEAD_DIM)
    new_wkv_p = jnp.stack([sp[:, :, 0, :, 0, :], sp[:, :, 1, :, 1, :]], axis=2)
    new_wkv_p = jnp.swapaxes(new_wkv_p.reshape(n_b, n_heads, HEAD_DIM, HEAD_DIM), -1, -2)
    new_wkv_s = jnp.transpose(new_wkv_s, (3, 0, 1, 2))
    new_shift_p = jnp.stack([p_head[b * seq + seq - 1, :sw] for b in range(n_b)])
    new_shift_s = p_head[n_p:, :sw]
    new_pool_p = jnp.stack([p_tail[b * seq + seq - POOL_BUF:(b + 1) * seq, :pw] for b in range(n_b)])
    new_pool_s = jnp.concatenate([st_pool[:, 1:], p_tail[n_p:, :pw][:, None, :]], axis=1)
    return (y_p.reshape(n_b, seq, d), y_s.reshape(n_sb, 1, d),
            new_wkv_p, new_shift_p, new_pool_p, new_wkv_s, new_shift_s, new_pool_s)


def kernel(x_prompt, x_sample, state_wkv, state_shift, state_pool, norm1_g, w_in, mu_shift, w0, w_decay_up, a0, w_iclr_up, w_gate_up, k_k, k_a, r_k, ln_x_g, ln_x_b, w_pool, pool_scale, w_branch_a, w_branch_b, w_out, norm2_g, w_router_group, w_router_expert, w_exp_gate, w_exp_up, w_exp_down, norm_f_g):
    assert norm1_g.shape[0] == 1, "single-layer trunk"
    outs = _layer(x_prompt, x_sample, state_wkv[0], state_shift[0], state_pool[0], norm_f_g,
                  norm1_g[0], w_in[0], mu_shift[0], w0[0], w_decay_up[0], a0[0], w_iclr_up[0], w_gate_up[0],
                  k_k[0], k_a[0], r_k[0], ln_x_g[0], ln_x_b[0], w_pool[0], pool_scale[0],
                  w_branch_a[0], w_branch_b[0], w_out[0], norm2_g[0], w_router_group[0], w_router_expert[0],
                  w_exp_gate[0], w_exp_up[0], w_exp_down[0])
    y_p, y_s = outs[0], outs[1]
    return (y_p, y_s) + tuple(o[None] for o in outs[2:])
```

```python
import functools
import math

import jax
import jax.numpy as jnp
import numpy as np
from jax import lax
from jax.experimental import pallas as pl
from jax.experimental.pallas import tpu as pltpu

F32 = jnp.float32
BF16 = jnp.bfloat16

HEAD_DIM = 64
LANES = 128
DECAY_RANK = 96
ICLR_RANK = 96
GATE_RANK = 256
LORA_W = DECAY_RANK + ICLR_RANK + GATE_RANK
LORA_PAD = 512
POOL_WINDOWS = (2, 4, 8, 16)
POOL_BUF = 15
POOL_CARRY = 16
N_GROUPS = 4
EXPERTS_PER_GROUP = 8
N_EXPERTS = N_GROUPS * EXPERTS_PER_GROUP
TOP_K = 2
RMS_EPS = 1e-6
GN_EPS = 6.4e-4
L2_EPS = 1e-12
WKV_CHUNK = 64
SCATTER_BLOCK = 8
MOE_GROUP_ROWS = 576
WKV_PAIR_UNROLL = 8
PREP_PIECES_PER_ROUND = 1
VMEM_LIMIT = 56 * 1024 * 1024


def _cparams(sem, vmem=VMEM_LIMIT):
    return pltpu.CompilerParams(dimension_semantics=sem, vmem_limit_bytes=vmem)


def _tile(n, target, mult=8):
    best = None
    for t in range(mult, min(n, target) + 1, mult):
        if n % t == 0:
            best = t
    return best if best is not None else n


def _dot(a, b):
    return jnp.dot(a, b, preferred_element_type=F32)


def _sigmoid(x):
    return 0.5 * jnp.tanh(0.5 * x) + 0.5


def _norm1_kernel(xp_ref, xs_ref, g_ref, h_ref, *, n_prompt_tiles):
    i = pl.program_id(0)

    def body(x):
        xn = x * lax.rsqrt(jnp.mean(x * x, axis=-1, keepdims=True) + RMS_EPS)
        h_ref[...] = (xn * g_ref[...]).astype(BF16)

    @pl.when(i < n_prompt_tiles)
    def _():
        body(xp_ref[...])

    @pl.when(i >= n_prompt_tiles)
    def _():
        body(xs_ref[...])


def _norm1(xp, xs, g):
    n_p, d = xp.shape
    n_s = xs.shape[0]
    tr = _tile(n_s, 256)
    assert n_p % tr == 0
    npt, nst = n_p // tr, n_s // tr
    return pl.pallas_call(
        functools.partial(_norm1_kernel, n_prompt_tiles=npt),
        grid=(npt + nst,),
        in_specs=[pl.BlockSpec((tr, d), lambda i: (jnp.minimum(i, npt - 1), 0)),
                  pl.BlockSpec((tr, d), lambda i: (jnp.maximum(i - npt, 0), 0)),
                  pl.BlockSpec((1, d), lambda i: (0, 0))],
        out_specs=pl.BlockSpec((tr, d), lambda i: (i, 0)),
        out_shape=jax.ShapeDtypeStruct((n_p + n_s, d), BF16),
        compiler_params=_cparams(("arbitrary",)),
        name="norm1",
    )(xp, xs, g.reshape(1, d))


def _mm_nt_kernel(a_ref, wt_ref, o_ref):
    o_ref[...] = lax.dot_general(a_ref[...], wt_ref[...].astype(BF16), _NT, preferred_element_type=F32)


def _matmul_nt(a, wt, row0, n_out, tm, tn, name):
    m, k = a.shape
    assert row0 % 8 == 0
    return pl.pallas_call(
        _mm_nt_kernel,
        grid=(m // tm, n_out // tn),
        in_specs=[pl.BlockSpec((tm, k), lambda i, j: (i, 0)),
                  pl.BlockSpec((pl.Element(tn), pl.Element(k)), lambda i, j: (pl.multiple_of(row0 + j * tn, 8), 0))],
        out_specs=pl.BlockSpec((tm, tn), lambda i, j: (i, j)),
        out_shape=jax.ShapeDtypeStruct((m, n_out), F32),
        compiler_params=_cparams(("arbitrary", "arbitrary")),
        name=name,
    )(a, wt)


def _merge_kernel(ya_ref, yb_ref, wa_ref, wb_ref, ga_ref, gb_ref, o_ref):
    a = _dot(ya_ref[...], wa_ref[...].astype(BF16))
    b = _dot(yb_ref[...], wb_ref[...].astype(BF16))
    o_ref[...] = (_sigmoid(ga_ref[...]) * a + _sigmoid(gb_ref[...]) * b).astype(BF16)


def _merge(ya, yb, wa, wb, p, gate_col, d, tm, tn):
    m, k = ya.shape
    ga0 = gate_col // tn
    gb0 = (gate_col + d) // tn
    return pl.pallas_call(
        _merge_kernel,
        grid=(m // tm, d // tn),
        in_specs=[pl.BlockSpec((tm, k), lambda i, j: (i, 0)),
                  pl.BlockSpec((tm, k), lambda i, j: (i, 0)),
                  pl.BlockSpec((k, tn), lambda i, j: (0, j)),
                  pl.BlockSpec((k, tn), lambda i, j: (0, j)),
                  pl.BlockSpec((tm, tn), lambda i, j: (i, ga0 + j)),
                  pl.BlockSpec((tm, tn), lambda i, j: (i, gb0 + j))],
        out_specs=pl.BlockSpec((tm, tn), lambda i, j: (i, j)),
        out_shape=jax.ShapeDtypeStruct((m, d), BF16),
        compiler_params=_cparams(("arbitrary", "arbitrary")),
        name="merge",
    )(ya, yb, wa, wb, p, p)


def _outproj_kernel(m_ref, w_ref, xp_ref, xs_ref, o_ref, *, n_prompt_tiles, n_prompt_tail):
    i = pl.program_id(0)
    y = _dot(m_ref[...], w_ref[...].astype(BF16))

    @pl.when(i < n_prompt_tiles)
    def _():
        o_ref[...] = xp_ref[...] + y

    @pl.when(i == n_prompt_tiles)
    def _():
        if n_prompt_tail:
            o_ref[:n_prompt_tail, :] = xp_ref[:n_prompt_tail, :] + y[:n_prompt_tail]
        o_ref[n_prompt_tail:, :] = xs_ref[...] + y[n_prompt_tail:]


def _outproj(merged, w, xp, xs, tm, tn):
    m, k = merged.shape
    n = w.shape[1]
    n_p, n_s = xp.shape[0], xs.shape[0]
    full, tail = divmod(n_p, tm)
    assert m == n_p + n_s and tail + n_s == tm and tail % 8 == 0
    return pl.pallas_call(
        functools.partial(_outproj_kernel, n_prompt_tiles=full, n_prompt_tail=tail),
        grid=(m // tm, n // tn),
        in_specs=[pl.BlockSpec((tm, k), lambda i, j: (i, 0)),
                  pl.BlockSpec((k, tn), lambda i, j: (0, j)),
                  pl.BlockSpec((tm, tn), lambda i, j: (jnp.minimum(i, (n_p - 1) // tm), j)),
                  pl.BlockSpec((n_s, tn), lambda i, j: (0, j))],
        out_specs=pl.BlockSpec((tm, tn), lambda i, j: (i, j)),
        out_shape=jax.ShapeDtypeStruct((m, n), F32),
        compiler_params=_cparams(("arbitrary", "arbitrary")),
        name="out_proj",
    )(merged, w, xp, xs)


def _split2(x):
    hi = x.astype(BF16)
    lo = (x - hi.astype(F32)).astype(BF16)
    return hi, lo


def _split3(x):
    hi = x.astype(BF16)
    r1 = x - hi.astype(F32)
    mid = r1.astype(BF16)
    lo = (r1 - mid.astype(F32)).astype(BF16)
    return hi, mid, lo


def _dot_bf16(a, b, dims=(((1,), (0,)), ((), ()))):
    return lax.dot_general(a.astype(BF16), b.astype(BF16), dims, preferred_element_type=F32)


_NT = (((1,), (1,)), ((), ()))
_TN = (((0,), (0,)), ((), ()))


def _seg_sum(x, bsum_ref):
    rows = x.shape[0]
    n_slabs = x.shape[1] // LANES
    xs = jnp.concatenate([x[:, q * LANES:(q + 1) * LANES] for q in range(n_slabs)], axis=0)
    hi, lo = _split2(xs)
    s = _dot(jnp.concatenate([hi, lo], axis=1), bsum_ref[...])
    return jnp.concatenate([s[q * rows:(q + 1) * rows] for q in range(n_slabs)], axis=1)


def _rwkv_prep(r, k, v, z, w0, a0, k_k, k_a, r_k, wd, wa, wg, bsum_ref):
    def lora_up(act, w_ref, first_row, n_rows):
        lo = first_row // LANES * LANES
        hi = -(-(first_row + n_rows) // LANES) * LANES
        return _dot(act(z[:, lo:hi]).astype(BF16), w_ref[lo:hi, :])

    lora_w = lora_up(jnp.tanh, wd, 0, DECAY_RANK)
    lora_a = lora_up(lambda t: t, wa, DECAY_RANK, ICLR_RANK)
    gate = lora_up(_sigmoid, wg, DECAY_RANK + ICLR_RANK, GATE_RANK)
    log_decay = -_sigmoid(w0 + lora_w) * math.exp(-0.5)
    a = _sigmoid(a0 + lora_a)
    kk = k * k_k
    kk = kk * jnp.minimum(lax.rsqrt(_seg_sum(kk * kk, bsum_ref)), 1.0 / L2_EPS)
    k2 = k * (1.0 + (a - 1.0) * k_a)
    bonus = _seg_sum(r * k2 * r_k, bsum_ref) * v
    return log_decay, a, gate, kk, k2, bonus


def _rwkv_finish(y, bonus, gate, ln_g, ln_b, bsum_ref):
    inv = 1.0 / HEAD_DIM
    mu = _seg_sum(y, bsum_ref) * inv
    yc = y - mu
    var = _seg_sum(yc * yc, bsum_ref) * inv
    yn = yc * lax.rsqrt(var + GN_EPS) * ln_g + ln_b
    return (yn + bonus) * gate


def _wkv_prompt_kernel(pr_ref, pk_ref, pv_ref, pz_ref, nr_ref, nk_ref, nv_ref, nz_ref,
                       mur_ref, muk_ref, muv_ref, muz_ref,
                       w0_ref, a0_ref, kk_ref, ka_ref, rk_ref, lng_ref, lnb_ref,
                       wd_ref, wa_ref, wg_ref, bsum_ref, tri_ref,
                       ya_ref, sout_ref,
                       s_scr, cr_scr, ck_scr, cv_scr, cz_scr,
                       at_scr, bt_scr, kt_scr, rt_scr, bh_scr, kh_scr, v_scr, gc_scr, bonus_scr, gate_scr, y_scr,
                       *, n_pairs, pair_unroll):
    c = pl.program_id(1)
    n_chunks = pl.num_programs(1)
    C = WKV_CHUNK
    slot = c % 2
    group = min(4, n_pairs)

    def shift(p_ref, carry, mu_ref, cols):
        p = p_ref[:, cols]
        prev = pltpu.roll(p, 1, axis=0)
        row = lax.broadcasted_iota(jnp.int32, p.shape, 0)
        prev = jnp.where(row == 0, carry[0:1, cols], prev)
        carry[0:1, cols] = p[C - 1:C, :]
        return p + (prev - p) * mu_ref[:, cols]

    def prepare(refs, dst):
        r_ref, k_ref, v_ref, z_ref = refs
        z = shift(z_ref, cz_scr, muz_ref, slice(None))

        def lora_up(act, w_ref, first_row, n_rows):
            lo = first_row // LANES * LANES
            hi = -(-(first_row + n_rows) // LANES) * LANES
            return _dot(act(z[:, lo:hi]).astype(BF16), w_ref[lo:hi, :])

        lora_w = lora_up(jnp.tanh, wd_ref, 0, DECAY_RANK)
        yield
        lora_a = lora_up(lambda t: t, wa_ref, DECAY_RANK, ICLR_RANK)
        yield
        gate_scr[dst] = lora_up(_sigmoid, wg_ref, DECAY_RANK + ICLR_RANK, GATE_RANK)
        yield
        for q0 in range(0, n_pairs, group):
            cols = slice(q0 * LANES, (q0 + group) * LANES)
            r = shift(r_ref, cr_scr, mur_ref, cols)
            k = shift(k_ref, ck_scr, muk_ref, cols)
            v = shift(v_ref, cv_scr, muv_ref, cols)
            yield
            lw = -_sigmoid(w0_ref[:, cols] + lora_w[:, cols]) * math.exp(-0.5)
            a = _sigmoid(a0_ref[:, cols] + lora_a[:, cols])
            kk = k * kk_ref[:, cols]
            yield
            kk = kk * jnp.minimum(lax.rsqrt(_seg_sum(kk * kk, bsum_ref)), 1.0 / L2_EPS)
            yield
            k2 = k * (1.0 + (a - 1.0) * ka_ref[:, cols])
            bonus_scr[dst, :, cols] = _seg_sum(r * k2 * rk_ref[:, cols], bsum_ref) * v
            yield
            cl = _dot(tri_ref[...], jnp.concatenate(_split3(lw), axis=0))
            yield
            cl_end = cl[C - 1:C, :]
            beta = kk * a
            e_neg = jnp.exp(-cl)
            e_hat = jnp.exp(cl_end - cl)
            outs = ((at_scr, -kk * jnp.exp(cl - lw)), (bt_scr, beta * e_neg), (kt_scr, k2 * e_neg),
                    (rt_scr, r * jnp.exp(cl)), (bh_scr, beta * e_hat), (kh_scr, k2 * e_hat), (v_scr, v),
                    (gc_scr, jnp.broadcast_to(jnp.exp(cl_end), (2 * C, cl.shape[1]))))
            for scr, x in outs:
                for q in range(group):
                    scr[dst, q0 + q] = x[:, q * LANES:(q + 1) * LANES]
                yield

    @pl.when(c == 0)
    def _first_chunk():
        s_scr[...] = jnp.zeros_like(s_scr)
        cr_scr[...] = jnp.zeros_like(cr_scr)
        ck_scr[...] = jnp.zeros_like(ck_scr)
        cv_scr[...] = jnp.zeros_like(cv_scr)
        cz_scr[...] = jnp.zeros_like(cz_scr)
        for _ in prepare((pr_ref, pk_ref, pv_ref, pz_ref), 0):
            pass

    lane = lax.broadcasted_iota(jnp.int32, (C, LANES), 1)
    first = lane < HEAD_DIM
    row2 = lax.broadcasted_iota(jnp.int32, (2 * C, 2 * C), 0)
    col2 = lax.broadcasted_iota(jnp.int32, (2 * C, 2 * C), 1)
    cbits = C.bit_length() - 1
    same = (row2 >> cbits) == (col2 >> cbits)
    tri_strict = same & ((row2 & (C - 1)) > (col2 & (C - 1)))
    tri_incl = same & ((row2 & (C - 1)) >= (col2 & (C - 1)))
    eye = row2 == col2

    def stack(x):
        return jnp.concatenate([jnp.where(first, x, 0.0), jnp.where(first, 0.0, x)], axis=0)

    def load_pair(p):
        return (at_scr[slot, p], bt_scr[slot, p], kt_scr[slot, p], rt_scr[slot, p], v_scr[slot, p],
                bh_scr[slot, p], kh_scr[slot, p], gc_scr[slot, p], s_scr[p])

    def compute_pair(vals):
        a_s, b_s, k_s, r_s, v_s, bh_s, kh_s = (stack(x) for x in vals[:7])
        gc, s_t = vals[7], vals[8]
        bk = jnp.concatenate([b_s, k_s], axis=0)
        ga = _dot_bf16(a_s, bk, _NT)
        yield
        gr = lax.dot_general(r_s.astype(BF16), bk.astype(BF16), _NT, preferred_element_type=F32)
        yield
        l_ba = jnp.where(tri_strict, ga[:, :2 * C], 0.0)
        l_ka = jnp.where(tri_strict, ga[:, 2 * C:], 0.0)
        m_br = jnp.where(tri_incl, gr[:, :2 * C], 0.0)
        m_kr = jnp.where(tri_incl, gr[:, 2 * C:], 0.0)
        lkv = _dot_bf16(l_ka, v_s)
        yield
        pw = _dot_bf16(l_ba, l_ba)
        yield
        t_inv = jnp.where(eye, 1.0, l_ba)
        steps = (C - 1).bit_length() - 1
        for step in range(1, steps + 1):
            if step < steps:
                both = _dot_bf16(pw, jnp.concatenate([pw, t_inv], axis=1))
                pw, t_inv = both[:, :2 * C], t_inv + both[:, 2 * C:]
            else:
                t_inv = t_inv + _dot_bf16(pw, t_inv)
            yield
        tx = _dot_bf16(t_inv, jnp.concatenate([a_s, lkv], axis=1))
        yield
        a_hat, u_hat = tx[:, :LANES], tx[:, LANES:]
        u = _dot_bf16(a_hat, s_t) + u_hat
        yield
        uv = jnp.concatenate([u, v_s], axis=0)
        ys = _dot(jnp.concatenate([r_s, m_br, m_kr], axis=1).astype(BF16),
                  jnp.concatenate([s_t, uv], axis=0).astype(BF16))
        yield
        s_new = s_t * gc.T + _dot_bf16(jnp.concatenate([bh_s, kh_s], axis=0), uv, _TN)
        return ys[:C] + ys[C:], s_new

    ahead = prepare((nr_ref, nk_ref, nv_ref, nz_ref), 1 - slot)
    for first_pair in range(0, n_pairs, pair_unroll):
        pairs = range(first_pair, first_pair + pair_unroll)
        gens = [compute_pair(load_pair(p)) for p in pairs]
        results = {}
        while len(results) < len(gens):
            for j, gen in enumerate(gens):
                if j not in results:
                    try:
                        next(gen)
                    except StopIteration as done:
                        results[j] = done.value
            for _ in range(PREP_PIECES_PER_ROUND):
                next(ahead, None)
        for j, p in enumerate(pairs):
            y_scr[p], s_scr[p] = results[j]
    for _ in ahead:
        pass

    y = jnp.concatenate([y_scr[q] for q in range(n_pairs)], axis=1)
    out = _rwkv_finish(y, bonus_scr[slot], gate_scr[slot], lng_ref[...], lnb_ref[...], bsum_ref)
    ya_ref[...] = out.astype(BF16)

    @pl.when(c == n_chunks - 1)
    def _store_state():
        sout_ref[0] = s_scr[...]


def _wkv_prompt(p, n_b, seq, d_a, mu_rkv, mu_z, prm, lora, bsum):
    C = WKV_CHUNK
    n_chunks = seq // C
    n_pairs = d_a // LANES
    zb = (3 * d_a) // LORA_PAD
    row = lambda b, c: b * n_chunks + c
    nxt = lambda b, c: b * n_chunks + jnp.minimum(c + 1, n_chunks - 1)
    vec = lambda w: pl.BlockSpec((1, w), lambda b, c: (0, 0))
    full = lambda s: pl.BlockSpec(s, lambda b, c: (0,) * len(s))
    tri = jnp.asarray(np.tile(np.tril(np.ones((C, C), np.float32)), (1, 3)), BF16)

    def chunk_specs(rows):
        return [pl.BlockSpec((C, d_a), lambda b, c: (rows(b, c), 0)),
                pl.BlockSpec((C, d_a), lambda b, c: (rows(b, c), 1)),
                pl.BlockSpec((C, d_a), lambda b, c: (rows(b, c), 2)),
                pl.BlockSpec((C, LORA_PAD), lambda b, c: (rows(b, c), zb))]

    in_specs = chunk_specs(row) + chunk_specs(nxt) + [vec(d_a), vec(d_a), vec(d_a), vec(LORA_PAD)] + [
        vec(d_a)] * 7 + [full((LORA_PAD, d_a))] * 3 + [full((2 * LANES, LANES)), full((C, 3 * C))]
    scr = [pltpu.VMEM((n_pairs, LANES, LANES), F32),
           pltpu.VMEM((8, d_a), F32), pltpu.VMEM((8, d_a), F32), pltpu.VMEM((8, d_a), F32),
           pltpu.VMEM((8, LORA_PAD), F32)] + [pltpu.VMEM((2, n_pairs, C, LANES), F32)] * 7 + [
           pltpu.VMEM((2, n_pairs, 2 * C, LANES), F32), pltpu.VMEM((2, C, d_a), F32),
           pltpu.VMEM((2, C, d_a), F32), pltpu.VMEM((n_pairs, C, LANES), F32)]
    pair_unroll = math.gcd(n_pairs, WKV_PAIR_UNROLL)
    ya, s_out = pl.pallas_call(
        functools.partial(_wkv_prompt_kernel, n_pairs=n_pairs, pair_unroll=pair_unroll),
        grid=(n_b, n_chunks),
        in_specs=in_specs,
        out_specs=[pl.BlockSpec((C, d_a), lambda b, c: (row(b, c), 0)),
                   pl.BlockSpec((1, n_pairs, LANES, LANES), lambda b, c: (b, 0, 0, 0))],
        out_shape=[jax.ShapeDtypeStruct((n_b * seq, d_a), BF16),
                   jax.ShapeDtypeStruct((n_b, n_pairs, LANES, LANES), F32)],
        scratch_shapes=scr,
        compiler_params=_cparams(("arbitrary", "arbitrary")),
        name="wkv_prompt",
    )(p, p, p, p, p, p, p, p, mu_rkv[0], mu_rkv[1], mu_rkv[2], mu_z, *prm, *lora, bsum, tri)
    return ya, s_out


def _wkv_sample_prep_kernel(pr_ref, pk_ref, pv_ref, pz_ref, sr_ref, sk_ref, sv_ref, sz_ref,
                            mur_ref, muk_ref, muv_ref, muz_ref,
                            w0_ref, a0_ref, kk_ref, ka_ref, rk_ref,
                            wd_ref, wa_ref, wg_ref, bsum_ref,
                            w_out, nkk_out, kka_out, k2_out, r_out, vt_out, bonus_out, gate_out):
    def shift(p_ref, s_ref, mu_ref):
        p = p_ref[...]
        return p + (s_ref[...] - p) * mu_ref[...]

    r = shift(pr_ref, sr_ref, mur_ref)
    k = shift(pk_ref, sk_ref, muk_ref)
    v = shift(pv_ref, sv_ref, muv_ref)
    z = shift(pz_ref, sz_ref, muz_ref)
    lw, a, gate, kk, k2, bonus = _rwkv_prep(
        r, k, v, z, w0_ref[...], a0_ref[...], kk_ref[...], ka_ref[...], rk_ref[...],
        wd_ref, wa_ref, wg_ref, bsum_ref)
    w_out[...] = jnp.exp(lw).T
    nkk_out[...] = (-kk).T
    kka_out[...] = (kk * a).T
    k2_out[...] = k2.T
    r_out[...] = r.T
    vt_out[...] = v.T
    bonus_out[...] = bonus
    gate_out[...] = gate


def _wkv_sample_step_kernel(s_ref, w_ref, nkk_ref, kka_ref, k2_ref, r_ref, vt_ref, snew_ref, yt_ref):
    def body(i, carry):
        s = s_ref[0, i]
        sa = jnp.sum(s * nkk_ref[...], axis=0, keepdims=True)
        s_new = s * w_ref[...] + sa * kka_ref[...] + vt_ref[pl.ds(i, 1), :] * k2_ref[...]
        snew_ref[0, i] = s_new
        yt_ref[pl.ds(i, 1), :] = jnp.sum(s_new * r_ref[...], axis=0, keepdims=True)
        return carry

    lax.fori_loop(0, HEAD_DIM, body, 0, unroll=8)


def _wkv_sample_finish_kernel(yt_ref, bonus_ref, gate_ref, lng_ref, lnb_ref, bsum_ref, ya_ref):
    y = yt_ref[...].T
    out = _rwkv_finish(y, bonus_ref[...], gate_ref[...], lng_ref[...], lnb_ref[...], bsum_ref)
    ya_ref[...] = out.astype(BF16)


def _wkv_sample(p, n_p, n_s, d_a, s_shift, state_hijb, mu_rkv, mu_z, prm, lora, bsum):
    assert n_p % n_s == 0
    rb = n_p // n_s
    zb = (3 * d_a) // LORA_PAD
    n_heads = d_a // HEAD_DIM
    w0, a0, k_k, k_a, r_k, ln_g, ln_b = prm
    full = lambda s: pl.BlockSpec(s, lambda i: (0,) * len(s))
    tok = jax.ShapeDtypeStruct((n_s, d_a), F32)
    outs = pl.pallas_call(
        _wkv_sample_prep_kernel,
        grid=(1,),
        in_specs=[pl.BlockSpec((n_s, d_a), lambda i: (rb, 0)),
                  pl.BlockSpec((n_s, d_a), lambda i: (rb, 1)),
                  pl.BlockSpec((n_s, d_a), lambda i: (rb, 2)),
                  pl.BlockSpec((n_s, LORA_PAD), lambda i: (rb, zb)),
                  full((n_s, d_a)), full((n_s, d_a)), full((n_s, d_a)), full((n_s, LORA_PAD)),
                  full((1, d_a)), full((1, d_a)), full((1, d_a)), full((1, LORA_PAD))]
                 + [full((1, d_a))] * 5 + [full((LORA_PAD, d_a))] * 3
                 + [full((2 * LANES, LANES))],
        out_specs=[full((d_a, n_s))] * 6 + [full((n_s, d_a))] * 2,
        out_shape=[jax.ShapeDtypeStruct((d_a, n_s), F32)] * 6 + [tok] * 2,
        compiler_params=_cparams(("arbitrary",)),
        name="wkv_sample_prep",
    )(p, p, p, p, *s_shift, mu_rkv[0], mu_rkv[1], mu_rkv[2], mu_z,
      w0, a0, k_k, k_a, r_k, *lora, bsum)
    w, nkk, kka, k2, r, vt, bonus, gate = outs
    headspec = pl.BlockSpec((HEAD_DIM, n_s), lambda h: (h, 0))
    stspec = pl.BlockSpec((1, HEAD_DIM, HEAD_DIM, n_s), lambda h: (h, 0, 0, 0))
    s_new, yt = pl.pallas_call(
        _wkv_sample_step_kernel,
        grid=(n_heads,),
        in_specs=[stspec] + [headspec] * 6,
        out_specs=[stspec, headspec],
        out_shape=[jax.ShapeDtypeStruct(state_hijb.shape, F32), jax.ShapeDtypeStruct((d_a, n_s), F32)],
        compiler_params=_cparams(("arbitrary",)),
        name="wkv_sample_step",
    )(state_hijb, w, nkk, kka, k2, r, vt)
    ya = pl.pallas_call(
        _wkv_sample_finish_kernel,
        grid=(1,),
        in_specs=[full((d_a, n_s)), full((n_s, d_a)), full((n_s, d_a)), full((1, d_a)), full((1, d_a)),
                  full((2 * LANES, LANES))],
        out_specs=full((n_s, d_a)),
        out_shape=jax.ShapeDtypeStruct((n_s, d_a), BF16),
        compiler_params=_cparams(("arbitrary",)),
        name="wkv_sample_finish",
    )(yt, bonus, gate, ln_g, ln_b, bsum)
    return ya, s_new


def _pool_prompt_kernel(u_ref, w_ref, sc_ref, o_ref, carry_scr, *, tt, gw):
    t = pl.program_id(1)

    @pl.when(t == 0)
    def _():
        carry_scr[...] = jnp.zeros_like(carry_scr)

    u = u_ref[...]
    pos = t * tt + lax.broadcasted_iota(jnp.int32, (tt, gw), 0)
    for gi, win in enumerate(POOL_WINDOWS):
        cols = slice(gi * gw, (gi + 1) * gw)
        ug = u[:, cols]
        cur = jnp.concatenate([carry_scr[:, cols], ug], axis=0)
        off = 0
        step = 1
        while step < win:
            cur = cur[step:] + cur[:-step]
            off += step
            step *= 2
        wsum = cur[POOL_CARRY - off:POOL_CARRY - off + tt]
        cnt = jnp.minimum(pos + 1, win).astype(F32)
        pooled = wsum / cnt - ug
        y = _dot(pooled.astype(BF16), w_ref[gi].astype(BF16)) * sc_ref[:, cols]
        o_ref[:, cols] = y.astype(BF16)
    carry_scr[...] = u[tt - POOL_CARRY:, :]


def _pool_prompt(p, n_b, seq, pw, w_pool, pool_scale):
    tt = _tile(seq, 256)
    nt = seq // tt
    gw = pw // len(POOL_WINDOWS)
    return pl.pallas_call(
        functools.partial(_pool_prompt_kernel, tt=tt, gw=gw),
        grid=(n_b, nt),
        in_specs=[pl.BlockSpec((tt, pw), lambda b, t: (b * nt + t, 0)),
                  pl.BlockSpec(w_pool.shape, lambda b, t: (0, 0, 0)),
                  pl.BlockSpec((1, pw), lambda b, t: (0, 0))],
        out_specs=pl.BlockSpec((tt, pw), lambda b, t: (b * nt + t, 0)),
        out_shape=jax.ShapeDtypeStruct((n_b * seq, pw), BF16),
        scratch_shapes=[pltpu.VMEM((POOL_CARRY, pw), F32)],
        compiler_params=_cparams(("arbitrary", "arbitrary")),
        name="pool_prompt",
    )(p, w_pool, pool_scale.reshape(1, pw))


def _pool_sample_kernel(u_ref, hist_ref, w_ref, sc_ref, o_ref, *, gw):
    u = u_ref[...]
    for gi, win in enumerate(POOL_WINDOWS):
        cols = slice(gi * gw, (gi + 1) * gw)
        ug = u[:, cols]
        wsum = ug
        for dback in range(1, win):
            wsum = wsum + hist_ref[POOL_BUF - dback, :, cols]
        pooled = wsum / float(win) - ug
        y = _dot(pooled.astype(BF16), w_ref[gi].astype(BF16)) * sc_ref[:, cols]
        o_ref[:, cols] = y.astype(BF16)


def _pool_sample(p, n_p, n_s, pw, hist_t, w_pool, pool_scale):
    gw = pw // len(POOL_WINDOWS)
    full = lambda s: pl.BlockSpec(s, lambda i: (0,) * len(s))
    return pl.pallas_call(
        functools.partial(_pool_sample_kernel, gw=gw),
        grid=(1,),
        in_specs=[pl.BlockSpec((n_s, pw), lambda i: (n_p // n_s, 0)),
                  full(hist_t.shape), full(w_pool.shape), full((1, pw))],
        out_specs=full((n_s, pw)),
        out_shape=jax.ShapeDtypeStruct((n_s, pw), BF16),
        compiler_params=_cparams(("arbitrary",)),
        name="pool_sample",
    )(p, hist_t, w_pool, pool_scale.reshape(1, pw))


def _router_kernel(x_ref, g_ref, wr_ref, h_ref, eid_ref, wt_ref):
    x = x_ref[...]
    h = x * lax.rsqrt(jnp.mean(x * x, axis=-1, keepdims=True) + RMS_EPS) * g_ref[...]
    h_ref[...] = h
    logits = _dot(h.astype(BF16), wr_ref[...])
    lane = lax.broadcasted_iota(jnp.int32, logits.shape, 1)
    neg = jnp.float32(-jnp.inf)
    big = jnp.int32(1 << 20)
    is_g = lane < N_GROUPS
    lg = jnp.where(is_g, logits, neg)
    mg = jnp.max(lg, axis=1, keepdims=True)
    g_sel = jnp.min(jnp.where(is_g & (lg == mg), lane, big), axis=1, keepdims=True)
    p_sel = 1.0 / jnp.sum(jnp.where(is_g, jnp.exp(lg - mg), 0.0), axis=1, keepdims=True)
    e_lane = lane - N_GROUPS
    in_grp = (e_lane >= 0) & (e_lane < N_EXPERTS) & ((e_lane >> 3) == g_sel)
    le = jnp.where(in_grp, logits, neg)
    m1 = jnp.max(le, axis=1, keepdims=True)
    i1 = jnp.min(jnp.where(in_grp & (le == m1), lane, big), axis=1, keepdims=True)
    le2 = jnp.where(lane == i1, neg, le)
    m2 = jnp.max(le2, axis=1, keepdims=True)
    i2 = jnp.min(jnp.where(in_grp & (lane != i1) & (le2 == m2), lane, big), axis=1, keepdims=True)
    e2 = jnp.exp(m2 - m1)
    w1 = p_sel / (1.0 + e2)
    w2 = p_sel * e2 / (1.0 + e2)
    eid_ref[...] = jnp.where(lane == 0, i1 - N_GROUPS, jnp.where(lane == 1, i2 - N_GROUPS, 0))
    wt_ref[...] = jnp.where(lane == 0, w1, jnp.where(lane == 1, w2, 0.0))


def _router(x1, g, w_router):
    n, d = x1.shape
    tr = _tile(n, 256)
    return pl.pallas_call(
        _router_kernel,
        grid=(n // tr,),
        in_specs=[pl.BlockSpec((tr, d), lambda i: (i, 0)),
                  pl.BlockSpec((1, d), lambda i: (0, 0)),
                  pl.BlockSpec((d, LANES), lambda i: (0, 0))],
        out_specs=[pl.BlockSpec((tr, d), lambda i: (i, 0)),
                   pl.BlockSpec((tr, LANES), lambda i: (i, 0)),
                   pl.BlockSpec((tr, LANES), lambda i: (i, 0))],
        out_shape=[jax.ShapeDtypeStruct((n, d), F32),
                   jax.ShapeDtypeStruct((n, LANES), jnp.int32),
                   jax.ShapeDtypeStruct((n, LANES), F32)],
        compiler_params=_cparams(("arbitrary",)),
        name="router",
    )(x1, g.reshape(1, d), w_router)


def _moe_kernel(ge_ref, gs_ref, gn_ref, tok_ref, dst_ref,
                h_hbm, wg_ref, wu_ref, wd_ref, out_hbm,
                xf_scr, xb_scr, acc_scr, sem_g, sem_s, *, rows, n_ftiles):
    g = pl.program_id(0)
    f = pl.program_id(1)
    nrows = gn_ref[g]
    start = gs_ref[g]

    def gather_copy(tok, i):
        return pltpu.make_async_copy(h_hbm.at[pl.ds(tok, 1)], xf_scr.at[pl.ds(i, 1)], sem_g)

    def scatter_copy(i, dst):
        return pltpu.make_async_copy(acc_scr.at[pl.ds(i, 1)], out_hbm.at[pl.ds(dst, 1)], sem_s)

    @pl.when(nrows > 0)
    def _group():
        def issue_gather(first):
            def issue(i, carry):
                gather_copy(tok_ref[first + i], i).start()
                return carry
            lax.fori_loop(0, rows, issue, 0, unroll=8)

        @pl.when(f == 0)
        def _rows_in():
            @pl.when(g == 0)
            def _():
                issue_gather(start)
            pltpu.make_async_copy(h_hbm.at[pl.ds(0, rows)], xf_scr, sem_g).wait()
            xb_scr[...] = xf_scr[...].astype(BF16)

        @pl.when((f == 1) & (gn_ref[g + 1] > 0))
        def _prefetch():
            issue_gather(gs_ref[g + 1])

        def ffn_slice(first):
            x = xb_scr[...]
            hg = _dot(x, wg_ref[0].astype(BF16))
            hu = _dot(x, wu_ref[0].astype(BF16))
            act = (hg * _sigmoid(hg) * hu).astype(BF16)
            part = _dot(act, wd_ref[0].astype(BF16))
            if first:
                acc_scr[...] = part
            else:
                acc_scr[...] += part

        @pl.when(f == 0)
        def _():
            ffn_slice(True)

        @pl.when(f > 0)
        def _():
            ffn_slice(False)

        @pl.when(f == n_ftiles - 1)
        def _scatter():
            nblk = nrows // SCATTER_BLOCK

            def issue_block(b, carry):
                for j in range(SCATTER_BLOCK):
                    i = b * SCATTER_BLOCK + j
                    scatter_copy(i, dst_ref[start + i]).start(priority=j % 2)
                return carry
            lax.fori_loop(0, nblk, issue_block, 0)

            def issue_one(i, carry):
                scatter_copy(i, dst_ref[start + i]).start()
                return carry
            lax.fori_loop(nblk * SCATTER_BLOCK, nrows, issue_one, 0)

            def wait_block(b, carry):
                pltpu.make_async_copy(acc_scr.at[pl.ds(0, SCATTER_BLOCK)], out_hbm.at[pl.ds(0, SCATTER_BLOCK)],
                                      sem_s).wait()
                return carry
            lax.fori_loop(0, nblk, wait_block, 0)

            def wait_one(i, carry):
                scatter_copy(0, 0).wait()
                return carry
            lax.fori_loop(nblk * SCATTER_BLOCK, nrows, wait_one, 0)


def _moe(h2, eid, w_gate, w_up, w_down, rows, tf):
    n, d = h2.shape
    d_e = w_gate.shape[2]
    n_assign = n * TOP_K
    n_ftiles = d_e // tf
    assert n_ftiles >= 2
    max_groups = -(-n_assign // rows) + N_EXPERTS
    eflat = eid.reshape(-1)
    order = jnp.argsort(eflat, stable=True).astype(jnp.int32)
    tok_sorted = order // TOP_K
    dst_sorted = (order % TOP_K) * n + tok_sorted
    counts = jnp.zeros((N_EXPERTS,), jnp.int32).at[eflat].add(1)
    starts = jnp.cumsum(counts) - counts
    groups_per_e = (counts + rows - 1) // rows
    g_ends = jnp.cumsum(groups_per_e)
    gidx = jnp.arange(max_groups, dtype=jnp.int32)
    n_groups = g_ends[-1]
    g_e = jnp.minimum(jnp.searchsorted(g_ends, gidx, side="right"), N_EXPERTS - 1).astype(jnp.int32)
    local = gidx - (g_ends[g_e] - groups_per_e[g_e])
    g_start = starts[g_e] + local * rows
    g_n = jnp.clip(counts[g_e] - local * rows, 0, rows)
    valid = gidx < n_groups
    last_e = g_e[jnp.maximum(n_groups - 1, 0)]
    g_e = jnp.where(valid, g_e, last_e).astype(jnp.int32)
    one = jnp.zeros((1,), jnp.int32)
    g_start = jnp.concatenate([jnp.where(valid, g_start, 0).astype(jnp.int32), one])
    g_n = jnp.concatenate([jnp.where(valid, g_n, 0).astype(jnp.int32), one])
    pad = jnp.zeros((rows,), jnp.int32)
    tok_sorted = jnp.concatenate([tok_sorted, pad])
    dst_sorted = jnp.concatenate([dst_sorted, pad])

    def fsel(g, f, gn):
        return jnp.where(gn[g] > 0, f, n_ftiles - 1)

    grid_spec = pltpu.PrefetchScalarGridSpec(
        num_scalar_prefetch=5,
        grid=(max_groups, n_ftiles),
        in_specs=[pl.BlockSpec(memory_space=pl.ANY),
                  pl.BlockSpec((1, d, tf), lambda g, f, ge, gs, gn, tk, ds: (ge[g], 0, fsel(g, f, gn))),
                  pl.BlockSpec((1, d, tf), lambda g, f, ge, gs, gn, tk, ds: (ge[g], 0, fsel(g, f, gn))),
                  pl.BlockSpec((1, tf, d), lambda g, f, ge, gs, gn, tk, ds: (ge[g], fsel(g, f, gn), 0))],
        out_specs=pl.BlockSpec(memory_space=pl.ANY),
        scratch_shapes=[pltpu.VMEM((rows, d), F32), pltpu.VMEM((rows, d), BF16), pltpu.VMEM((rows, d), F32),
                        pltpu.SemaphoreType.DMA(()), pltpu.SemaphoreType.DMA(())],
    )
    return pl.pallas_call(
        functools.partial(_moe_kernel, rows=rows, n_ftiles=n_ftiles),
        grid_spec=grid_spec,
        out_shape=jax.ShapeDtypeStruct((TOP_K * n, d), F32),
        compiler_params=_cparams(("arbitrary", "arbitrary")),
        name="moe_experts",
    )(g_e, g_start, g_n, tok_sorted, dst_sorted, h2, w_gate, w_up, w_down)


def _combine_kernel(x_ref, y0_ref, y1_ref, wt_ref, g_ref, op_ref, os_ref, *, n_prompt_tiles):
    i = pl.program_id(0)
    wt = wt_ref[...]
    x = x_ref[...] + (y0_ref[...] * wt[:, 0:1] + y1_ref[...] * wt[:, 1:2])
    y = x * lax.rsqrt(jnp.mean(x * x, axis=-1, keepdims=True) + RMS_EPS) * g_ref[...]

    @pl.when(i < n_prompt_tiles)
    def _():
        op_ref[...] = y

    @pl.when(i >= n_prompt_tiles)
    def _():
        os_ref[...] = y


def _combine(x1, y2, wt, g, n_p, n_s):
    n, d = x1.shape
    tr = _tile(n_s, 256)
    npt, nst = n_p // tr, n_s // tr
    nt = npt + nst
    return pl.pallas_call(
        functools.partial(_combine_kernel, n_prompt_tiles=npt),
        grid=(nt,),
        in_specs=[pl.BlockSpec((tr, d), lambda i: (i, 0)),
                  pl.BlockSpec((tr, d), lambda i: (i, 0)),
                  pl.BlockSpec((tr, d), lambda i: (i + nt, 0)),
                  pl.BlockSpec((tr, LANES), lambda i: (i, 0)),
                  pl.BlockSpec((1, d), lambda i: (0, 0))],
        out_specs=[pl.BlockSpec((tr, d), lambda i: (jnp.minimum(i, npt - 1), 0)),
                   pl.BlockSpec((tr, d), lambda i: (jnp.maximum(i - npt, 0), 0))],
        out_shape=[jax.ShapeDtypeStruct((n_p, d), F32), jax.ShapeDtypeStruct((n_s, d), F32)],
        compiler_params=_cparams(("arbitrary",)),
        name="combine",
    )(x1, y2, y2, wt, g.reshape(1, d))


def _layer(xp, xs, st_wkv, st_shift, st_pool, norm_out_g, norm1_g, w_in, mu_shift, w0, w_decay_up, a0,
           w_iclr_up, w_gate_up, k_k, k_a, r_k, ln_x_g, ln_x_b, w_pool, pool_scale,
           w_branch_a, w_branch_b, w_out, norm2_g, w_router_group, w_router_expert,
           w_exp_gate, w_exp_up, w_exp_down):
    n_b, seq, d = xp.shape
    n_sb = xs.shape[0]
    n_p, n_s = n_b * seq, n_sb * xs.shape[1]
    n = n_p + n_s
    d_a = w_branch_a.shape[0]
    pw = w_branch_b.shape[0]
    sw = 3 * d_a + LORA_W
    assert d_a == pw and d == 2 * d_a and xs.shape[1] == 1 and w_in.shape[1] == sw + pw + 2 * d

    zpad = LORA_PAD - LORA_W
    w_in_t = w_in.T
    head_w = 3 * d_a + LORA_PAD
    pieces = lambda t: ([t[..., i * d_a:(i + 1) * d_a] for i in range(3)],
                        jnp.pad(t[..., 3 * d_a:sw], [(0, 0)] * (t.ndim - 1) + [(0, zpad)]))
    mu_rkv, mu_z = pieces(mu_shift.reshape(1, sw))
    ss_rkv, ss_z = pieces(st_shift)
    vec = lambda t: t.reshape(1, d_a)
    prm = (vec(w0), vec(a0), vec(k_k), vec(k_a), vec(r_k), vec(ln_x_g), vec(ln_x_b))
    lpad = lambda w, r0: jnp.zeros((LORA_PAD, d_a), F32).at[r0:r0 + w.shape[0]].set(w).astype(BF16)
    lora = (lpad(w_decay_up, 0), lpad(w_iclr_up, DECAY_RANK), lpad(w_gate_up, DECAY_RANK + ICLR_RANK))
    lane_head = np.arange(LANES) // HEAD_DIM
    bsum = jnp.asarray(np.tile((lane_head[:, None] == lane_head[None, :]).astype(np.float32), (2, 1)), BF16)

    tm = _tile(n, 1664, 8)
    xp2, xs2 = xp.reshape(n_p, d), xs.reshape(n_s, d)
    h = _norm1(xp2, xs2, norm1_g)
    tn = _tile(d, 256, LANES)
    p_head = _matmul_nt(h, w_in_t, 0, head_w, tm, tn, "in_proj_head")
    p_tail = _matmul_nt(h, w_in_t, sw, pw + 2 * d, tm, tn, "in_proj_tail")

    ya_p, s_pairs = _wkv_prompt(p_head, n_b, seq, d_a, mu_rkv, mu_z, prm, lora, bsum)
    ya_s, new_wkv_s = _wkv_sample(p_head, n_p, n_s, d_a, (*ss_rkv, ss_z), jnp.transpose(st_wkv, (1, 2, 3, 0)),
                                  mu_rkv, mu_z, prm, lora, bsum)
    yb_p = _pool_prompt(p_tail, n_b, seq, pw, w_pool, pool_scale)
    yb_s = _pool_sample(p_tail, n_p, n_s, pw, jnp.swapaxes(st_pool, 0, 1), w_pool, pool_scale)
    ya = jnp.concatenate([ya_p, ya_s], axis=0)
    yb = jnp.concatenate([yb_p, yb_s], axis=0)

    merged = _merge(ya, yb, w_branch_a, w_branch_b, p_tail, pw, d, tm, tn)
    x1 = _outproj(merged, w_out, xp2, xs2, tm, tn)

    w_router = jnp.concatenate([w_router_group, w_router_expert,
                                jnp.zeros((d, LANES - N_GROUPS - N_EXPERTS), F32)], axis=1).astype(BF16)
    h2, eid, wt = _router(x1, norm2_g, w_router)
    d_e = w_exp_gate.shape[2]
    y2 = _moe(h2, eid[:, :TOP_K], w_exp_gate, w_exp_up, w_exp_down,
              rows=MOE_GROUP_ROWS if n * TOP_K >= 4096 else 64, tf=_tile(d_e, min(256, d_e // 2), LANES))
    y_p, y_s = _combine(x1, y2, wt, norm_out_g, n_p, n_s)

    n_heads = d_a // HEAD_DIM
    sp = s_pairs.reshape(n_b, n_heads // 2, 2, HEAD_DIM, 2, HEAD_DIM)
    new_wkv_p = jnp.stack([sp[:, :, 0, :, 0, :], sp[:, :, 1, :, 1, :]], axis=2)
    new_wkv_p = jnp.swapaxes(new_wkv_p.reshape(n_b, n_heads, HEAD_DIM, HEAD_DIM), -1, -2)
    new_wkv_s = jnp.transpose(new_wkv_s, (3, 0, 1, 2))
    new_shift_p = jnp.stack([p_head[b * seq + seq - 1, :sw] for b in range(n_b)])
    new_shift_s = p_head[n_p:, :sw]
    new_pool_p = jnp.stack([p_tail[b * seq + seq - POOL_BUF:(b + 1) * seq, :pw] for b in range(n_b)])
    new_pool_s = jnp.concatenate([st_pool[:, 1:], p_tail[n_p:, :pw][:, None, :]], axis=1)
    return (y_p.reshape(n_b, seq, d), y_s.reshape(n_sb, 1, d),
            new_wkv_p, new_shift_p, new_pool_p, new_wkv_s, new_shift_s, new_pool_s)


def kernel(x_prompt, x_sample, state_wkv, state_shift, state_pool, norm1_g, w_in, mu_shift, w0, w_decay_up, a0, w_iclr_up, w_gate_up, k_k, k_a, r_k, ln_x_g, ln_x_b, w_pool, pool_scale, w_branch_a, w_branch_b, w_out, norm2_g, w_router_group, w_router_expert, w_exp_gate, w_exp_up, w_exp_down, norm_f_g):
    assert norm1_g.shape[0] == 1, "single-layer trunk"
    outs = _layer(x_prompt, x_sample, state_wkv[0], state_shift[0], state_pool[0], norm_f_g,
                  norm1_g[0], w_in[0], mu_shift[0], w0[0], w_decay_up[0], a0[0], w_iclr_up[0], w_gate_up[0],
                  k_k[0], k_a[0], r_k[0], ln_x_g[0], ln_x_b[0], w_pool[0], pool_scale[0],
                  w_branch_a[0], w_branch_b[0], w_out[0], norm2_g[0], w_router_group[0], w_router_expert[0],
                  w_exp_gate[0], w_exp_up[0], w_exp_down[0])
    y_p, y_s = outs[0], outs[1]
    return (y_p, y_s) + tuple(o[None] for o in outs[2:])
```

```python
import functools
import math

import jax
import jax.numpy as jnp
import numpy as np
from jax import lax
from jax.experimental import pallas as pl
from jax.experimental.pallas import tpu as pltpu

F32 = jnp.float32
BF16 = jnp.bfloat16

HEAD_DIM = 64
LANES = 128
DECAY_RANK = 96
ICLR_RANK = 96
GATE_RANK = 256
LORA_W = DECAY_RANK + ICLR_RANK + GATE_RANK
LORA_PAD = 512
POOL_WINDOWS = (2, 4, 8, 16)
POOL_BUF = 15
POOL_CARRY = 16
N_GROUPS = 4
EXPERTS_PER_GROUP = 8
N_EXPERTS = N_GROUPS * EXPERTS_PER_GROUP
TOP_K = 2
RMS_EPS = 1e-6
GN_EPS = 6.4e-4
L2_EPS = 1e-12
WKV_CHUNK = 64
SCATTER_BLOCK = 8
MOE_GROUP_ROWS = 576
WKV_PAIR_UNROLL = 8
PREP_PIECES_PER_ROUND = 1
VMEM_LIMIT = 56 * 1024 * 1024


def _cparams(sem, vmem=VMEM_LIMIT):
    return pltpu.CompilerParams(dimension_semantics=sem, vmem_limit_bytes=vmem)


def _tile(n, target, mult=8):
    best = None
    for t in range(mult, min(n, target) + 1, mult):
        if n % t == 0:
            best = t
    return best if best is not None else n


def _dot(a, b):
    return jnp.dot(a, b, preferred_element_type=F32)


def _sigmoid(x):
    return 0.5 * jnp.tanh(0.5 * x) + 0.5


def _norm1_kernel(xp_ref, xs_ref, g_ref, h_ref, *, n_prompt_tiles):
    i = pl.program_id(0)

    def body(x):
        xn = x * lax.rsqrt(jnp.mean(x * x, axis=-1, keepdims=True) + RMS_EPS)
        h_ref[...] = (xn * g_ref[...]).astype(BF16)

    @pl.when(i < n_prompt_tiles)
    def _():
        body(xp_ref[...])

    @pl.when(i >= n_prompt_tiles)
    def _():
        body(xs_ref[...])


def _norm1(xp, xs, g):
    n_p, d = xp.shape
    n_s = xs.shape[0]
    tr = _tile(n_s, 256)
    assert n_p % tr == 0
    npt, nst = n_p // tr, n_s // tr
    return pl.pallas_call(
        functools.partial(_norm1_kernel, n_prompt_tiles=npt),
        grid=(npt + nst,),
        in_specs=[pl.BlockSpec((tr, d), lambda i: (jnp.minimum(i, npt - 1), 0)),
                  pl.BlockSpec((tr, d), lambda i: (jnp.maximum(i - npt, 0), 0)),
                  pl.BlockSpec((1, d), lambda i: (0, 0))],
        out_specs=pl.BlockSpec((tr, d), lambda i: (i, 0)),
        out_shape=jax.ShapeDtypeStruct((n_p + n_s, d), BF16),
        compiler_params=_cparams(("arbitrary",)),
        name="norm1",
    )(xp, xs, g.reshape(1, d))


def _mm_nt_kernel(a_ref, wt_ref, o_ref):
    o_ref[...] = lax.dot_general(a_ref[...], wt_ref[...].astype(BF16), _NT, preferred_element_type=F32)


def _matmul_nt(a, wt, row0, n_out, tm, tn, name):
    m, k = a.shape
    assert row0 % 8 == 0
    return pl.pallas_call(
        _mm_nt_kernel,
        grid=(m // tm, n_out // tn),
        in_specs=[pl.BlockSpec((tm, k), lambda i, j: (i, 0)),
                  pl.BlockSpec((pl.Element(tn), pl.Element(k)), lambda i, j: (pl.multiple_of(row0 + j * tn, 8), 0))],
        out_specs=pl.BlockSpec((tm, tn), lambda i, j: (i, j)),
        out_shape=jax.ShapeDtypeStruct((m, n_out), F32),
        compiler_params=_cparams(("arbitrary", "arbitrary")),
        name=name,
    )(a, wt)


def _merge_kernel(ya_ref, yb_ref, wa_ref, wb_ref, ga_ref, gb_ref, o_ref):
    a = _dot(ya_ref[...], wa_ref[...].astype(BF16))
    b = _dot(yb_ref[...], wb_ref[...].astype(BF16))
    o_ref[...] = (_sigmoid(ga_ref[...]) * a + _sigmoid(gb_ref[...]) * b).astype(BF16)


def _merge(ya, yb, wa, wb, p, gate_col, d, tm, tn):
    m, k = ya.shape
    ga0 = gate_col // tn
    gb0 = (gate_col + d) // tn
    return pl.pallas_call(
        _merge_kernel,
        grid=(m // tm, d // tn),
        in_specs=[pl.BlockSpec((tm, k), lambda i, j: (i, 0)),
                  pl.BlockSpec((tm, k), lambda i, j: (i, 0)),
                  pl.BlockSpec((k, tn), lambda i, j: (0, j)),
                  pl.BlockSpec((k, tn), lambda i, j: (0, j)),
                  pl.BlockSpec((tm, tn), lambda i, j: (i, ga0 + j)),
                  pl.BlockSpec((tm, tn), lambda i, j: (i, gb0 + j))],
        out_specs=pl.BlockSpec((tm, tn), lambda i, j: (i, j)),
        out_shape=jax.ShapeDtypeStruct((m, d), BF16),
        compiler_params=_cparams(("arbitrary", "arbitrary")),
        name="merge",
    )(ya, yb, wa, wb, p, p)


def _outproj_kernel(m_ref, w_ref, xp_ref, xs_ref, o_ref, *, n_prompt_tiles, n_prompt_tail):
    i = pl.program_id(0)
    y = _dot(m_ref[...], w_ref[...].astype(BF16))

    @pl.when(i < n_prompt_tiles)
    def _():
        o_ref[...] = xp_ref[...] + y

    @pl.when(i == n_prompt_tiles)
    def _():
        if n_prompt_tail:
            o_ref[:n_prompt_tail, :] = xp_ref[:n_prompt_tail, :] + y[:n_prompt_tail]
        o_ref[n_prompt_tail:, :] = xs_ref[...] + y[n_prompt_tail:]


def _outproj(merged, w, xp, xs, tm, tn):
    m, k = merged.shape
    n = w.shape[1]
    n_p, n_s = xp.shape[0], xs.shape[0]
    full, tail = divmod(n_p, tm)
    assert m == n_p + n_s and tail + n_s == tm and tail % 8 == 0
    return pl.pallas_call(
        functools.partial(_outproj_kernel, n_prompt_tiles=full, n_prompt_tail=tail),
        grid=(m // tm, n // tn),
        in_specs=[pl.BlockSpec((tm, k), lambda i, j: (i, 0)),
                  pl.BlockSpec((k, tn), lambda i, j: (0, j)),
                  pl.BlockSpec((tm, tn), lambda i, j: (jnp.minimum(i, (n_p - 1) // tm), j)),
                  pl.BlockSpec((n_s, tn), lambda i, j: (0, j))],
        out_specs=pl.BlockSpec((tm, tn), lambda i, j: (i, j)),
        out_shape=jax.ShapeDtypeStruct((m, n), F32),
        compiler_params=_cparams(("arbitrary", "arbitrary")),
        name="out_proj",
    )(merged, w, xp, xs)


def _split2(x):
    hi = x.astype(BF16)
    lo = (x - hi.astype(F32)).astype(BF16)
    return hi, lo


def _split3(x):
    hi = x.astype(BF16)
    r1 = x - hi.astype(F32)
    mid = r1.astype(BF16)
    lo = (r1 - mid.astype(F32)).astype(BF16)
    return hi, mid, lo


def _dot_bf16(a, b, dims=(((1,), (0,)), ((), ()))):
    return lax.dot_general(a.astype(BF16), b.astype(BF16), dims, preferred_element_type=F32)


_NT = (((1,), (1,)), ((), ()))
_TN = (((0,), (0,)), ((), ()))


def _seg_sum(x, bsum_ref):
    rows = x.shape[0]
    n_slabs = x.shape[1] // LANES
    xs = jnp.concatenate([x[:, q * LANES:(q + 1) * LANES] for q in range(n_slabs)], axis=0)
    hi, lo = _split2(xs)
    s = _dot(jnp.concatenate([hi, lo], axis=1), bsum_ref[...])
    return jnp.concatenate([s[q * rows:(q + 1) * rows] for q in range(n_slabs)], axis=1)


def _rwkv_prep(r, k, v, z, w0, a0, k_k, k_a, r_k, wd, wa, wg, bsum_ref):
    def lora_up(act, w_ref, first_row, n_rows):
        lo = first_row // LANES * LANES
        hi = -(-(first_row + n_rows) // LANES) * LANES
        return _dot(act(z[:, lo:hi]).astype(BF16), w_ref[lo:hi, :])

    lora_w = lora_up(jnp.tanh, wd, 0, DECAY_RANK)
    lora_a = lora_up(lambda t: t, wa, DECAY_RANK, ICLR_RANK)
    gate = lora_up(_sigmoid, wg, DECAY_RANK + ICLR_RANK, GATE_RANK)
    log_decay = -_sigmoid(w0 + lora_w) * math.exp(-0.5)
    a = _sigmoid(a0 + lora_a)
    kk = k * k_k
    kk = kk * jnp.minimum(lax.rsqrt(_seg_sum(kk * kk, bsum_ref)), 1.0 / L2_EPS)
    k2 = k * (1.0 + (a - 1.0) * k_a)
    bonus = _seg_sum(r * k2 * r_k, bsum_ref) * v
    return log_decay, a, gate, kk, k2, bonus


def _rwkv_finish(y, bonus, gate, ln_g, ln_b, bsum_ref):
    inv = 1.0 / HEAD_DIM
    mu = _seg_sum(y, bsum_ref) * inv
    yc = y - mu
    var = _seg_sum(yc * yc, bsum_ref) * inv
    yn = yc * lax.rsqrt(var + GN_EPS) * ln_g + ln_b
    return (yn + bonus) * gate


def _wkv_prompt_kernel(pr_ref, pk_ref, pv_ref, pz_ref, nr_ref, nk_ref, nv_ref, nz_ref,
                       mur_ref, muk_ref, muv_ref, muz_ref,
                       w0_ref, a0_ref, kk_ref, ka_ref, rk_ref, lng_ref, lnb_ref,
                       wd_ref, wa_ref, wg_ref, bsum_ref, tri_ref,
                       ya_ref, sout_ref,
                       s_scr, cr_scr, ck_scr, cv_scr, cz_scr,
                       at_scr, bt_scr, kt_scr, rt_scr, bh_scr, kh_scr, v_scr, gc_scr, bonus_scr, gate_scr, y_scr,
                       *, n_pairs, pair_unroll):
    c = pl.program_id(1)
    n_chunks = pl.num_programs(1)
    C = WKV_CHUNK
    slot = c % 2
    group = min(4, n_pairs)

    def shift(p_ref, carry, mu_ref, cols):
        p = p_ref[:, cols]
        prev = pltpu.roll(p, 1, axis=0)
        row = lax.broadcasted_iota(jnp.int32, p.shape, 0)
        prev = jnp.where(row == 0, carry[0:1, cols], prev)
        carry[0:1, cols] = p[C - 1:C, :]
        return p + (prev - p) * mu_ref[:, cols]

    def prepare(refs, dst):
        r_ref, k_ref, v_ref, z_ref = refs
        z = shift(z_ref, cz_scr, muz_ref, slice(None))

        def lora_up(act, w_ref, first_row, n_rows):
            lo = first_row // LANES * LANES
            hi = -(-(first_row + n_rows) // LANES) * LANES
            return _dot(act(z[:, lo:hi]).astype(BF16), w_ref[lo:hi, :])

        lora_w = lora_up(jnp.tanh, wd_ref, 0, DECAY_RANK)
        yield
        lora_a = lora_up(lambda t: t, wa_ref, DECAY_RANK, ICLR_RANK)
        yield
        gate_scr[dst] = lora_up(_sigmoid, wg_ref, DECAY_RANK + ICLR_RANK, GATE_RANK)
        yield
        for q0 in range(0, n_pairs, group):
            cols = slice(q0 * LANES, (q0 + group) * LANES)
            r = shift(r_ref, cr_scr, mur_ref, cols)
            k = shift(k_ref, ck_scr, muk_ref, cols)
            v = shift(v_ref, cv_scr, muv_ref, cols)
            yield
            lw = -_sigmoid(w0_ref[:, cols] + lora_w[:, cols]) * math.exp(-0.5)
            a = _sigmoid(a0_ref[:, cols] + lora_a[:, cols])
            kk = k * kk_ref[:, cols]
            yield
            kk = kk * jnp.minimum(lax.rsqrt(_seg_sum(kk * kk, bsum_ref)), 1.0 / L2_EPS)
            yield
            k2 = k * (1.0 + (a - 1.0) * ka_ref[:, cols])
            bonus_scr[dst, :, cols] = _seg_sum(r * k2 * rk_ref[:, cols], bsum_ref) * v
            yield
            cl = _dot(tri_ref[...], jnp.concatenate(_split3(lw), axis=0))
            yield
            cl_end = cl[C - 1:C, :]
            beta = kk * a
            e_neg = jnp.exp(-cl)
            e_hat = jnp.exp(cl_end - cl)
            outs = ((at_scr, -kk * jnp.exp(cl - lw)), (bt_scr, beta * e_neg), (kt_scr, k2 * e_neg),
                    (rt_scr, r * jnp.exp(cl)), (bh_scr, beta * e_hat), (kh_scr, k2 * e_hat), (v_scr, v),
                    (gc_scr, jnp.broadcast_to(jnp.exp(cl_end), (2 * C, cl.shape[1]))))
            for scr, x in outs:
                for q in range(group):
                    scr[dst, q0 + q] = x[:, q * LANES:(q + 1) * LANES]
                yield

    @pl.when(c == 0)
    def _first_chunk():
        s_scr[...] = jnp.zeros_like(s_scr)
        cr_scr[...] = jnp.zeros_like(cr_scr)
        ck_scr[...] = jnp.zeros_like(ck_scr)
        cv_scr[...] = jnp.zeros_like(cv_scr)
        cz_scr[...] = jnp.zeros_like(cz_scr)
        for _ in prepare((pr_ref, pk_ref, pv_ref, pz_ref), 0):
            pass

    lane = lax.broadcasted_iota(jnp.int32, (C, LANES), 1)
    first = lane < HEAD_DIM
    row2 = lax.broadcasted_iota(jnp.int32, (2 * C, 2 * C), 0)
    col2 = lax.broadcasted_iota(jnp.int32, (2 * C, 2 * C), 1)
    cbits = C.bit_length() - 1
    same = (row2 >> cbits) == (col2 >> cbits)
    tri_strict = same & ((row2 & (C - 1)) > (col2 & (C - 1)))
    tri_incl = same & ((row2 & (C - 1)) >= (col2 & (C - 1)))
    eye = row2 == col2

    def stack(x):
        return jnp.concatenate([jnp.where(first, x, 0.0), jnp.where(first, 0.0, x)], axis=0)

    def load_pair(p):
        return (at_scr[slot, p], bt_scr[slot, p], kt_scr[slot, p], rt_scr[slot, p], v_scr[slot, p],
                bh_scr[slot, p], kh_scr[slot, p], gc_scr[slot, p], s_scr[p])

    def compute_pair(vals):
        a_s, b_s, k_s, r_s, v_s, bh_s, kh_s = (stack(x) for x in vals[:7])
        gc, s_t = vals[7], vals[8]
        bk = jnp.concatenate([b_s, k_s], axis=0)
        ga = _dot_bf16(a_s, bk, _NT)
        yield
        gr = lax.dot_general(r_s.astype(BF16), bk.astype(BF16), _NT, preferred_element_type=F32)
        yield
        l_ba = jnp.where(tri_strict, ga[:, :2 * C], 0.0)
        l_ka = jnp.where(tri_strict, ga[:, 2 * C:], 0.0)
        m_br = jnp.where(tri_incl, gr[:, :2 * C], 0.0)
        m_kr = jnp.where(tri_incl, gr[:, 2 * C:], 0.0)
        lkv = _dot_bf16(l_ka, v_s)
        yield
        pw = _dot_bf16(l_ba, l_ba)
        yield
        t_inv = jnp.where(eye, 1.0, l_ba)
        steps = (C - 1).bit_length() - 1
        for step in range(1, steps + 1):
            if step < steps:
                both = _dot_bf16(pw, jnp.concatenate([pw, t_inv], axis=1))
                pw, t_inv = both[:, :2 * C], t_inv + both[:, 2 * C:]
            else:
                t_inv = t_inv + _dot_bf16(pw, t_inv)
            yield
        tx = _dot_bf16(t_inv, jnp.concatenate([a_s, lkv], axis=1))
        yield
        a_hat, u_hat = tx[:, :LANES], tx[:, LANES:]
        u = _dot_bf16(a_hat, s_t) + u_hat
        yield
        uv = jnp.concatenate([u, v_s], axis=0)
        ys = _dot(jnp.concatenate([r_s, m_br, m_kr], axis=1).astype(BF16),
                  jnp.concatenate([s_t, uv], axis=0).astype(BF16))
        yield
        s_new = s_t * gc.T + _dot_bf16(jnp.concatenate([bh_s, kh_s], axis=0), uv, _TN)
        return ys[:C] + ys[C:], s_new

    ahead = prepare((nr_ref, nk_ref, nv_ref, nz_ref), 1 - slot)
    for first_pair in range(0, n_pairs, pair_unroll):
        pairs = range(first_pair, first_pair + pair_unroll)
        gens = [compute_pair(load_pair(p)) for p in pairs]
        results = {}
        while len(results) < len(gens):
            for j, gen in enumerate(gens):
                if j not in results:
                    try:
                        next(gen)
                    except StopIteration as done:
                        results[j] = done.value
            for _ in range(PREP_PIECES_PER_ROUND):
                next(ahead, None)
        for j, p in enumerate(pairs):
            y_scr[p], s_scr[p] = results[j]
    for _ in ahead:
        pass

    y = jnp.concatenate([y_scr[q] for q in range(n_pairs)], axis=1)
    out = _rwkv_finish(y, bonus_scr[slot], gate_scr[slot], lng_ref[...], lnb_ref[...], bsum_ref)
    ya_ref[...] = out.astype(BF16)

    @pl.when(c == n_chunks - 1)
    def _store_state():
        for q in range(n_pairs):
            t = s_scr[q].T
            sout_ref[0, 2 * q] = t[:HEAD_DIM, :HEAD_DIM]
            sout_ref[0, 2 * q + 1] = t[HEAD_DIM:, HEAD_DIM:]


def _wkv_prompt(p, n_b, seq, d_a, mu_rkv, mu_z, prm, lora, bsum):
    C = WKV_CHUNK
    n_chunks = seq // C
    n_pairs = d_a // LANES
    zb = (3 * d_a) // LORA_PAD
    row = lambda b, c: b * n_chunks + c
    nxt = lambda b, c: b * n_chunks + jnp.minimum(c + 1, n_chunks - 1)
    vec = lambda w: pl.BlockSpec((1, w), lambda b, c: (0, 0))
    full = lambda s: pl.BlockSpec(s, lambda b, c: (0,) * len(s))
    tri = jnp.asarray(np.tile(np.tril(np.ones((C, C), np.float32)), (1, 3)), BF16)

    def chunk_specs(rows):
        return [pl.BlockSpec((C, d_a), lambda b, c: (rows(b, c), 0)),
                pl.BlockSpec((C, d_a), lambda b, c: (rows(b, c), 1)),
                pl.BlockSpec((C, d_a), lambda b, c: (rows(b, c), 2)),
                pl.BlockSpec((C, LORA_PAD), lambda b, c: (rows(b, c), zb))]

    in_specs = chunk_specs(row) + chunk_specs(nxt) + [vec(d_a), vec(d_a), vec(d_a), vec(LORA_PAD)] + [
        vec(d_a)] * 7 + [full((LORA_PAD, d_a))] * 3 + [full((2 * LANES, LANES)), full((C, 3 * C))]
    scr = [pltpu.VMEM((n_pairs, LANES, LANES), F32),
           pltpu.VMEM((8, d_a), F32), pltpu.VMEM((8, d_a), F32), pltpu.VMEM((8, d_a), F32),
           pltpu.VMEM((8, LORA_PAD), F32)] + [pltpu.VMEM((2, n_pairs, C, LANES), F32)] * 7 + [
           pltpu.VMEM((2, n_pairs, 2 * C, LANES), F32), pltpu.VMEM((2, C, d_a), F32),
           pltpu.VMEM((2, C, d_a), F32), pltpu.VMEM((n_pairs, C, LANES), F32)]
    pair_unroll = math.gcd(n_pairs, WKV_PAIR_UNROLL)
    ya, s_out = pl.pallas_call(
        functools.partial(_wkv_prompt_kernel, n_pairs=n_pairs, pair_unroll=pair_unroll),
        grid=(n_b, n_chunks),
        in_specs=in_specs,
        out_specs=[pl.BlockSpec((C, d_a), lambda b, c: (row(b, c), 0)),
                   pl.BlockSpec((1, 2 * n_pairs, HEAD_DIM, HEAD_DIM), lambda b, c: (b, 0, 0, 0))],
        out_shape=[jax.ShapeDtypeStruct((n_b * seq, d_a), BF16),
                   jax.ShapeDtypeStruct((n_b, 2 * n_pairs, HEAD_DIM, HEAD_DIM), F32)],
        scratch_shapes=scr,
        compiler_params=_cparams(("arbitrary", "arbitrary")),
        name="wkv_prompt",
    )(p, p, p, p, p, p, p, p, mu_rkv[0], mu_rkv[1], mu_rkv[2], mu_z, *prm, *lora, bsum, tri)
    return ya, s_out


def _wkv_sample_prep_kernel(pr_ref, pk_ref, pv_ref, pz_ref, sr_ref, sk_ref, sv_ref, sz_ref,
                            mur_ref, muk_ref, muv_ref, muz_ref,
                            w0_ref, a0_ref, kk_ref, ka_ref, rk_ref,
                            wd_ref, wa_ref, wg_ref, bsum_ref,
                            w_out, nkk_out, kka_out, k2_out, r_out, vt_out, bonus_out, gate_out):
    def shift(p_ref, s_ref, mu_ref):
        p = p_ref[...]
        return p + (s_ref[...] - p) * mu_ref[...]

    r = shift(pr_ref, sr_ref, mur_ref)
    k = shift(pk_ref, sk_ref, muk_ref)
    v = shift(pv_ref, sv_ref, muv_ref)
    z = shift(pz_ref, sz_ref, muz_ref)
    lw, a, gate, kk, k2, bonus = _rwkv_prep(
        r, k, v, z, w0_ref[...], a0_ref[...], kk_ref[...], ka_ref[...], rk_ref[...],
        wd_ref, wa_ref, wg_ref, bsum_ref)
    w_out[...] = jnp.exp(lw).T
    nkk_out[...] = (-kk).T
    kka_out[...] = (kk * a).T
    k2_out[...] = k2.T
    r_out[...] = r.T
    vt_out[...] = v.T
    bonus_out[...] = bonus
    gate_out[...] = gate


def _wkv_sample_step_kernel(s_ref, w_ref, nkk_ref, kka_ref, k2_ref, r_ref, vt_ref, snew_ref, yt_ref):
    def body(i, carry):
        s = s_ref[0, i]
        sa = jnp.sum(s * nkk_ref[...], axis=0, keepdims=True)
        s_new = s * w_ref[...] + sa * kka_ref[...] + vt_ref[pl.ds(i, 1), :] * k2_ref[...]
        snew_ref[0, i] = s_new
        yt_ref[pl.ds(i, 1), :] = jnp.sum(s_new * r_ref[...], axis=0, keepdims=True)
        return carry

    lax.fori_loop(0, HEAD_DIM, body, 0, unroll=8)


def _wkv_sample_finish_kernel(yt_ref, bonus_ref, gate_ref, lng_ref, lnb_ref, bsum_ref, ya_ref):
    y = yt_ref[...].T
    out = _rwkv_finish(y, bonus_ref[...], gate_ref[...], lng_ref[...], lnb_ref[...], bsum_ref)
    ya_ref[...] = out.astype(BF16)


def _wkv_sample(p, n_p, n_s, d_a, s_shift, state_hijb, mu_rkv, mu_z, prm, lora, bsum):
    assert n_p % n_s == 0
    rb = n_p // n_s
    zb = (3 * d_a) // LORA_PAD
    n_heads = d_a // HEAD_DIM
    w0, a0, k_k, k_a, r_k, ln_g, ln_b = prm
    full = lambda s: pl.BlockSpec(s, lambda i: (0,) * len(s))
    tok = jax.ShapeDtypeStruct((n_s, d_a), F32)
    outs = pl.pallas_call(
        _wkv_sample_prep_kernel,
        grid=(1,),
        in_specs=[pl.BlockSpec((n_s, d_a), lambda i: (rb, 0)),
                  pl.BlockSpec((n_s, d_a), lambda i: (rb, 1)),
                  pl.BlockSpec((n_s, d_a), lambda i: (rb, 2)),
                  pl.BlockSpec((n_s, LORA_PAD), lambda i: (rb, zb)),
                  full((n_s, d_a)), full((n_s, d_a)), full((n_s, d_a)), full((n_s, LORA_PAD)),
                  full((1, d_a)), full((1, d_a)), full((1, d_a)), full((1, LORA_PAD))]
                 + [full((1, d_a))] * 5 + [full((LORA_PAD, d_a))] * 3
                 + [full((2 * LANES, LANES))],
        out_specs=[full((d_a, n_s))] * 6 + [full((n_s, d_a))] * 2,
        out_shape=[jax.ShapeDtypeStruct((d_a, n_s), F32)] * 6 + [tok] * 2,
        compiler_params=_cparams(("arbitrary",)),
        name="wkv_sample_prep",
    )(p, p, p, p, *s_shift, mu_rkv[0], mu_rkv[1], mu_rkv[2], mu_z,
      w0, a0, k_k, k_a, r_k, *lora, bsum)
    w, nkk, kka, k2, r, vt, bonus, gate = outs
    headspec = pl.BlockSpec((HEAD_DIM, n_s), lambda h: (h, 0))
    stspec = pl.BlockSpec((1, HEAD_DIM, HEAD_DIM, n_s), lambda h: (h, 0, 0, 0))
    s_new, yt = pl.pallas_call(
        _wkv_sample_step_kernel,
        grid=(n_heads,),
        in_specs=[stspec] + [headspec] * 6,
        out_specs=[stspec, headspec],
        out_shape=[jax.ShapeDtypeStruct(state_hijb.shape, F32), jax.ShapeDtypeStruct((d_a, n_s), F32)],
        compiler_params=_cparams(("arbitrary",)),
        name="wkv_sample_step",
    )(state_hijb, w, nkk, kka, k2, r, vt)
    ya = pl.pallas_call(
        _wkv_sample_finish_kernel,
        grid=(1,),
        in_specs=[full((d_a, n_s)), full((n_s, d_a)), full((n_s, d_a)), full((1, d_a)), full((1, d_a)),
                  full((2 * LANES, LANES))],
        out_specs=full((n_s, d_a)),
        out_shape=jax.ShapeDtypeStruct((n_s, d_a), BF16),
        compiler_params=_cparams(("arbitrary",)),
        name="wkv_sample_finish",
    )(yt, bonus, gate, ln_g, ln_b, bsum)
    return ya, s_new


def _pool_prompt_kernel(u_ref, w_ref, sc_ref, o_ref, carry_scr, *, tt, gw):
    t = pl.program_id(1)

    @pl.when(t == 0)
    def _():
        carry_scr[...] = jnp.zeros_like(carry_scr)

    u = u_ref[...]
    pos = t * tt + lax.broadcasted_iota(jnp.int32, (tt, gw), 0)
    for gi, win in enumerate(POOL_WINDOWS):
        cols = slice(gi * gw, (gi + 1) * gw)
        ug = u[:, cols]
        cur = jnp.concatenate([carry_scr[:, cols], ug], axis=0)
        off = 0
        step = 1
        while step < win:
            cur = cur[step:] + cur[:-step]
            off += step
            step *= 2
        wsum = cur[POOL_CARRY - off:POOL_CARRY - off + tt]
        cnt = jnp.minimum(pos + 1, win).astype(F32)
        pooled = wsum / cnt - ug
        y = _dot(pooled.astype(BF16), w_ref[gi].astype(BF16)) * sc_ref[:, cols]
        o_ref[:, cols] = y.astype(BF16)
    carry_scr[...] = u[tt - POOL_CARRY:, :]


def _pool_prompt(p, n_b, seq, pw, w_pool, pool_scale):
    tt = _tile(seq, 256)
    nt = seq // tt
    gw = pw // len(POOL_WINDOWS)
    return pl.pallas_call(
        functools.partial(_pool_prompt_kernel, tt=tt, gw=gw),
        grid=(n_b, nt),
        in_specs=[pl.BlockSpec((tt, pw), lambda b, t: (b * nt + t, 0)),
                  pl.BlockSpec(w_pool.shape, lambda b, t: (0, 0, 0)),
                  pl.BlockSpec((1, pw), lambda b, t: (0, 0))],
        out_specs=pl.BlockSpec((tt, pw), lambda b, t: (b * nt + t, 0)),
        out_shape=jax.ShapeDtypeStruct((n_b * seq, pw), BF16),
        scratch_shapes=[pltpu.VMEM((POOL_CARRY, pw), F32)],
        compiler_params=_cparams(("arbitrary", "arbitrary")),
        name="pool_prompt",
    )(p, w_pool, pool_scale.reshape(1, pw))


def _pool_sample_kernel(u_ref, hist_ref, w_ref, sc_ref, o_ref, *, gw):
    u = u_ref[...]
    for gi, win in enumerate(POOL_WINDOWS):
        cols = slice(gi * gw, (gi + 1) * gw)
        ug = u[:, cols]
        wsum = ug
        for dback in range(1, win):
            wsum = wsum + hist_ref[POOL_BUF - dback, :, cols]
        pooled = wsum / float(win) - ug
        y = _dot(pooled.astype(BF16), w_ref[gi].astype(BF16)) * sc_ref[:, cols]
        o_ref[:, cols] = y.astype(BF16)


def _pool_sample(p, n_p, n_s, pw, hist_t, w_pool, pool_scale):
    gw = pw // len(POOL_WINDOWS)
    full = lambda s: pl.BlockSpec(s, lambda i: (0,) * len(s))
    return pl.pallas_call(
        functools.partial(_pool_sample_kernel, gw=gw),
        grid=(1,),
        in_specs=[pl.BlockSpec((n_s, pw), lambda i: (n_p // n_s, 0)),
                  full(hist_t.shape), full(w_pool.shape), full((1, pw))],
        out_specs=full((n_s, pw)),
        out_shape=jax.ShapeDtypeStruct((n_s, pw), BF16),
        compiler_params=_cparams(("arbitrary",)),
        name="pool_sample",
    )(p, hist_t, w_pool, pool_scale.reshape(1, pw))


def _router_kernel(x_ref, g_ref, wr_ref, h_ref, eid_ref, wt_ref):
    x = x_ref[...]
    h = x * lax.rsqrt(jnp.mean(x * x, axis=-1, keepdims=True) + RMS_EPS) * g_ref[...]
    h_ref[...] = h
    logits = _dot(h.astype(BF16), wr_ref[...])
    lane = lax.broadcasted_iota(jnp.int32, logits.shape, 1)
    neg = jnp.float32(-jnp.inf)
    big = jnp.int32(1 << 20)
    is_g = lane < N_GROUPS
    lg = jnp.where(is_g, logits, neg)
    mg = jnp.max(lg, axis=1, keepdims=True)
    g_sel = jnp.min(jnp.where(is_g & (lg == mg), lane, big), axis=1, keepdims=True)
    p_sel = 1.0 / jnp.sum(jnp.where(is_g, jnp.exp(lg - mg), 0.0), axis=1, keepdims=True)
    e_lane = lane - N_GROUPS
    in_grp = (e_lane >= 0) & (e_lane < N_EXPERTS) & ((e_lane >> 3) == g_sel)
    le = jnp.where(in_grp, logits, neg)
    m1 = jnp.max(le, axis=1, keepdims=True)
    i1 = jnp.min(jnp.where(in_grp & (le == m1), lane, big), axis=1, keepdims=True)
    le2 = jnp.where(lane == i1, neg, le)
    m2 = jnp.max(le2, axis=1, keepdims=True)
    i2 = jnp.min(jnp.where(in_grp & (lane != i1) & (le2 == m2), lane, big), axis=1, keepdims=True)
    e2 = jnp.exp(m2 - m1)
    w1 = p_sel / (1.0 + e2)
    w2 = p_sel * e2 / (1.0 + e2)
    eid_ref[...] = jnp.where(lane == 0, i1 - N_GROUPS, jnp.where(lane == 1, i2 - N_GROUPS, 0))
    wt_ref[...] = jnp.where(lane == 0, w1, jnp.where(lane == 1, w2, 0.0))


def _router(x1, g, w_router):
    n, d = x1.shape
    tr = _tile(n, 256)
    return pl.pallas_call(
        _router_kernel,
        grid=(n // tr,),
        in_specs=[pl.BlockSpec((tr, d), lambda i: (i, 0)),
                  pl.BlockSpec((1, d), lambda i: (0, 0)),
                  pl.BlockSpec((d, LANES), lambda i: (0, 0))],
        out_specs=[pl.BlockSpec((tr, d), lambda i: (i, 0)),
                   pl.BlockSpec((tr, LANES), lambda i: (i, 0)),
                   pl.BlockSpec((tr, LANES), lambda i: (i, 0))],
        out_shape=[jax.ShapeDtypeStruct((n, d), F32),
                   jax.ShapeDtypeStruct((n, LANES), jnp.int32),
                   jax.ShapeDtypeStruct((n, LANES), F32)],
        compiler_params=_cparams(("arbitrary",)),
        name="router",
    )(x1, g.reshape(1, d), w_router)


def _moe_kernel(ge_ref, gs_ref, gn_ref, tok_ref, dst_ref,
                h_hbm, wg_ref, wu_ref, wd_ref, out_hbm,
                xf_scr, xb_scr, acc_scr, sem_g, sem_s, *, rows, n_ftiles):
    g = pl.program_id(0)
    f = pl.program_id(1)
    nrows = gn_ref[g]
    start = gs_ref[g]

    def gather_copy(tok, i):
        return pltpu.make_async_copy(h_hbm.at[pl.ds(tok, 1)], xf_scr.at[pl.ds(i, 1)], sem_g)

    def scatter_copy(i, dst):
        return pltpu.make_async_copy(acc_scr.at[pl.ds(i, 1)], out_hbm.at[pl.ds(dst, 1)], sem_s)

    @pl.when(nrows > 0)
    def _group():
        def issue_gather(first):
            def issue(i, carry):
                gather_copy(tok_ref[first + i], i).start()
                return carry
            lax.fori_loop(0, rows, issue, 0, unroll=8)

        @pl.when(f == 0)
        def _rows_in():
            @pl.when(g == 0)
            def _():
                issue_gather(start)
            pltpu.make_async_copy(h_hbm.at[pl.ds(0, rows)], xf_scr, sem_g).wait()
            xb_scr[...] = xf_scr[...].astype(BF16)

        @pl.when((f == 1) & (gn_ref[g + 1] > 0))
        def _prefetch():
            issue_gather(gs_ref[g + 1])

        def ffn_slice(first):
            x = xb_scr[...]
            hg = _dot(x, wg_ref[0].astype(BF16))
            hu = _dot(x, wu_ref[0].astype(BF16))
            act = (hg * _sigmoid(hg) * hu).astype(BF16)
            part = _dot(act, wd_ref[0].astype(BF16))
            if first:
                acc_scr[...] = part
            else:
                acc_scr[...] += part

        @pl.when(f == 0)
        def _():
            ffn_slice(True)

        @pl.when(f > 0)
        def _():
            ffn_slice(False)

        @pl.when(f == n_ftiles - 1)
        def _scatter():
            nblk = nrows // SCATTER_BLOCK

            def issue_block(b, carry):
                for j in range(SCATTER_BLOCK):
                    i = b * SCATTER_BLOCK + j
                    scatter_copy(i, dst_ref[start + i]).start(priority=j % 2)
                return carry
            lax.fori_loop(0, nblk, issue_block, 0)

            def issue_one(i, carry):
                scatter_copy(i, dst_ref[start + i]).start()
                return carry
            lax.fori_loop(nblk * SCATTER_BLOCK, nrows, issue_one, 0)

            def wait_block(b, carry):
                pltpu.make_async_copy(acc_scr.at[pl.ds(0, SCATTER_BLOCK)], out_hbm.at[pl.ds(0, SCATTER_BLOCK)],
                                      sem_s).wait()
                return carry
            lax.fori_loop(0, nblk, wait_block, 0)

            def wait_one(i, carry):
                scatter_copy(0, 0).wait()
                return carry
            lax.fori_loop(nblk * SCATTER_BLOCK, nrows, wait_one, 0)


def _moe(h2, eid, n_p, w_gate, w_up, w_down, rows, tf):
    n, d = h2.shape
    n_s = n - n_p
    d_e = w_gate.shape[2]
    n_assign = n * TOP_K
    n_ftiles = d_e // tf
    assert n_ftiles >= 2
    max_groups = -(-n_assign // rows) + N_EXPERTS
    eflat = eid.reshape(-1)
    order = jnp.argsort(eflat, stable=True).astype(jnp.int32)
    tok_sorted = order // TOP_K
    slot_sorted = order % TOP_K
    dst_sorted = jnp.where(tok_sorted < n_p, slot_sorted * n_p + tok_sorted,
                           TOP_K * n_p + slot_sorted * n_s + (tok_sorted - n_p))
    experts = jnp.arange(N_EXPERTS, dtype=jnp.int32)
    counts = jnp.sum((eflat[:, None] == experts[None, :]).astype(jnp.int32), axis=0)
    starts = jnp.cumsum(counts) - counts
    groups_per_e = (counts + rows - 1) // rows
    g_ends = jnp.cumsum(groups_per_e)
    gidx = jnp.arange(max_groups, dtype=jnp.int32)
    n_groups = g_ends[-1]
    g_e = jnp.minimum(jnp.sum((g_ends[None, :] <= gidx[:, None]).astype(jnp.int32), axis=1), N_EXPERTS - 1)
    local = gidx - (g_ends[g_e] - groups_per_e[g_e])
    g_start = starts[g_e] + local * rows
    g_n = jnp.clip(counts[g_e] - local * rows, 0, rows)
    valid = gidx < n_groups
    last_e = g_e[jnp.maximum(n_groups - 1, 0)]
    g_e = jnp.where(valid, g_e, last_e).astype(jnp.int32)
    one = jnp.zeros((1,), jnp.int32)
    g_start = jnp.concatenate([jnp.where(valid, g_start, 0).astype(jnp.int32), one])
    g_n = jnp.concatenate([jnp.where(valid, g_n, 0).astype(jnp.int32), one])
    pad = jnp.zeros((rows,), jnp.int32)
    tok_sorted = jnp.concatenate([tok_sorted, pad])
    dst_sorted = jnp.concatenate([dst_sorted, pad])

    def fsel(g, f, gn):
        return jnp.where(gn[g] > 0, f, n_ftiles - 1)

    grid_spec = pltpu.PrefetchScalarGridSpec(
        num_scalar_prefetch=5,
        grid=(max_groups, n_ftiles),
        in_specs=[pl.BlockSpec(memory_space=pl.ANY),
                  pl.BlockSpec((1, d, tf), lambda g, f, ge, gs, gn, tk, ds: (ge[g], 0, fsel(g, f, gn))),
                  pl.BlockSpec((1, d, tf), lambda g, f, ge, gs, gn, tk, ds: (ge[g], 0, fsel(g, f, gn))),
                  pl.BlockSpec((1, tf, d), lambda g, f, ge, gs, gn, tk, ds: (ge[g], fsel(g, f, gn), 0))],
        out_specs=pl.BlockSpec(memory_space=pl.ANY),
        scratch_shapes=[pltpu.VMEM((rows, d), F32), pltpu.VMEM((rows, d), BF16), pltpu.VMEM((rows, d), F32),
                        pltpu.SemaphoreType.DMA(()), pltpu.SemaphoreType.DMA(())],
    )
    return pl.pallas_call(
        functools.partial(_moe_kernel, rows=rows, n_ftiles=n_ftiles),
        grid_spec=grid_spec,
        out_shape=jax.ShapeDtypeStruct((TOP_K * n, d), F32),
        compiler_params=_cparams(("arbitrary", "arbitrary")),
        name="moe_experts",
    )(g_e, g_start, g_n, tok_sorted, dst_sorted, h2, w_gate, w_up, w_down)


def _combine_kernel(x_ref, y0_ref, y1_ref, wt_ref, g_ref, o_ref):
    wt = wt_ref[...]
    x = x_ref[...] + (y0_ref[...] * wt[:, 0:1] + y1_ref[...] * wt[:, 1:2])
    o_ref[...] = x * lax.rsqrt(jnp.mean(x * x, axis=-1, keepdims=True) + RMS_EPS) * g_ref[...]


def _combine(x1, y2, wt, g, row0, rows, y_row0):
    d = x1.shape[1]
    tr = _tile(rows, 256)
    assert row0 % tr == 0 and y_row0 % tr == 0
    x0, y0, y1 = row0 // tr, y_row0 // tr, (y_row0 + rows) // tr
    return pl.pallas_call(
        _combine_kernel,
        grid=(rows // tr,),
        in_specs=[pl.BlockSpec((tr, d), lambda i: (x0 + i, 0)),
                  pl.BlockSpec((tr, d), lambda i: (y0 + i, 0)),
                  pl.BlockSpec((tr, d), lambda i: (y1 + i, 0)),
                  pl.BlockSpec((tr, LANES), lambda i: (x0 + i, 0)),
                  pl.BlockSpec((1, d), lambda i: (0, 0))],
        out_specs=pl.BlockSpec((tr, d), lambda i: (i, 0)),
        out_shape=jax.ShapeDtypeStruct((rows, d), F32),
        compiler_params=_cparams(("arbitrary",)),
        name="combine",
    )(x1, y2, y2, wt, g.reshape(1, d))


def _layer(xp, xs, st_wkv, st_shift, st_pool, norm_out_g, norm1_g, w_in, mu_shift, w0, w_decay_up, a0,
           w_iclr_up, w_gate_up, k_k, k_a, r_k, ln_x_g, ln_x_b, w_pool, pool_scale,
           w_branch_a, w_branch_b, w_out, norm2_g, w_router_group, w_router_expert,
           w_exp_gate, w_exp_up, w_exp_down):
    n_b, seq, d = xp.shape
    n_sb = xs.shape[0]
    n_p, n_s = n_b * seq, n_sb * xs.shape[1]
    n = n_p + n_s
    d_a = w_branch_a.shape[0]
    pw = w_branch_b.shape[0]
    sw = 3 * d_a + LORA_W
    assert d_a == pw and d == 2 * d_a and xs.shape[1] == 1 and w_in.shape[1] == sw + pw + 2 * d

    zpad = LORA_PAD - LORA_W
    w_in_t = w_in.T
    head_w = 3 * d_a + LORA_PAD
    pieces = lambda t: ([t[..., i * d_a:(i + 1) * d_a] for i in range(3)],
                        jnp.pad(t[..., 3 * d_a:sw], [(0, 0)] * (t.ndim - 1) + [(0, zpad)]))
    mu_rkv, mu_z = pieces(mu_shift.reshape(1, sw))
    ss_rkv, ss_z = pieces(st_shift)
    vec = lambda t: t.reshape(1, d_a)
    prm = (vec(w0), vec(a0), vec(k_k), vec(k_a), vec(r_k), vec(ln_x_g), vec(ln_x_b))
    lpad = lambda w, r0: jnp.zeros((LORA_PAD, d_a), F32).at[r0:r0 + w.shape[0]].set(w).astype(BF16)
    lora = (lpad(w_decay_up, 0), lpad(w_iclr_up, DECAY_RANK), lpad(w_gate_up, DECAY_RANK + ICLR_RANK))
    lane_head = np.arange(LANES) // HEAD_DIM
    bsum = jnp.asarray(np.tile((lane_head[:, None] == lane_head[None, :]).astype(np.float32), (2, 1)), BF16)

    tm = _tile(n, 1664, 8)
    xp2, xs2 = xp.reshape(n_p, d), xs.reshape(n_s, d)
    h = _norm1(xp2, xs2, norm1_g)
    tn = _tile(d, 256, LANES)
    p_head = _matmul_nt(h, w_in_t, 0, head_w, tm, tn, "in_proj_head")
    p_tail = _matmul_nt(h, w_in_t, sw, pw + 2 * d, tm, tn, "in_proj_tail")

    ya_p, new_wkv_p = _wkv_prompt(p_head, n_b, seq, d_a, mu_rkv, mu_z, prm, lora, bsum)
    ya_s, new_wkv_s = _wkv_sample(p_head, n_p, n_s, d_a, (*ss_rkv, ss_z), jnp.transpose(st_wkv, (1, 2, 3, 0)),
                                  mu_rkv, mu_z, prm, lora, bsum)
    yb_p = _pool_prompt(p_tail, n_b, seq, pw, w_pool, pool_scale)
    yb_s = _pool_sample(p_tail, n_p, n_s, pw, jnp.swapaxes(st_pool, 0, 1), w_pool, pool_scale)
    ya = jnp.concatenate([ya_p, ya_s], axis=0)
    yb = jnp.concatenate([yb_p, yb_s], axis=0)

    merged = _merge(ya, yb, w_branch_a, w_branch_b, p_tail, pw, d, tm, tn)
    x1 = _outproj(merged, w_out, xp2, xs2, tm, tn)

    w_router = jnp.concatenate([w_router_group, w_router_expert,
                                jnp.zeros((d, LANES - N_GROUPS - N_EXPERTS), F32)], axis=1).astype(BF16)
    h2, eid, wt = _router(x1, norm2_g, w_router)
    d_e = w_exp_gate.shape[2]
    y2 = _moe(h2, eid[:, :TOP_K], n_p, w_exp_gate, w_exp_up, w_exp_down,
              rows=MOE_GROUP_ROWS if n * TOP_K >= 4096 else 64, tf=_tile(d_e, min(256, d_e // 2), LANES))
    y_p = _combine(x1, y2, wt, norm_out_g, 0, n_p, 0)
    y_s = _combine(x1, y2, wt, norm_out_g, n_p, n_s, TOP_K * n_p)

    new_wkv_s = jnp.transpose(new_wkv_s, (3, 0, 1, 2))
    new_shift_p = jnp.stack([p_head[b * seq + seq - 1, :sw] for b in range(n_b)])
    new_shift_s = p_head[n_p:, :sw]
    new_pool_p = jnp.stack([p_tail[b * seq + seq - POOL_BUF:(b + 1) * seq, :pw] for b in range(n_b)])
    new_pool_s = jnp.concatenate([st_pool[:, 1:], p_tail[n_p:, :pw][:, None, :]], axis=1)
    return (y_p.reshape(n_b, seq, d), y_s.reshape(n_sb, 1, d),
            new_wkv_p, new_shift_p, new_pool_p, new_wkv_s, new_shift_s, new_pool_s)


def kernel(x_prompt, x_sample, state_wkv, state_shift, state_pool, norm1_g, w_in, mu_shift, w0, w_decay_up, a0, w_iclr_up, w_gate_up, k_k, k_a, r_k, ln_x_g, ln_x_b, w_pool, pool_scale, w_branch_a, w_branch_b, w_out, norm2_g, w_router_group, w_router_expert, w_exp_gate, w_exp_up, w_exp_down, norm_f_g):
    assert norm1_g.shape[0] == 1, "single-layer trunk"
    outs = _layer(x_prompt, x_sample, state_wkv[0], state_shift[0], state_pool[0], norm_f_g,
                  norm1_g[0], w_in[0], mu_shift[0], w0[0], w_decay_up[0], a0[0], w_iclr_up[0], w_gate_up[0],
                  k_k[0], k_a[0], r_k[0], ln_x_g[0], ln_x_b[0], w_pool[0], pool_scale[0],
                  w_branch_a[0], w_branch_b[0], w_out[0], norm2_g[0], w_router_group[0], w_router_expert[0],
                  w_exp_gate[0], w_exp_up[0], w_exp_down[0])
    y_p, y_s = outs[0], outs[1]
    return (y_p, y_s) + tuple(o[None] for o in outs[2:])
```

```python
import functools
import math

import jax
import jax.numpy as jnp
import numpy as np
from jax import lax
from jax.experimental import pallas as pl
from jax.experimental.pallas import tpu as pltpu

F32 = jnp.float32
BF16 = jnp.bfloat16

HEAD_DIM = 64
LANES = 128
DECAY_RANK = 96
ICLR_RANK = 96
GATE_RANK = 256
LORA_W = DECAY_RANK + ICLR_RANK + GATE_RANK
LORA_PAD = 512
POOL_WINDOWS = (2, 4, 8, 16)
POOL_BUF = 15
POOL_CARRY = 16
N_GROUPS = 4
EXPERTS_PER_GROUP = 8
N_EXPERTS = N_GROUPS * EXPERTS_PER_GROUP
TOP_K = 2
RMS_EPS = 1e-6
GN_EPS = 6.4e-4
L2_EPS = 1e-12
WKV_CHUNK = 64
SCATTER_BLOCK = 8
MOE_GROUP_ROWS = 576
WKV_PAIR_UNROLL = 8
PREP_PIECES_PER_ROUND = 1
VMEM_LIMIT = 56 * 1024 * 1024


def _cparams(sem, vmem=VMEM_LIMIT):
    return pltpu.CompilerParams(dimension_semantics=sem, vmem_limit_bytes=vmem)


def _tile(n, target, mult=8):
    best = None
    for t in range(mult, min(n, target) + 1, mult):
        if n % t == 0:
            best = t
    return best if best is not None else n


def _dot(a, b):
    return jnp.dot(a, b, preferred_element_type=F32)


def _sigmoid(x):
    return 0.5 * jnp.tanh(0.5 * x) + 0.5


def _norm1_kernel(xp_ref, xs_ref, g_ref, h_ref, *, n_prompt_tiles):
    i = pl.program_id(0)

    def body(x):
        xn = x * lax.rsqrt(jnp.mean(x * x, axis=-1, keepdims=True) + RMS_EPS)
        h_ref[...] = (xn * g_ref[...]).astype(BF16)

    @pl.when(i < n_prompt_tiles)
    def _():
        body(xp_ref[...])

    @pl.when(i >= n_prompt_tiles)
    def _():
        body(xs_ref[...])


def _norm1(xp, xs, g):
    n_p, d = xp.shape
    n_s = xs.shape[0]
    tr = _tile(n_s, 256)
    assert n_p % tr == 0
    npt, nst = n_p // tr, n_s // tr
    return pl.pallas_call(
        functools.partial(_norm1_kernel, n_prompt_tiles=npt),
        grid=(npt + nst,),
        in_specs=[pl.BlockSpec((tr, d), lambda i: (jnp.minimum(i, npt - 1), 0)),
                  pl.BlockSpec((tr, d), lambda i: (jnp.maximum(i - npt, 0), 0)),
                  pl.BlockSpec((1, d), lambda i: (0, 0))],
        out_specs=pl.BlockSpec((tr, d), lambda i: (i, 0)),
        out_shape=jax.ShapeDtypeStruct((n_p + n_s, d), BF16),
        compiler_params=_cparams(("arbitrary",)),
        name="norm1",
    )(xp, xs, g.reshape(1, d))


def _mm_nt_kernel(a_ref, wt_ref, o_ref):
    o_ref[...] = lax.dot_general(a_ref[...], wt_ref[...].astype(BF16), _NT, preferred_element_type=F32)


def _matmul_nt(a, wt, row0, n_out, tm, tn, name):
    m, k = a.shape
    assert row0 % 8 == 0
    return pl.pallas_call(
        _mm_nt_kernel,
        grid=(m // tm, n_out // tn),
        in_specs=[pl.BlockSpec((tm, k), lambda i, j: (i, 0)),
                  pl.BlockSpec((pl.Element(tn), pl.Element(k)), lambda i, j: (pl.multiple_of(row0 + j * tn, 8), 0))],
        out_specs=pl.BlockSpec((tm, tn), lambda i, j: (i, j)),
        out_shape=jax.ShapeDtypeStruct((m, n_out), F32),
        compiler_params=_cparams(("arbitrary", "arbitrary")),
        name=name,
    )(a, wt)


def _merge_kernel(yap_ref, yas_ref, ybp_ref, ybs_ref, wa_ref, wb_ref, ga_ref, gb_ref, o_ref,
                  *, n_prompt_tiles, n_prompt_tail):
    i = pl.program_id(0)
    wa = wa_ref[...].astype(BF16)
    wb = wb_ref[...].astype(BF16)

    def gated(ya, yb, rows):
        a = _dot(ya, wa)
        b = _dot(yb, wb)
        o_ref[rows, :] = (_sigmoid(ga_ref[rows, :]) * a + _sigmoid(gb_ref[rows, :]) * b).astype(BF16)

    @pl.when(i < n_prompt_tiles)
    def _():
        gated(yap_ref[...], ybp_ref[...], slice(None))

    @pl.when(i == n_prompt_tiles)
    def _():
        if n_prompt_tail:
            gated(yap_ref[:n_prompt_tail, :], ybp_ref[:n_prompt_tail, :], slice(0, n_prompt_tail))
        gated(yas_ref[...], ybs_ref[...], slice(n_prompt_tail, None))


def _merge(ya_p, ya_s, yb_p, yb_s, wa, wb, p, gate_col, d, tm, tn):
    n_p, k = ya_p.shape
    n_s = ya_s.shape[0]
    m = n_p + n_s
    full, tail = divmod(n_p, tm)
    assert tail + n_s == tm and tail % 16 == 0
    ga0 = gate_col // tn
    gb0 = (gate_col + d) // tn
    prompt_rows = pl.BlockSpec((tm, k), lambda i, j: (jnp.minimum(i, (n_p - 1) // tm), 0))
    sample_rows = pl.BlockSpec((n_s, k), lambda i, j: (0, 0))
    return pl.pallas_call(
        functools.partial(_merge_kernel, n_prompt_tiles=full, n_prompt_tail=tail),
        grid=(m // tm, d // tn),
        in_specs=[prompt_rows, sample_rows, prompt_rows, sample_rows,
                  pl.BlockSpec((k, tn), lambda i, j: (0, j)),
                  pl.BlockSpec((k, tn), lambda i, j: (0, j)),
                  pl.BlockSpec((tm, tn), lambda i, j: (i, ga0 + j)),
                  pl.BlockSpec((tm, tn), lambda i, j: (i, gb0 + j))],
        out_specs=pl.BlockSpec((tm, tn), lambda i, j: (i, j)),
        out_shape=jax.ShapeDtypeStruct((m, d), BF16),
        compiler_params=_cparams(("arbitrary", "arbitrary")),
        name="merge",
    )(ya_p, ya_s, yb_p, yb_s, wa, wb, p, p)


def _outproj_kernel(m_ref, w_ref, xp_ref, xs_ref, o_ref, *, n_prompt_tiles, n_prompt_tail):
    i = pl.program_id(0)
    y = _dot(m_ref[...], w_ref[...].astype(BF16))

    @pl.when(i < n_prompt_tiles)
    def _():
        o_ref[...] = xp_ref[...] + y

    @pl.when(i == n_prompt_tiles)
    def _():
        if n_prompt_tail:
            o_ref[:n_prompt_tail, :] = xp_ref[:n_prompt_tail, :] + y[:n_prompt_tail]
        o_ref[n_prompt_tail:, :] = xs_ref[...] + y[n_prompt_tail:]


def _outproj(merged, w, xp, xs, tm, tn):
    m, k = merged.shape
    n = w.shape[1]
    n_p, n_s = xp.shape[0], xs.shape[0]
    full, tail = divmod(n_p, tm)
    assert m == n_p + n_s and tail + n_s == tm and tail % 8 == 0
    return pl.pallas_call(
        functools.partial(_outproj_kernel, n_prompt_tiles=full, n_prompt_tail=tail),
        grid=(m // tm, n // tn),
        in_specs=[pl.BlockSpec((tm, k), lambda i, j: (i, 0)),
                  pl.BlockSpec((k, tn), lambda i, j: (0, j)),
                  pl.BlockSpec((tm, tn), lambda i, j: (jnp.minimum(i, (n_p - 1) // tm), j)),
                  pl.BlockSpec((n_s, tn), lambda i, j: (0, j))],
        out_specs=pl.BlockSpec((tm, tn), lambda i, j: (i, j)),
        out_shape=jax.ShapeDtypeStruct((m, n), F32),
        compiler_params=_cparams(("arbitrary", "arbitrary")),
        name="out_proj",
    )(merged, w, xp, xs)


def _split2(x):
    hi = x.astype(BF16)
    lo = (x - hi.astype(F32)).astype(BF16)
    return hi, lo


def _split3(x):
    hi = x.astype(BF16)
    r1 = x - hi.astype(F32)
    mid = r1.astype(BF16)
    lo = (r1 - mid.astype(F32)).astype(BF16)
    return hi, mid, lo


def _dot_bf16(a, b, dims=(((1,), (0,)), ((), ()))):
    return lax.dot_general(a.astype(BF16), b.astype(BF16), dims, preferred_element_type=F32)


_NT = (((1,), (1,)), ((), ()))
_TN = (((0,), (0,)), ((), ()))


def _seg_sum(x, bsum_ref):
    rows = x.shape[0]
    n_slabs = x.shape[1] // LANES
    xs = jnp.concatenate([x[:, q * LANES:(q + 1) * LANES] for q in range(n_slabs)], axis=0)
    hi, lo = _split2(xs)
    s = _dot(jnp.concatenate([hi, lo], axis=1), bsum_ref[...])
    return jnp.concatenate([s[q * rows:(q + 1) * rows] for q in range(n_slabs)], axis=1)


def _rwkv_prep(r, k, v, z, w0, a0, k_k, k_a, r_k, wd, wa, wg, bsum_ref):
    def lora_up(act, w_ref, first_row, n_rows):
        lo = first_row // LANES * LANES
        hi = -(-(first_row + n_rows) // LANES) * LANES
        return _dot(act(z[:, lo:hi]).astype(BF16), w_ref[lo:hi, :])

    lora_w = lora_up(jnp.tanh, wd, 0, DECAY_RANK)
    lora_a = lora_up(lambda t: t, wa, DECAY_RANK, ICLR_RANK)
    gate = lora_up(_sigmoid, wg, DECAY_RANK + ICLR_RANK, GATE_RANK)
    log_decay = -_sigmoid(w0 + lora_w) * math.exp(-0.5)
    a = _sigmoid(a0 + lora_a)
    kk = k * k_k
    kk = kk * jnp.minimum(lax.rsqrt(_seg_sum(kk * kk, bsum_ref)), 1.0 / L2_EPS)
    k2 = k * (1.0 + (a - 1.0) * k_a)
    bonus = _seg_sum(r * k2 * r_k, bsum_ref) * v
    return log_decay, a, gate, kk, k2, bonus


def _rwkv_finish(y, bonus, gate, ln_g, ln_b, bsum_ref):
    inv = 1.0 / HEAD_DIM
    mu = _seg_sum(y, bsum_ref) * inv
    yc = y - mu
    var = _seg_sum(yc * yc, bsum_ref) * inv
    yn = yc * lax.rsqrt(var + GN_EPS) * ln_g + ln_b
    return (yn + bonus) * gate


def _wkv_prompt_kernel(pr_ref, pk_ref, pv_ref, pz_ref, nr_ref, nk_ref, nv_ref, nz_ref,
                       mur_ref, muk_ref, muv_ref, muz_ref,
                       w0_ref, a0_ref, kk_ref, ka_ref, rk_ref, lng_ref, lnb_ref,
                       wd_ref, wa_ref, wg_ref, bsum_ref, tri_ref,
                       ya_ref, sout_ref,
                       s_scr, cr_scr, ck_scr, cv_scr, cz_scr,
                       at_scr, bt_scr, kt_scr, rt_scr, bh_scr, kh_scr, v_scr, gc_scr, bonus_scr, gate_scr, y_scr,
                       *, n_pairs, pair_unroll):
    c = pl.program_id(1)
    n_chunks = pl.num_programs(1)
    C = WKV_CHUNK
    slot = c % 2
    group = min(4, n_pairs)

    def shift(p_ref, carry, mu_ref, cols):
        p = p_ref[:, cols]
        prev = pltpu.roll(p, 1, axis=0)
        row = lax.broadcasted_iota(jnp.int32, p.shape, 0)
        prev = jnp.where(row == 0, carry[0:1, cols], prev)
        carry[0:1, cols] = p[C - 1:C, :]
        return p + (prev - p) * mu_ref[:, cols]

    def prepare(refs, dst):
        r_ref, k_ref, v_ref, z_ref = refs
        z = shift(z_ref, cz_scr, muz_ref, slice(None))

        def lora_up(act, w_ref, first_row, n_rows):
            lo = first_row // LANES * LANES
            hi = -(-(first_row + n_rows) // LANES) * LANES
            return _dot(act(z[:, lo:hi]).astype(BF16), w_ref[lo:hi, :])

        lora_w = lora_up(jnp.tanh, wd_ref, 0, DECAY_RANK)
        yield
        lora_a = lora_up(lambda t: t, wa_ref, DECAY_RANK, ICLR_RANK)
        yield
        gate_scr[dst] = lora_up(_sigmoid, wg_ref, DECAY_RANK + ICLR_RANK, GATE_RANK)
        yield
        for q0 in range(0, n_pairs, group):
            cols = slice(q0 * LANES, (q0 + group) * LANES)
            r = shift(r_ref, cr_scr, mur_ref, cols)
            k = shift(k_ref, ck_scr, muk_ref, cols)
            v = shift(v_ref, cv_scr, muv_ref, cols)
            yield
            lw = -_sigmoid(w0_ref[:, cols] + lora_w[:, cols]) * math.exp(-0.5)
            a = _sigmoid(a0_ref[:, cols] + lora_a[:, cols])
            kk = k * kk_ref[:, cols]
            yield
            kk = kk * jnp.minimum(lax.rsqrt(_seg_sum(kk * kk, bsum_ref)), 1.0 / L2_EPS)
            yield
            k2 = k * (1.0 + (a - 1.0) * ka_ref[:, cols])
            bonus_scr[dst, :, cols] = _seg_sum(r * k2 * rk_ref[:, cols], bsum_ref) * v
            yield
            cl = _dot(tri_ref[...], jnp.concatenate(_split3(lw), axis=0))
            yield
            cl_end = cl[C - 1:C, :]
            beta = kk * a
            e_neg = jnp.exp(-cl)
            e_hat = jnp.exp(cl_end - cl)
            outs = ((at_scr, -kk * jnp.exp(cl - lw)), (bt_scr, beta * e_neg), (kt_scr, k2 * e_neg),
                    (rt_scr, r * jnp.exp(cl)), (bh_scr, beta * e_hat), (kh_scr, k2 * e_hat), (v_scr, v),
                    (gc_scr, jnp.broadcast_to(jnp.exp(cl_end), (2 * C, cl.shape[1]))))
            for scr, x in outs:
                for q in range(group):
                    scr[dst, q0 + q] = x[:, q * LANES:(q + 1) * LANES]
                yield

    @pl.when(c == 0)
    def _first_chunk():
        s_scr[...] = jnp.zeros_like(s_scr)
        cr_scr[...] = jnp.zeros_like(cr_scr)
        ck_scr[...] = jnp.zeros_like(ck_scr)
        cv_scr[...] = jnp.zeros_like(cv_scr)
        cz_scr[...] = jnp.zeros_like(cz_scr)
        for _ in prepare((pr_ref, pk_ref, pv_ref, pz_ref), 0):
            pass

    lane = lax.broadcasted_iota(jnp.int32, (C, LANES), 1)
    first = lane < HEAD_DIM
    row2 = lax.broadcasted_iota(jnp.int32, (2 * C, 2 * C), 0)
    col2 = lax.broadcasted_iota(jnp.int32, (2 * C, 2 * C), 1)
    cbits = C.bit_length() - 1
    same = (row2 >> cbits) == (col2 >> cbits)
    tri_strict = same & ((row2 & (C - 1)) > (col2 & (C - 1)))
    tri_incl = same & ((row2 & (C - 1)) >= (col2 & (C - 1)))
    eye = row2 == col2

    def stack(x):
        return jnp.concatenate([jnp.where(first, x, 0.0), jnp.where(first, 0.0, x)], axis=0)

    def load_pair(p):
        return (at_scr[slot, p], bt_scr[slot, p], kt_scr[slot, p], rt_scr[slot, p], v_scr[slot, p],
                bh_scr[slot, p], kh_scr[slot, p], gc_scr[slot, p], s_scr[p])

    def compute_pair(vals):
        a_s, b_s, k_s, r_s, v_s, bh_s, kh_s = (stack(x) for x in vals[:7])
        gc, s_t = vals[7], vals[8]
        bk = jnp.concatenate([b_s, k_s], axis=0)
        ga = _dot_bf16(a_s, bk, _NT)
        yield
        gr = lax.dot_general(r_s.astype(BF16), bk.astype(BF16), _NT, preferred_element_type=F32)
        yield
        l_ba = jnp.where(tri_strict, ga[:, :2 * C], 0.0)
        l_ka = jnp.where(tri_strict, ga[:, 2 * C:], 0.0)
        m_br = jnp.where(tri_incl, gr[:, :2 * C], 0.0)
        m_kr = jnp.where(tri_incl, gr[:, 2 * C:], 0.0)
        lkv = _dot_bf16(l_ka, v_s)
        yield
        pw = _dot_bf16(l_ba, l_ba)
        yield
        t_inv = jnp.where(eye, 1.0, l_ba)
        steps = (C - 1).bit_length() - 1
        for step in range(1, steps + 1):
            if step < steps:
                both = _dot_bf16(pw, jnp.concatenate([pw, t_inv], axis=1))
                pw, t_inv = both[:, :2 * C], t_inv + both[:, 2 * C:]
            else:
                t_inv = t_inv + _dot_bf16(pw, t_inv)
            yield
        tx = _dot_bf16(t_inv, jnp.concatenate([a_s, lkv], axis=1))
        yield
        a_hat, u_hat = tx[:, :LANES], tx[:, LANES:]
        u = _dot_bf16(a_hat, s_t) + u_hat
        yield
        uv = jnp.concatenate([u, v_s], axis=0)
        ys = _dot(jnp.concatenate([r_s, m_br, m_kr], axis=1).astype(BF16),
                  jnp.concatenate([s_t, uv], axis=0).astype(BF16))
        yield
        s_new = s_t * gc.T + _dot_bf16(jnp.concatenate([bh_s, kh_s], axis=0), uv, _TN)
        return ys[:C] + ys[C:], s_new

    ahead = prepare((nr_ref, nk_ref, nv_ref, nz_ref), 1 - slot)
    for first_pair in range(0, n_pairs, pair_unroll):
        pairs = range(first_pair, first_pair + pair_unroll)
        gens = [compute_pair(load_pair(p)) for p in pairs]
        results = {}
        while len(results) < len(gens):
            for j, gen in enumerate(gens):
                if j not in results:
                    try:
                        next(gen)
                    except StopIteration as done:
                        results[j] = done.value
            for _ in range(PREP_PIECES_PER_ROUND):
                next(ahead, None)
        for j, p in enumerate(pairs):
            y_scr[p], s_scr[p] = results[j]
    for _ in ahead:
        pass

    y = jnp.concatenate([y_scr[q] for q in range(n_pairs)], axis=1)
    out = _rwkv_finish(y, bonus_scr[slot], gate_scr[slot], lng_ref[...], lnb_ref[...], bsum_ref)
    ya_ref[...] = out.astype(BF16)

    @pl.when(c == n_chunks - 1)
    def _store_state():
        for q in range(n_pairs):
            t = s_scr[q].T
            sout_ref[0, 2 * q] = t[:HEAD_DIM, :HEAD_DIM]
            sout_ref[0, 2 * q + 1] = t[HEAD_DIM:, HEAD_DIM:]


def _wkv_prompt(p, n_b, seq, d_a, mu_rkv, mu_z, prm, lora, bsum):
    C = WKV_CHUNK
    n_chunks = seq // C
    n_pairs = d_a // LANES
    zb = (3 * d_a) // LORA_PAD
    row = lambda b, c: b * n_chunks + c
    nxt = lambda b, c: b * n_chunks + jnp.minimum(c + 1, n_chunks - 1)
    vec = lambda w: pl.BlockSpec((1, w), lambda b, c: (0, 0))
    full = lambda s: pl.BlockSpec(s, lambda b, c: (0,) * len(s))
    tri = jnp.asarray(np.tile(np.tril(np.ones((C, C), np.float32)), (1, 3)), BF16)

    def chunk_specs(rows):
        return [pl.BlockSpec((C, d_a), lambda b, c: (rows(b, c), 0)),
                pl.BlockSpec((C, d_a), lambda b, c: (rows(b, c), 1)),
                pl.BlockSpec((C, d_a), lambda b, c: (rows(b, c), 2)),
                pl.BlockSpec((C, LORA_PAD), lambda b, c: (rows(b, c), zb))]

    in_specs = chunk_specs(row) + chunk_specs(nxt) + [vec(d_a), vec(d_a), vec(d_a), vec(LORA_PAD)] + [
        vec(d_a)] * 7 + [full((LORA_PAD, d_a))] * 3 + [full((2 * LANES, LANES)), full((C, 3 * C))]
    scr = [pltpu.VMEM((n_pairs, LANES, LANES), F32),
           pltpu.VMEM((8, d_a), F32), pltpu.VMEM((8, d_a), F32), pltpu.VMEM((8, d_a), F32),
           pltpu.VMEM((8, LORA_PAD), F32)] + [pltpu.VMEM((2, n_pairs, C, LANES), F32)] * 7 + [
           pltpu.VMEM((2, n_pairs, 2 * C, LANES), F32), pltpu.VMEM((2, C, d_a), F32),
           pltpu.VMEM((2, C, d_a), F32), pltpu.VMEM((n_pairs, C, LANES), F32)]
    pair_unroll = math.gcd(n_pairs, WKV_PAIR_UNROLL)
    ya, s_out = pl.pallas_call(
        functools.partial(_wkv_prompt_kernel, n_pairs=n_pairs, pair_unroll=pair_unroll),
        grid=(n_b, n_chunks),
        in_specs=in_specs,
        out_specs=[pl.BlockSpec((C, d_a), lambda b, c: (row(b, c), 0)),
                   pl.BlockSpec((1, 2 * n_pairs, HEAD_DIM, HEAD_DIM), lambda b, c: (b, 0, 0, 0))],
        out_shape=[jax.ShapeDtypeStruct((n_b * seq, d_a), BF16),
                   jax.ShapeDtypeStruct((n_b, 2 * n_pairs, HEAD_DIM, HEAD_DIM), F32)],
        scratch_shapes=scr,
        compiler_params=_cparams(("arbitrary", "arbitrary")),
        name="wkv_prompt",
    )(p, p, p, p, p, p, p, p, mu_rkv[0], mu_rkv[1], mu_rkv[2], mu_z, *prm, *lora, bsum, tri)
    return ya, s_out


def _wkv_sample_prep_kernel(pr_ref, pk_ref, pv_ref, pz_ref, sr_ref, sk_ref, sv_ref, sz_ref,
                            mur_ref, muk_ref, muv_ref, muz_ref,
                            w0_ref, a0_ref, kk_ref, ka_ref, rk_ref,
                            wd_ref, wa_ref, wg_ref, bsum_ref,
                            w_out, nkk_out, kka_out, k2_out, r_out, vt_out, bonus_out, gate_out):
    def shift(p_ref, s_ref, mu_ref):
        p = p_ref[...]
        return p + (s_ref[...] - p) * mu_ref[...]

    r = shift(pr_ref, sr_ref, mur_ref)
    k = shift(pk_ref, sk_ref, muk_ref)
    v = shift(pv_ref, sv_ref, muv_ref)
    z = shift(pz_ref, sz_ref, muz_ref)
    lw, a, gate, kk, k2, bonus = _rwkv_prep(
        r, k, v, z, w0_ref[...], a0_ref[...], kk_ref[...], ka_ref[...], rk_ref[...],
        wd_ref, wa_ref, wg_ref, bsum_ref)
    w_out[...] = jnp.exp(lw).T
    nkk_out[...] = (-kk).T
    kka_out[...] = (kk * a).T
    k2_out[...] = k2.T
    r_out[...] = r.T
    vt_out[...] = v.T
    bonus_out[...] = bonus
    gate_out[...] = gate


def _wkv_sample_step_kernel(s_ref, w_ref, nkk_ref, kka_ref, k2_ref, r_ref, vt_ref, snew_ref, yt_ref):
    def body(i, carry):
        s = s_ref[0, i]
        sa = jnp.sum(s * nkk_ref[...], axis=0, keepdims=True)
        s_new = s * w_ref[...] + sa * kka_ref[...] + vt_ref[pl.ds(i, 1), :] * k2_ref[...]
        snew_ref[0, i] = s_new
        yt_ref[pl.ds(i, 1), :] = jnp.sum(s_new * r_ref[...], axis=0, keepdims=True)
        return carry

    lax.fori_loop(0, HEAD_DIM, body, 0, unroll=8)


def _wkv_sample_finish_kernel(yt_ref, bonus_ref, gate_ref, lng_ref, lnb_ref, bsum_ref, ya_ref):
    y = yt_ref[...].T
    out = _rwkv_finish(y, bonus_ref[...], gate_ref[...], lng_ref[...], lnb_ref[...], bsum_ref)
    ya_ref[...] = out.astype(BF16)


def _wkv_sample(p, n_p, n_s, d_a, s_shift, state_hijb, mu_rkv, mu_z, prm, lora, bsum):
    assert n_p % n_s == 0
    rb = n_p // n_s
    zb = (3 * d_a) // LORA_PAD
    n_heads = d_a // HEAD_DIM
    w0, a0, k_k, k_a, r_k, ln_g, ln_b = prm
    full = lambda s: pl.BlockSpec(s, lambda i: (0,) * len(s))
    tok = jax.ShapeDtypeStruct((n_s, d_a), F32)
    outs = pl.pallas_call(
        _wkv_sample_prep_kernel,
        grid=(1,),
        in_specs=[pl.BlockSpec((n_s, d_a), lambda i: (rb, 0)),
                  pl.BlockSpec((n_s, d_a), lambda i: (rb, 1)),
                  pl.BlockSpec((n_s, d_a), lambda i: (rb, 2)),
                  pl.BlockSpec((n_s, LORA_PAD), lambda i: (rb, zb)),
                  full((n_s, d_a)), full((n_s, d_a)), full((n_s, d_a)), full((n_s, LORA_PAD)),
                  full((1, d_a)), full((1, d_a)), full((1, d_a)), full((1, LORA_PAD))]
                 + [full((1, d_a))] * 5 + [full((LORA_PAD, d_a))] * 3
                 + [full((2 * LANES, LANES))],
        out_specs=[full((d_a, n_s))] * 6 + [full((n_s, d_a))] * 2,
        out_shape=[jax.ShapeDtypeStruct((d_a, n_s), F32)] * 6 + [tok] * 2,
        compiler_params=_cparams(("arbitrary",)),
        name="wkv_sample_prep",
    )(p, p, p, p, *s_shift, mu_rkv[0], mu_rkv[1], mu_rkv[2], mu_z,
      w0, a0, k_k, k_a, r_k, *lora, bsum)
    w, nkk, kka, k2, r, vt, bonus, gate = outs
    headspec = pl.BlockSpec((HEAD_DIM, n_s), lambda h: (h, 0))
    stspec = pl.BlockSpec((1, HEAD_DIM, HEAD_DIM, n_s), lambda h: (h, 0, 0, 0))
    s_new, yt = pl.pallas_call(
        _wkv_sample_step_kernel,
        grid=(n_heads,),
        in_specs=[stspec] + [headspec] * 6,
        out_specs=[stspec, headspec],
        out_shape=[jax.ShapeDtypeStruct(state_hijb.shape, F32), jax.ShapeDtypeStruct((d_a, n_s), F32)],
        compiler_params=_cparams(("arbitrary",)),
        name="wkv_sample_step",
    )(state_hijb, w, nkk, kka, k2, r, vt)
    ya = pl.pallas_call(
        _wkv_sample_finish_kernel,
        grid=(1,),
        in_specs=[full((d_a, n_s)), full((n_s, d_a)), full((n_s, d_a)), full((1, d_a)), full((1, d_a)),
                  full((2 * LANES, LANES))],
        out_specs=full((n_s, d_a)),
        out_shape=jax.ShapeDtypeStruct((n_s, d_a), BF16),
        compiler_params=_cparams(("arbitrary",)),
        name="wkv_sample_finish",
    )(yt, bonus, gate, ln_g, ln_b, bsum)
    return ya, s_new


def _pool_prompt_kernel(u_ref, w_ref, sc_ref, o_ref, carry_scr, *, tt, gw):
    t = pl.program_id(1)

    @pl.when(t == 0)
    def _():
        carry_scr[...] = jnp.zeros_like(carry_scr)

    u = u_ref[...]
    pos = t * tt + lax.broadcasted_iota(jnp.int32, (tt, gw), 0)
    for gi, win in enumerate(POOL_WINDOWS):
        cols = slice(gi * gw, (gi + 1) * gw)
        ug = u[:, cols]
        cur = jnp.concatenate([carry_scr[:, cols], ug], axis=0)
        off = 0
        step = 1
        while step < win:
            cur = cur[step:] + cur[:-step]
            off += step
            step *= 2
        wsum = cur[POOL_CARRY - off:POOL_CARRY - off + tt]
        cnt = jnp.minimum(pos + 1, win).astype(F32)
        pooled = wsum / cnt - ug
        y = _dot(pooled.astype(BF16), w_ref[gi].astype(BF16)) * sc_ref[:, cols]
        o_ref[:, cols] = y.astype(BF16)
    carry_scr[...] = u[tt - POOL_CARRY:, :]


def _pool_prompt(p, n_b, seq, pw, w_pool, pool_scale):
    tt = _tile(seq, 256)
    nt = seq // tt
    gw = pw // len(POOL_WINDOWS)
    return pl.pallas_call(
        functools.partial(_pool_prompt_kernel, tt=tt, gw=gw),
        grid=(n_b, nt),
        in_specs=[pl.BlockSpec((tt, pw), lambda b, t: (b * nt + t, 0)),
                  pl.BlockSpec(w_pool.shape, lambda b, t: (0, 0, 0)),
                  pl.BlockSpec((1, pw), lambda b, t: (0, 0))],
        out_specs=pl.BlockSpec((tt, pw), lambda b, t: (b * nt + t, 0)),
        out_shape=jax.ShapeDtypeStruct((n_b * seq, pw), BF16),
        scratch_shapes=[pltpu.VMEM((POOL_CARRY, pw), F32)],
        compiler_params=_cparams(("arbitrary", "arbitrary")),
        name="pool_prompt",
    )(p, w_pool, pool_scale.reshape(1, pw))


def _pool_sample_kernel(u_ref, hist_ref, w_ref, sc_ref, o_ref, *, gw):
    u = u_ref[...]
    for gi, win in enumerate(POOL_WINDOWS):
        cols = slice(gi * gw, (gi + 1) * gw)
        ug = u[:, cols]
        wsum = ug
        for dback in range(1, win):
            wsum = wsum + hist_ref[POOL_BUF - dback, :, cols]
        pooled = wsum / float(win) - ug
        y = _dot(pooled.astype(BF16), w_ref[gi].astype(BF16)) * sc_ref[:, cols]
        o_ref[:, cols] = y.astype(BF16)


def _pool_sample(p, n_p, n_s, pw, hist_t, w_pool, pool_scale):
    gw = pw // len(POOL_WINDOWS)
    full = lambda s: pl.BlockSpec(s, lambda i: (0,) * len(s))
    return pl.pallas_call(
        functools.partial(_pool_sample_kernel, gw=gw),
        grid=(1,),
        in_specs=[pl.BlockSpec((n_s, pw), lambda i: (n_p // n_s, 0)),
                  full(hist_t.shape), full(w_pool.shape), full((1, pw))],
        out_specs=full((n_s, pw)),
        out_shape=jax.ShapeDtypeStruct((n_s, pw), BF16),
        compiler_params=_cparams(("arbitrary",)),
        name="pool_sample",
    )(p, hist_t, w_pool, pool_scale.reshape(1, pw))


def _router_kernel(x_ref, g_ref, wr_ref, h_ref, eid_ref, wt_ref):
    x = x_ref[...]
    h = x * lax.rsqrt(jnp.mean(x * x, axis=-1, keepdims=True) + RMS_EPS) * g_ref[...]
    h_ref[...] = h
    logits = _dot(h.astype(BF16), wr_ref[...])
    lane = lax.broadcasted_iota(jnp.int32, logits.shape, 1)
    neg = jnp.float32(-jnp.inf)
    big = jnp.int32(1 << 20)
    is_g = lane < N_GROUPS
    lg = jnp.where(is_g, logits, neg)
    mg = jnp.max(lg, axis=1, keepdims=True)
    g_sel = jnp.min(jnp.where(is_g & (lg == mg), lane, big), axis=1, keepdims=True)
    p_sel = 1.0 / jnp.sum(jnp.where(is_g, jnp.exp(lg - mg), 0.0), axis=1, keepdims=True)
    e_lane = lane - N_GROUPS
    in_grp = (e_lane >= 0) & (e_lane < N_EXPERTS) & ((e_lane >> 3) == g_sel)
    le = jnp.where(in_grp, logits, neg)
    m1 = jnp.max(le, axis=1, keepdims=True)
    i1 = jnp.min(jnp.where(in_grp & (le == m1), lane, big), axis=1, keepdims=True)
    le2 = jnp.where(lane == i1, neg, le)
    m2 = jnp.max(le2, axis=1, keepdims=True)
    i2 = jnp.min(jnp.where(in_grp & (lane != i1) & (le2 == m2), lane, big), axis=1, keepdims=True)
    e2 = jnp.exp(m2 - m1)
    w1 = p_sel / (1.0 + e2)
    w2 = p_sel * e2 / (1.0 + e2)
    eid_ref[...] = jnp.where(lane == 0, i1 - N_GROUPS, jnp.where(lane == 1, i2 - N_GROUPS, 0))
    wt_ref[...] = jnp.where(lane == 0, w1, jnp.where(lane == 1, w2, 0.0))


def _router(x1, g, w_router):
    n, d = x1.shape
    tr = _tile(n, 256)
    return pl.pallas_call(
        _router_kernel,
        grid=(n // tr,),
        in_specs=[pl.BlockSpec((tr, d), lambda i: (i, 0)),
                  pl.BlockSpec((1, d), lambda i: (0, 0)),
                  pl.BlockSpec((d, LANES), lambda i: (0, 0))],
        out_specs=[pl.BlockSpec((tr, d), lambda i: (i, 0)),
                   pl.BlockSpec((tr, LANES), lambda i: (i, 0)),
                   pl.BlockSpec((tr, LANES), lambda i: (i, 0))],
        out_shape=[jax.ShapeDtypeStruct((n, d), F32),
                   jax.ShapeDtypeStruct((n, LANES), jnp.int32),
                   jax.ShapeDtypeStruct((n, LANES), F32)],
        compiler_params=_cparams(("arbitrary",)),
        name="router",
    )(x1, g.reshape(1, d), w_router)


def _moe_kernel(ge_ref, gs_ref, gn_ref, tok_ref, dst_ref,
                h_hbm, wg_ref, wu_ref, wd_ref, out_hbm,
                xf_scr, xb_scr, acc_scr, sem_g, sem_s, *, rows, n_ftiles):
    g = pl.program_id(0)
    f = pl.program_id(1)
    nrows = gn_ref[g]
    start = gs_ref[g]

    def gather_copy(tok, i):
        return pltpu.make_async_copy(h_hbm.at[pl.ds(tok, 1)], xf_scr.at[pl.ds(i, 1)], sem_g)

    def scatter_copy(i, dst):
        return pltpu.make_async_copy(acc_scr.at[pl.ds(i, 1)], out_hbm.at[pl.ds(dst, 1)], sem_s)

    @pl.when(nrows > 0)
    def _group():
        def issue_gather(first):
            def issue(i, carry):
                gather_copy(tok_ref[first + i], i).start()
                return carry
            lax.fori_loop(0, rows, issue, 0, unroll=8)

        @pl.when(f == 0)
        def _rows_in():
            @pl.when(g == 0)
            def _():
                issue_gather(start)
            pltpu.make_async_copy(h_hbm.at[pl.ds(0, rows)], xf_scr, sem_g).wait()
            xb_scr[...] = xf_scr[...].astype(BF16)

        @pl.when((f == 1) & (gn_ref[g + 1] > 0))
        def _prefetch():
            issue_gather(gs_ref[g + 1])

        def ffn_slice(first):
            x = xb_scr[...]
            hg = _dot(x, wg_ref[0].astype(BF16))
            hu = _dot(x, wu_ref[0].astype(BF16))
            act = (hg * _sigmoid(hg) * hu).astype(BF16)
            part = _dot(act, wd_ref[0].astype(BF16))
            if first:
                acc_scr[...] = part
            else:
                acc_scr[...] += part

        @pl.when(f == 0)
        def _():
            ffn_slice(True)

        @pl.when(f > 0)
        def _():
            ffn_slice(False)

        @pl.when(f == n_ftiles - 1)
        def _scatter():
            nblk = nrows // SCATTER_BLOCK

            def issue_block(b, carry):
                for j in range(SCATTER_BLOCK):
                    i = b * SCATTER_BLOCK + j
                    scatter_copy(i, dst_ref[start + i]).start(priority=j % 2)
                return carry
            lax.fori_loop(0, nblk, issue_block, 0)

            def issue_one(i, carry):
                scatter_copy(i, dst_ref[start + i]).start()
                return carry
            lax.fori_loop(nblk * SCATTER_BLOCK, nrows, issue_one, 0)

            def wait_block(b, carry):
                pltpu.make_async_copy(acc_scr.at[pl.ds(0, SCATTER_BLOCK)], out_hbm.at[pl.ds(0, SCATTER_BLOCK)],
                                      sem_s).wait()
                return carry
            lax.fori_loop(0, nblk, wait_block, 0)

            def wait_one(i, carry):
                scatter_copy(0, 0).wait()
                return carry
            lax.fori_loop(nblk * SCATTER_BLOCK, nrows, wait_one, 0)


def _moe(h2, eid, n_p, w_gate, w_up, w_down, rows, tf):
    n, d = h2.shape
    n_s = n - n_p
    d_e = w_gate.shape[2]
    n_assign = n * TOP_K
    n_ftiles = d_e // tf
    assert n_ftiles >= 2
    max_groups = -(-n_assign // rows) + N_EXPERTS
    eflat = eid.reshape(-1)
    order = jnp.argsort(eflat, stable=True).astype(jnp.int32)
    tok_sorted = order // TOP_K
    slot_sorted = order % TOP_K
    dst_sorted = jnp.where(tok_sorted < n_p, slot_sorted * n_p + tok_sorted,
                           TOP_K * n_p + slot_sorted * n_s + (tok_sorted - n_p))
    experts = jnp.arange(N_EXPERTS, dtype=jnp.int32)
    counts = jnp.sum((eflat[:, None] == experts[None, :]).astype(jnp.int32), axis=0)
    starts = jnp.cumsum(counts) - counts
    groups_per_e = (counts + rows - 1) // rows
    g_ends = jnp.cumsum(groups_per_e)
    gidx = jnp.arange(max_groups, dtype=jnp.int32)
    n_groups = g_ends[-1]
    g_e = jnp.minimum(jnp.sum((g_ends[None, :] <= gidx[:, None]).astype(jnp.int32), axis=1), N_EXPERTS - 1)
    local = gidx - (g_ends[g_e] - groups_per_e[g_e])
    g_start = starts[g_e] + local * rows
    g_n = jnp.clip(counts[g_e] - local * rows, 0, rows)
    valid = gidx < n_groups
    last_e = g_e[jnp.maximum(n_groups - 1, 0)]
    g_e = jnp.where(valid, g_e, last_e).astype(jnp.int32)
    one = jnp.zeros((1,), jnp.int32)
    g_start = jnp.concatenate([jnp.where(valid, g_start, 0).astype(jnp.int32), one])
    g_n = jnp.concatenate([jnp.where(valid, g_n, 0).astype(jnp.int32), one])
    pad = jnp.zeros((rows,), jnp.int32)
    tok_sorted = jnp.concatenate([tok_sorted, pad])
    dst_sorted = jnp.concatenate([dst_sorted, pad])

    def fsel(g, f, gn):
        return jnp.where(gn[g] > 0, f, n_ftiles - 1)

    grid_spec = pltpu.PrefetchScalarGridSpec(
        num_scalar_prefetch=5,
        grid=(max_groups, n_ftiles),
        in_specs=[pl.BlockSpec(memory_space=pl.ANY),
                  pl.BlockSpec((1, d, tf), lambda g, f, ge, gs, gn, tk, ds: (ge[g], 0, fsel(g, f, gn))),
                  pl.BlockSpec((1, d, tf), lambda g, f, ge, gs, gn, tk, ds: (ge[g], 0, fsel(g, f, gn))),
                  pl.BlockSpec((1, tf, d), lambda g, f, ge, gs, gn, tk, ds: (ge[g], fsel(g, f, gn), 0))],
        out_specs=pl.BlockSpec(memory_space=pl.ANY),
        scratch_shapes=[pltpu.VMEM((rows, d), F32), pltpu.VMEM((rows, d), BF16), pltpu.VMEM((rows, d), F32),
                        pltpu.SemaphoreType.DMA(()), pltpu.SemaphoreType.DMA(())],
    )
    return pl.pallas_call(
        functools.partial(_moe_kernel, rows=rows, n_ftiles=n_ftiles),
        grid_spec=grid_spec,
        out_shape=jax.ShapeDtypeStruct((TOP_K * n, d), F32),
        compiler_params=_cparams(("arbitrary", "arbitrary")),
        name="moe_experts",
    )(g_e, g_start, g_n, tok_sorted, dst_sorted, h2, w_gate, w_up, w_down)


def _combine_kernel(x_ref, y0_ref, y1_ref, wt_ref, g_ref, o_ref):
    wt = wt_ref[...]
    x = x_ref[...] + (y0_ref[...] * wt[:, 0:1] + y1_ref[...] * wt[:, 1:2])
    o_ref[...] = x * lax.rsqrt(jnp.mean(x * x, axis=-1, keepdims=True) + RMS_EPS) * g_ref[...]


def _combine(x1, y2, wt, g, row0, rows, y_row0):
    d = x1.shape[1]
    tr = _tile(rows, 256)
    assert row0 % tr == 0 and y_row0 % tr == 0
    x0, y0, y1 = row0 // tr, y_row0 // tr, (y_row0 + rows) // tr
    return pl.pallas_call(
        _combine_kernel,
        grid=(rows // tr,),
        in_specs=[pl.BlockSpec((tr, d), lambda i: (x0 + i, 0)),
                  pl.BlockSpec((tr, d), lambda i: (y0 + i, 0)),
                  pl.BlockSpec((tr, d), lambda i: (y1 + i, 0)),
                  pl.BlockSpec((tr, LANES), lambda i: (x0 + i, 0)),
                  pl.BlockSpec((1, d), lambda i: (0, 0))],
        out_specs=pl.BlockSpec((tr, d), lambda i: (i, 0)),
        out_shape=jax.ShapeDtypeStruct((rows, d), F32),
        compiler_params=_cparams(("arbitrary",)),
        name="combine",
    )(x1, y2, y2, wt, g.reshape(1, d))


def _layer(xp, xs, st_wkv, st_shift, st_pool, norm_out_g, norm1_g, w_in, mu_shift, w0, w_decay_up, a0,
           w_iclr_up, w_gate_up, k_k, k_a, r_k, ln_x_g, ln_x_b, w_pool, pool_scale,
           w_branch_a, w_branch_b, w_out, norm2_g, w_router_group, w_router_expert,
           w_exp_gate, w_exp_up, w_exp_down):
    n_b, seq, d = xp.shape
    n_sb = xs.shape[0]
    n_p, n_s = n_b * seq, n_sb * xs.shape[1]
    n = n_p + n_s
    d_a = w_branch_a.shape[0]
    pw = w_branch_b.shape[0]
    sw = 3 * d_a + LORA_W
    assert d_a == pw and d == 2 * d_a and xs.shape[1] == 1 and w_in.shape[1] == sw + pw + 2 * d

    zpad = LORA_PAD - LORA_W
    w_in_t = w_in.T
    head_w = 3 * d_a + LORA_PAD
    pieces = lambda t: ([t[..., i * d_a:(i + 1) * d_a] for i in range(3)],
                        jnp.pad(t[..., 3 * d_a:sw], [(0, 0)] * (t.ndim - 1) + [(0, zpad)]))
    mu_rkv, mu_z = pieces(mu_shift.reshape(1, sw))
    ss_rkv, ss_z = pieces(st_shift)
    vec = lambda t: t.reshape(1, d_a)
    prm = (vec(w0), vec(a0), vec(k_k), vec(k_a), vec(r_k), vec(ln_x_g), vec(ln_x_b))
    lpad = lambda w, r0: jnp.zeros((LORA_PAD, d_a), F32).at[r0:r0 + w.shape[0]].set(w).astype(BF16)
    lora = (lpad(w_decay_up, 0), lpad(w_iclr_up, DECAY_RANK), lpad(w_gate_up, DECAY_RANK + ICLR_RANK))
    lane_head = np.arange(LANES) // HEAD_DIM
    bsum = jnp.asarray(np.tile((lane_head[:, None] == lane_head[None, :]).astype(np.float32), (2, 1)), BF16)

    tm = _tile(n, 1664, 8)
    xp2, xs2 = xp.reshape(n_p, d), xs.reshape(n_s, d)
    h = _norm1(xp2, xs2, norm1_g)
    tn = _tile(d, 256, LANES)
    p_head = _matmul_nt(h, w_in_t, 0, head_w, tm, tn, "in_proj_head")
    p_tail = _matmul_nt(h, w_in_t, sw, pw + 2 * d, tm, tn, "in_proj_tail")

    ya_p, new_wkv_p = _wkv_prompt(p_head, n_b, seq, d_a, mu_rkv, mu_z, prm, lora, bsum)
    ya_s, new_wkv_s = _wkv_sample(p_head, n_p, n_s, d_a, (*ss_rkv, ss_z), jnp.transpose(st_wkv, (1, 2, 3, 0)),
                                  mu_rkv, mu_z, prm, lora, bsum)
    yb_p = _pool_prompt(p_tail, n_b, seq, pw, w_pool, pool_scale)
    yb_s = _pool_sample(p_tail, n_p, n_s, pw, jnp.swapaxes(st_pool, 0, 1), w_pool, pool_scale)

    merged = _merge(ya_p, ya_s, yb_p, yb_s, w_branch_a, w_branch_b, p_tail, pw, d, tm, tn)
    x1 = _outproj(merged, w_out, xp2, xs2, tm, tn)

    w_router = jnp.concatenate([w_router_group, w_router_expert,
                                jnp.zeros((d, LANES - N_GROUPS - N_EXPERTS), F32)], axis=1).astype(BF16)
    h2, eid, wt = _router(x1, norm2_g, w_router)
    d_e = w_exp_gate.shape[2]
    y2 = _moe(h2, eid[:, :TOP_K], n_p, w_exp_gate, w_exp_up, w_exp_down,
              rows=MOE_GROUP_ROWS if n * TOP_K >= 4096 else 64, tf=_tile(d_e, min(256, d_e // 2), LANES))
    y_p = _combine(x1, y2, wt, norm_out_g, 0, n_p, 0)
    y_s = _combine(x1, y2, wt, norm_out_g, n_p, n_s, TOP_K * n_p)

    new_wkv_s = jnp.transpose(new_wkv_s, (3, 0, 1, 2))
    new_shift_p = jnp.stack([p_head[b * seq + seq - 1, :sw] for b in range(n_b)])
    new_shift_s = p_head[n_p:, :sw]
    new_pool_p = jnp.stack([p_tail[b * seq + seq - POOL_BUF:(b + 1) * seq, :pw] for b in range(n_b)])
    new_pool_s = jnp.concatenate([st_pool[:, 1:], p_tail[n_p:, :pw][:, None, :]], axis=1)
    return (y_p.reshape(n_b, seq, d), y_s.reshape(n_sb, 1, d),
            new_wkv_p, new_shift_p, new_pool_p, new_wkv_s, new_shift_s, new_pool_s)


def kernel(x_prompt, x_sample, state_wkv, state_shift, state_pool, norm1_g, w_in, mu_shift, w0, w_decay_up, a0, w_iclr_up, w_gate_up, k_k, k_a, r_k, ln_x_g, ln_x_b, w_pool, pool_scale, w_branch_a, w_branch_b, w_out, norm2_g, w_router_group, w_router_expert, w_exp_gate, w_exp_up, w_exp_down, norm_f_g):
    assert norm1_g.shape[0] == 1, "single-layer trunk"
    outs = _layer(x_prompt, x_sample, state_wkv[0], state_shift[0], state_pool[0], norm_f_g,
                  norm1_g[0], w_in[0], mu_shift[0], w0[0], w_decay_up[0], a0[0], w_iclr_up[0], w_gate_up[0],
                  k_k[0], k_a[0], r_k[0], ln_x_g[0], ln_x_b[0], w_pool[0], pool_scale[0],
                  w_branch_a[0], w_branch_b[0], w_out[0], norm2_g[0], w_router_group[0], w_router_expert[0],
                  w_exp_gate[0], w_exp_up[0], w_exp_down[0])
    y_p, y_s = outs[0], outs[1]
    return (y_p, y_s) + tuple(o[None] for o in outs[2:])
```

```python
import functools
import math

import jax
import jax.numpy as jnp
import numpy as np
from jax import lax
from jax.experimental import pallas as pl
from jax.experimental.pallas import tpu as pltpu

F32 = jnp.float32
BF16 = jnp.bfloat16

HEAD_DIM = 64
LANES = 128
DECAY_RANK = 96
ICLR_RANK = 96
GATE_RANK = 256
LORA_W = DECAY_RANK + ICLR_RANK + GATE_RANK
LORA_PAD = 512
POOL_WINDOWS = (2, 4, 8, 16)
POOL_BUF = 15
POOL_CARRY = 16
N_GROUPS = 4
EXPERTS_PER_GROUP = 8
N_EXPERTS = N_GROUPS * EXPERTS_PER_GROUP
TOP_K = 2
RMS_EPS = 1e-6
GN_EPS = 6.4e-4
L2_EPS = 1e-12
WKV_CHUNK = 64
SCATTER_BLOCK = 8
MOE_GROUP_ROWS = 576
WKV_PAIR_UNROLL = 8
PREP_PIECES_PER_ROUND = 1
VMEM_LIMIT = 56 * 1024 * 1024


def _cparams(sem, vmem=VMEM_LIMIT):
    return pltpu.CompilerParams(dimension_semantics=sem, vmem_limit_bytes=vmem)


def _tile(n, target, mult=8):
    best = None
    for t in range(mult, min(n, target) + 1, mult):
        if n % t == 0:
            best = t
    return best if best is not None else n


def _dot(a, b):
    return jnp.dot(a, b, preferred_element_type=F32)


def _sigmoid(x):
    return 0.5 * jnp.tanh(0.5 * x) + 0.5


def _norm1_kernel(xp_ref, xs_ref, g_ref, h_ref, *, n_prompt_tiles):
    i = pl.program_id(0)

    def body(x):
        xn = x * lax.rsqrt(jnp.mean(x * x, axis=-1, keepdims=True) + RMS_EPS)
        h_ref[...] = (xn * g_ref[...]).astype(BF16)

    @pl.when(i < n_prompt_tiles)
    def _():
        body(xp_ref[...])

    @pl.when(i >= n_prompt_tiles)
    def _():
        body(xs_ref[...])


def _norm1(xp, xs, g):
    n_p, d = xp.shape
    n_s = xs.shape[0]
    tr = _tile(n_s, 256)
    assert n_p % tr == 0
    npt, nst = n_p // tr, n_s // tr
    return pl.pallas_call(
        functools.partial(_norm1_kernel, n_prompt_tiles=npt),
        grid=(npt + nst,),
        in_specs=[pl.BlockSpec((tr, d), lambda i: (jnp.minimum(i, npt - 1), 0)),
                  pl.BlockSpec((tr, d), lambda i: (jnp.maximum(i - npt, 0), 0)),
                  pl.BlockSpec((1, d), lambda i: (0, 0))],
        out_specs=pl.BlockSpec((tr, d), lambda i: (i, 0)),
        out_shape=jax.ShapeDtypeStruct((n_p + n_s, d), BF16),
        compiler_params=_cparams(("arbitrary",)),
        name="norm1",
    )(xp, xs, g.reshape(1, d))


def _mm_nt_kernel(a_ref, wt_ref, o_ref):
    o_ref[...] = lax.dot_general(a_ref[...], wt_ref[...].astype(BF16), _NT, preferred_element_type=F32)


def _matmul_nt(a, wt, row0, n_out, tm, tn, name):
    m, k = a.shape
    assert row0 % 8 == 0
    return pl.pallas_call(
        _mm_nt_kernel,
        grid=(m // tm, n_out // tn),
        in_specs=[pl.BlockSpec((tm, k), lambda i, j: (i, 0)),
                  pl.BlockSpec((pl.Element(tn), pl.Element(k)), lambda i, j: (pl.multiple_of(row0 + j * tn, 8), 0))],
        out_specs=pl.BlockSpec((tm, tn), lambda i, j: (i, j)),
        out_shape=jax.ShapeDtypeStruct((m, n_out), F32),
        compiler_params=_cparams(("arbitrary", "arbitrary")),
        name=name,
    )(a, wt)


def _merge_kernel(yap_ref, yas_ref, ybp_ref, ybs_ref, wa_ref, wb_ref, ga_ref, gb_ref, o_ref,
                  *, n_prompt_tiles, n_prompt_tail):
    i = pl.program_id(0)
    wa = wa_ref[...].astype(BF16)
    wb = wb_ref[...].astype(BF16)

    def gated(ya, yb, rows):
        a = _dot(ya, wa)
        b = _dot(yb, wb)
        o_ref[rows, :] = (_sigmoid(ga_ref[rows, :]) * a + _sigmoid(gb_ref[rows, :]) * b).astype(BF16)

    @pl.when(i < n_prompt_tiles)
    def _():
        gated(yap_ref[...], ybp_ref[...], slice(None))

    @pl.when(i == n_prompt_tiles)
    def _():
        if n_prompt_tail:
            gated(yap_ref[:n_prompt_tail, :], ybp_ref[:n_prompt_tail, :], slice(0, n_prompt_tail))
        gated(yas_ref[...], ybs_ref[...], slice(n_prompt_tail, None))


def _merge(ya_p, ya_s, yb_p, yb_s, wa, wb, p, gate_col, d, tm, tn):
    n_p, k = ya_p.shape
    n_s = ya_s.shape[0]
    m = n_p + n_s
    full, tail = divmod(n_p, tm)
    assert tail + n_s == tm and tail % 16 == 0
    ga0 = gate_col // tn
    gb0 = (gate_col + d) // tn
    prompt_rows = pl.BlockSpec((tm, k), lambda i, j: (jnp.minimum(i, (n_p - 1) // tm), 0))
    sample_rows = pl.BlockSpec((n_s, k), lambda i, j: (0, 0))
    return pl.pallas_call(
        functools.partial(_merge_kernel, n_prompt_tiles=full, n_prompt_tail=tail),
        grid=(m // tm, d // tn),
        in_specs=[prompt_rows, sample_rows, prompt_rows, sample_rows,
                  pl.BlockSpec((k, tn), lambda i, j: (0, j)),
                  pl.BlockSpec((k, tn), lambda i, j: (0, j)),
                  pl.BlockSpec((tm, tn), lambda i, j: (i, ga0 + j)),
                  pl.BlockSpec((tm, tn), lambda i, j: (i, gb0 + j))],
        out_specs=pl.BlockSpec((tm, tn), lambda i, j: (i, j)),
        out_shape=jax.ShapeDtypeStruct((m, d), BF16),
        compiler_params=_cparams(("arbitrary", "arbitrary")),
        name="merge",
    )(ya_p, ya_s, yb_p, yb_s, wa, wb, p, p)


def _outproj_kernel(m_ref, w_ref, xp_ref, xs_ref, o_ref, *, n_prompt_tiles, n_prompt_tail):
    i = pl.program_id(0)
    y = _dot(m_ref[...], w_ref[...].astype(BF16))

    @pl.when(i < n_prompt_tiles)
    def _():
        o_ref[...] = xp_ref[...] + y

    @pl.when(i == n_prompt_tiles)
    def _():
        if n_prompt_tail:
            o_ref[:n_prompt_tail, :] = xp_ref[:n_prompt_tail, :] + y[:n_prompt_tail]
        o_ref[n_prompt_tail:, :] = xs_ref[...] + y[n_prompt_tail:]


def _outproj(merged, w, xp, xs, tm, tn):
    m, k = merged.shape
    n = w.shape[1]
    n_p, n_s = xp.shape[0], xs.shape[0]
    full, tail = divmod(n_p, tm)
    assert m == n_p + n_s and tail + n_s == tm and tail % 8 == 0
    return pl.pallas_call(
        functools.partial(_outproj_kernel, n_prompt_tiles=full, n_prompt_tail=tail),
        grid=(m // tm, n // tn),
        in_specs=[pl.BlockSpec((tm, k), lambda i, j: (i, 0)),
                  pl.BlockSpec((k, tn), lambda i, j: (0, j)),
                  pl.BlockSpec((tm, tn), lambda i, j: (jnp.minimum(i, (n_p - 1) // tm), j)),
                  pl.BlockSpec((n_s, tn), lambda i, j: (0, j))],
        out_specs=pl.BlockSpec((tm, tn), lambda i, j: (i, j)),
        out_shape=jax.ShapeDtypeStruct((m, n), F32),
        compiler_params=_cparams(("arbitrary", "arbitrary")),
        name="out_proj",
    )(merged, w, xp, xs)


def _split2(x):
    hi = x.astype(BF16)
    lo = (x - hi.astype(F32)).astype(BF16)
    return hi, lo


def _split3(x):
    hi = x.astype(BF16)
    r1 = x - hi.astype(F32)
    mid = r1.astype(BF16)
    lo = (r1 - mid.astype(F32)).astype(BF16)
    return hi, mid, lo


def _dot_bf16(a, b, dims=(((1,), (0,)), ((), ()))):
    return lax.dot_general(a.astype(BF16), b.astype(BF16), dims, preferred_element_type=F32)


_NT = (((1,), (1,)), ((), ()))
_TN = (((0,), (0,)), ((), ()))


def _seg_sum(x, bsum_ref):
    rows = x.shape[0]
    n_slabs = x.shape[1] // LANES
    xs = jnp.concatenate([x[:, q * LANES:(q + 1) * LANES] for q in range(n_slabs)], axis=0)
    hi, lo = _split2(xs)
    s = _dot(jnp.concatenate([hi, lo], axis=1), bsum_ref[...])
    return jnp.concatenate([s[q * rows:(q + 1) * rows] for q in range(n_slabs)], axis=1)


def _rwkv_prep(r, k, v, z, w0, a0, k_k, k_a, r_k, wd, wa, wg, bsum_ref):
    def lora_up(act, w_ref, first_row, n_rows):
        lo = first_row // LANES * LANES
        hi = -(-(first_row + n_rows) // LANES) * LANES
        return _dot(act(z[:, lo:hi]).astype(BF16), w_ref[lo:hi, :])

    lora_w = lora_up(jnp.tanh, wd, 0, DECAY_RANK)
    lora_a = lora_up(lambda t: t, wa, DECAY_RANK, ICLR_RANK)
    gate = lora_up(_sigmoid, wg, DECAY_RANK + ICLR_RANK, GATE_RANK)
    log_decay = -_sigmoid(w0 + lora_w) * math.exp(-0.5)
    a = _sigmoid(a0 + lora_a)
    kk = k * k_k
    kk = kk * jnp.minimum(lax.rsqrt(_seg_sum(kk * kk, bsum_ref)), 1.0 / L2_EPS)
    k2 = k * (1.0 + (a - 1.0) * k_a)
    bonus = _seg_sum(r * k2 * r_k, bsum_ref) * v
    return log_decay, a, gate, kk, k2, bonus


def _rwkv_finish(y, bonus, gate, ln_g, ln_b, bsum_ref):
    inv = 1.0 / HEAD_DIM
    mu = _seg_sum(y, bsum_ref) * inv
    yc = y - mu
    var = _seg_sum(yc * yc, bsum_ref) * inv
    yn = yc * lax.rsqrt(var + GN_EPS) * ln_g + ln_b
    return (yn + bonus) * gate


def _wkv_prompt_kernel(pr_ref, pk_ref, pv_ref, pz_ref, nr_ref, nk_ref, nv_ref, nz_ref,
                       mur_ref, muk_ref, muv_ref, muz_ref,
                       w0_ref, a0_ref, kk_ref, ka_ref, rk_ref, lng_ref, lnb_ref,
                       wd_ref, wa_ref, wg_ref, bsum_ref, tri_ref,
                       ya_ref, sout_ref,
                       s_scr, cr_scr, ck_scr, cv_scr, cz_scr,
                       at_scr, bt_scr, kt_scr, rt_scr, bh_scr, kh_scr, v_scr, gc_scr, bonus_scr, gate_scr, y_scr,
                       *, n_pairs, pair_unroll):
    c = pl.program_id(1)
    n_chunks = pl.num_programs(1)
    C = WKV_CHUNK
    slot = c % 2
    group = min(4, n_pairs)

    def shift(p_ref, carry, mu_ref, cols):
        p = p_ref[:, cols]
        prev = pltpu.roll(p, 1, axis=0)
        row = lax.broadcasted_iota(jnp.int32, p.shape, 0)
        prev = jnp.where(row == 0, carry[0:1, cols], prev)
        carry[0:1, cols] = p[C - 1:C, :]
        return p + (prev - p) * mu_ref[:, cols]

    def prepare(refs, dst):
        r_ref, k_ref, v_ref, z_ref = refs
        z = shift(z_ref, cz_scr, muz_ref, slice(None))

        def lora_up(act, w_ref, first_row, n_rows):
            lo = first_row // LANES * LANES
            hi = -(-(first_row + n_rows) // LANES) * LANES
            return _dot(act(z[:, lo:hi]).astype(BF16), w_ref[lo:hi, :])

        lora_w = lora_up(jnp.tanh, wd_ref, 0, DECAY_RANK)
        yield
        lora_a = lora_up(lambda t: t, wa_ref, DECAY_RANK, ICLR_RANK)
        yield
        gate_scr[dst] = lora_up(_sigmoid, wg_ref, DECAY_RANK + ICLR_RANK, GATE_RANK)
        yield
        for q0 in range(0, n_pairs, group):
            cols = slice(q0 * LANES, (q0 + group) * LANES)
            r = shift(r_ref, cr_scr, mur_ref, cols)
            k = shift(k_ref, ck_scr, muk_ref, cols)
            v = shift(v_ref, cv_scr, muv_ref, cols)
            yield
            lw = -_sigmoid(w0_ref[:, cols] + lora_w[:, cols]) * math.exp(-0.5)
            a = _sigmoid(a0_ref[:, cols] + lora_a[:, cols])
            kk = k * kk_ref[:, cols]
            yield
            kk = kk * jnp.minimum(lax.rsqrt(_seg_sum(kk * kk, bsum_ref)), 1.0 / L2_EPS)
            yield
            k2 = k * (1.0 + (a - 1.0) * ka_ref[:, cols])
            bonus_scr[dst, :, cols] = _seg_sum(r * k2 * rk_ref[:, cols], bsum_ref) * v
            yield
            cl = _dot(tri_ref[...], jnp.concatenate(_split3(lw), axis=0))
            yield
            cl_end = cl[C - 1:C, :]
            beta = kk * a
            e_neg = jnp.exp(-cl)
            e_hat = jnp.exp(cl_end - cl)
            outs = ((at_scr, -kk * jnp.exp(cl - lw)), (bt_scr, beta * e_neg), (kt_scr, k2 * e_neg),
                    (rt_scr, r * jnp.exp(cl)), (bh_scr, beta * e_hat), (kh_scr, k2 * e_hat), (v_scr, v),
                    (gc_scr, jnp.broadcast_to(jnp.exp(cl_end), (2 * C, cl.shape[1]))))
            for scr, x in outs:
                for q in range(group):
                    scr[dst, q0 + q] = x[:, q * LANES:(q + 1) * LANES]
                yield

    @pl.when(c == 0)
    def _first_chunk():
        s_scr[...] = jnp.zeros_like(s_scr)
        cr_scr[...] = jnp.zeros_like(cr_scr)
        ck_scr[...] = jnp.zeros_like(ck_scr)
        cv_scr[...] = jnp.zeros_like(cv_scr)
        cz_scr[...] = jnp.zeros_like(cz_scr)
        for _ in prepare((pr_ref, pk_ref, pv_ref, pz_ref), 0):
            pass

    lane = lax.broadcasted_iota(jnp.int32, (C, LANES), 1)
    first = lane < HEAD_DIM
    row2 = lax.broadcasted_iota(jnp.int32, (2 * C, 2 * C), 0)
    col2 = lax.broadcasted_iota(jnp.int32, (2 * C, 2 * C), 1)
    cbits = C.bit_length() - 1
    same = (row2 >> cbits) == (col2 >> cbits)
    tri_strict = same & ((row2 & (C - 1)) > (col2 & (C - 1)))
    tri_incl = same & ((row2 & (C - 1)) >= (col2 & (C - 1)))
    eye = row2 == col2

    def stack(x):
        return jnp.concatenate([jnp.where(first, x, 0.0), jnp.where(first, 0.0, x)], axis=0)

    def load_pair(p):
        return (at_scr[slot, p], bt_scr[slot, p], kt_scr[slot, p], rt_scr[slot, p], v_scr[slot, p],
                bh_scr[slot, p], kh_scr[slot, p], gc_scr[slot, p], s_scr[p])

    def compute_pair(vals):
        a_s, b_s, k_s, r_s, v_s, bh_s, kh_s = (stack(x) for x in vals[:7])
        gc, s_t = vals[7], vals[8]
        bk = jnp.concatenate([b_s, k_s], axis=0)
        ga = _dot_bf16(a_s, bk, _NT)
        yield
        gr = lax.dot_general(r_s.astype(BF16), bk.astype(BF16), _NT, preferred_element_type=F32)
        yield
        l_ba = jnp.where(tri_strict, ga[:, :2 * C], 0.0)
        l_ka = jnp.where(tri_strict, ga[:, 2 * C:], 0.0)
        m_br = jnp.where(tri_incl, gr[:, :2 * C], 0.0)
        m_kr = jnp.where(tri_incl, gr[:, 2 * C:], 0.0)
        lkv = _dot_bf16(l_ka, v_s)
        yield
        pw = _dot_bf16(l_ba, l_ba)
        yield
        t_inv = jnp.where(eye, 1.0, l_ba)
        steps = (C - 1).bit_length() - 1
        for step in range(1, steps + 1):
            if step < steps:
                both = _dot_bf16(pw, jnp.concatenate([pw, t_inv], axis=1))
                pw, t_inv = both[:, :2 * C], t_inv + both[:, 2 * C:]
            else:
                t_inv = t_inv + _dot_bf16(pw, t_inv)
            yield
        tx = _dot_bf16(t_inv, jnp.concatenate([a_s, lkv], axis=1))
        yield
        a_hat, u_hat = tx[:, :LANES], tx[:, LANES:]
        u = _dot_bf16(a_hat, s_t) + u_hat
        yield
        uv = jnp.concatenate([u, v_s], axis=0)
        ys = _dot(jnp.concatenate([r_s, m_br, m_kr], axis=1).astype(BF16),
                  jnp.concatenate([s_t, uv], axis=0).astype(BF16))
        yield
        s_new = s_t * gc.T + _dot_bf16(jnp.concatenate([bh_s, kh_s], axis=0), uv, _TN)
        return ys[:C] + ys[C:], s_new

    ahead = prepare((nr_ref, nk_ref, nv_ref, nz_ref), 1 - slot)
    for first_pair in range(0, n_pairs, pair_unroll):
        pairs = range(first_pair, first_pair + pair_unroll)
        gens = [compute_pair(load_pair(p)) for p in pairs]
        results = {}
        while len(results) < len(gens):
            for j, gen in enumerate(gens):
                if j not in results:
                    try:
                        next(gen)
                    except StopIteration as done:
                        results[j] = done.value
            for _ in range(PREP_PIECES_PER_ROUND):
                next(ahead, None)
        for j, p in enumerate(pairs):
            y_scr[p], s_scr[p] = results[j]
    for _ in ahead:
        pass

    y = jnp.concatenate([y_scr[q] for q in range(n_pairs)], axis=1)
    out = _rwkv_finish(y, bonus_scr[slot], gate_scr[slot], lng_ref[...], lnb_ref[...], bsum_ref)
    ya_ref[...] = out.astype(BF16)

    @pl.when(c == n_chunks - 1)
    def _store_state():
        for q in range(n_pairs):
            t = s_scr[q].T
            sout_ref[0, 2 * q] = t[:HEAD_DIM, :HEAD_DIM]
            sout_ref[0, 2 * q + 1] = t[HEAD_DIM:, HEAD_DIM:]


def _wkv_prompt(p, n_b, seq, d_a, mu_rkv, mu_z, prm, lora, bsum):
    C = WKV_CHUNK
    n_chunks = seq // C
    n_pairs = d_a // LANES
    zb = (3 * d_a) // LORA_PAD
    row = lambda b, c: b * n_chunks + c
    nxt = lambda b, c: b * n_chunks + jnp.minimum(c + 1, n_chunks - 1)
    vec = lambda w: pl.BlockSpec((1, w), lambda b, c: (0, 0))
    full = lambda s: pl.BlockSpec(s, lambda b, c: (0,) * len(s))
    tri = jnp.asarray(np.tile(np.tril(np.ones((C, C), np.float32)), (1, 3)), BF16)

    def chunk_specs(rows):
        return [pl.BlockSpec((C, d_a), lambda b, c: (rows(b, c), 0)),
                pl.BlockSpec((C, d_a), lambda b, c: (rows(b, c), 1)),
                pl.BlockSpec((C, d_a), lambda b, c: (rows(b, c), 2)),
                pl.BlockSpec((C, LORA_PAD), lambda b, c: (rows(b, c), zb))]

    in_specs = chunk_specs(row) + chunk_specs(nxt) + [vec(d_a), vec(d_a), vec(d_a), vec(LORA_PAD)] + [
        vec(d_a)] * 7 + [full((LORA_PAD, d_a))] * 3 + [full((2 * LANES, LANES)), full((C, 3 * C))]
    scr = [pltpu.VMEM((n_pairs, LANES, LANES), F32),
           pltpu.VMEM((8, d_a), F32), pltpu.VMEM((8, d_a), F32), pltpu.VMEM((8, d_a), F32),
           pltpu.VMEM((8, LORA_PAD), F32)] + [pltpu.VMEM((2, n_pairs, C, LANES), F32)] * 7 + [
           pltpu.VMEM((2, n_pairs, 2 * C, LANES), F32), pltpu.VMEM((2, C, d_a), F32),
           pltpu.VMEM((2, C, d_a), F32), pltpu.VMEM((n_pairs, C, LANES), F32)]
    pair_unroll = math.gcd(n_pairs, WKV_PAIR_UNROLL)
    ya, s_out = pl.pallas_call(
        functools.partial(_wkv_prompt_kernel, n_pairs=n_pairs, pair_unroll=pair_unroll),
        grid=(n_b, n_chunks),
        in_specs=in_specs,
        out_specs=[pl.BlockSpec((C, d_a), lambda b, c: (row(b, c), 0)),
                   pl.BlockSpec((1, 2 * n_pairs, HEAD_DIM, HEAD_DIM), lambda b, c: (b, 0, 0, 0))],
        out_shape=[jax.ShapeDtypeStruct((n_b * seq, d_a), BF16),
                   jax.ShapeDtypeStruct((n_b, 2 * n_pairs, HEAD_DIM, HEAD_DIM), F32)],
        scratch_shapes=scr,
        compiler_params=_cparams(("arbitrary", "arbitrary")),
        name="wkv_prompt",
    )(p, p, p, p, p, p, p, p, mu_rkv[0], mu_rkv[1], mu_rkv[2], mu_z, *prm, *lora, bsum, tri)
    return ya, s_out


def _wkv_sample_prep_kernel(pr_ref, pk_ref, pv_ref, pz_ref, sr_ref, sk_ref, sv_ref, sz_ref,
                            mur_ref, muk_ref, muv_ref, muz_ref,
                            w0_ref, a0_ref, kk_ref, ka_ref, rk_ref,
                            wd_ref, wa_ref, wg_ref, bsum_ref,
                            w_out, nkk_out, kka_out, k2_out, r_out, vt_out, bonus_out, gate_out):
    def shift(p_ref, s_ref, mu_ref):
        p = p_ref[...]
        return p + (s_ref[...] - p) * mu_ref[...]

    r = shift(pr_ref, sr_ref, mur_ref)
    k = shift(pk_ref, sk_ref, muk_ref)
    v = shift(pv_ref, sv_ref, muv_ref)
    z = shift(pz_ref, sz_ref, muz_ref)
    lw, a, gate, kk, k2, bonus = _rwkv_prep(
        r, k, v, z, w0_ref[...], a0_ref[...], kk_ref[...], ka_ref[...], rk_ref[...],
        wd_ref, wa_ref, wg_ref, bsum_ref)
    w_out[...] = jnp.exp(lw).T
    nkk_out[...] = (-kk).T
    kka_out[...] = (kk * a).T
    k2_out[...] = k2.T
    r_out[...] = r.T
    vt_out[...] = v.T
    bonus_out[...] = bonus
    gate_out[...] = gate


def _wkv_sample_step_kernel(s_ref, w_ref, nkk_ref, kka_ref, k2_ref, r_ref, vt_ref, snew_ref, yt_ref):
    def body(i, carry):
        s = s_ref[0, i]
        sa = jnp.sum(s * nkk_ref[...], axis=0, keepdims=True)
        s_new = s * w_ref[...] + sa * kka_ref[...] + vt_ref[pl.ds(i, 1), :] * k2_ref[...]
        snew_ref[0, i] = s_new
        yt_ref[pl.ds(i, 1), :] = jnp.sum(s_new * r_ref[...], axis=0, keepdims=True)
        return carry

    lax.fori_loop(0, HEAD_DIM, body, 0, unroll=8)


def _wkv_sample_finish_kernel(yt_ref, bonus_ref, gate_ref, lng_ref, lnb_ref, bsum_ref, ya_ref):
    y = yt_ref[...].T
    out = _rwkv_finish(y, bonus_ref[...], gate_ref[...], lng_ref[...], lnb_ref[...], bsum_ref)
    ya_ref[...] = out.astype(BF16)


def _wkv_sample(p, n_p, n_s, d_a, s_shift, state_hijb, mu_rkv, mu_z, prm, lora, bsum):
    assert n_p % n_s == 0
    rb = n_p // n_s
    zb = (3 * d_a) // LORA_PAD
    n_heads = d_a // HEAD_DIM
    w0, a0, k_k, k_a, r_k, ln_g, ln_b = prm
    full = lambda s: pl.BlockSpec(s, lambda i: (0,) * len(s))
    tok = jax.ShapeDtypeStruct((n_s, d_a), F32)
    outs = pl.pallas_call(
        _wkv_sample_prep_kernel,
        grid=(1,),
        in_specs=[pl.BlockSpec((n_s, d_a), lambda i: (rb, 0)),
                  pl.BlockSpec((n_s, d_a), lambda i: (rb, 1)),
                  pl.BlockSpec((n_s, d_a), lambda i: (rb, 2)),
                  pl.BlockSpec((n_s, LORA_PAD), lambda i: (rb, zb)),
                  full((n_s, d_a)), full((n_s, d_a)), full((n_s, d_a)), full((n_s, LORA_PAD)),
                  full((1, d_a)), full((1, d_a)), full((1, d_a)), full((1, LORA_PAD))]
                 + [full((1, d_a))] * 5 + [full((LORA_PAD, d_a))] * 3
                 + [full((2 * LANES, LANES))],
        out_specs=[full((d_a, n_s))] * 6 + [full((n_s, d_a))] * 2,
        out_shape=[jax.ShapeDtypeStruct((d_a, n_s), F32)] * 6 + [tok] * 2,
        compiler_params=_cparams(("arbitrary",)),
        name="wkv_sample_prep",
    )(p, p, p, p, *s_shift, mu_rkv[0], mu_rkv[1], mu_rkv[2], mu_z,
      w0, a0, k_k, k_a, r_k, *lora, bsum)
    w, nkk, kka, k2, r, vt, bonus, gate = outs
    headspec = pl.BlockSpec((HEAD_DIM, n_s), lambda h: (h, 0))
    stspec = pl.BlockSpec((1, HEAD_DIM, HEAD_DIM, n_s), lambda h: (h, 0, 0, 0))
    s_new, yt = pl.pallas_call(
        _wkv_sample_step_kernel,
        grid=(n_heads,),
        in_specs=[stspec] + [headspec] * 6,
        out_specs=[stspec, headspec],
        out_shape=[jax.ShapeDtypeStruct(state_hijb.shape, F32), jax.ShapeDtypeStruct((d_a, n_s), F32)],
        compiler_params=_cparams(("arbitrary",)),
        name="wkv_sample_step",
    )(state_hijb, w, nkk, kka, k2, r, vt)
    ya = pl.pallas_call(
        _wkv_sample_finish_kernel,
        grid=(1,),
        in_specs=[full((d_a, n_s)), full((n_s, d_a)), full((n_s, d_a)), full((1, d_a)), full((1, d_a)),
                  full((2 * LANES, LANES))],
        out_specs=full((n_s, d_a)),
        out_shape=jax.ShapeDtypeStruct((n_s, d_a), BF16),
        compiler_params=_cparams(("arbitrary",)),
        name="wkv_sample_finish",
    )(yt, bonus, gate, ln_g, ln_b, bsum)
    return ya, s_new


def _pool_prompt_kernel(u_ref, w_ref, sc_ref, o_ref, carry_scr, *, tt, gw):
    t = pl.program_id(1)

    @pl.when(t == 0)
    def _():
        carry_scr[...] = jnp.zeros_like(carry_scr)

    u = u_ref[...]
    pos = t * tt + lax.broadcasted_iota(jnp.int32, (tt, gw), 0)
    for gi, win in enumerate(POOL_WINDOWS):
        cols = slice(gi * gw, (gi + 1) * gw)
        ug = u[:, cols]
        cur = jnp.concatenate([carry_scr[:, cols], ug], axis=0)
        off = 0
        step = 1
        while step < win:
            cur = cur[step:] + cur[:-step]
            off += step
            step *= 2
        wsum = cur[POOL_CARRY - off:POOL_CARRY - off + tt]
        cnt = jnp.minimum(pos + 1, win).astype(F32)
        pooled = wsum / cnt - ug
        y = _dot(pooled.astype(BF16), w_ref[gi].astype(BF16)) * sc_ref[:, cols]
        o_ref[:, cols] = y.astype(BF16)
    carry_scr[...] = u[tt - POOL_CARRY:, :]


def _pool_prompt(p, n_b, seq, pw, w_pool, pool_scale):
    tt = _tile(seq, 512)
    nt = seq // tt
    gw = pw // len(POOL_WINDOWS)
    return pl.pallas_call(
        functools.partial(_pool_prompt_kernel, tt=tt, gw=gw),
        grid=(n_b, nt),
        in_specs=[pl.BlockSpec((tt, pw), lambda b, t: (b * nt + t, 0)),
                  pl.BlockSpec(w_pool.shape, lambda b, t: (0, 0, 0)),
                  pl.BlockSpec((1, pw), lambda b, t: (0, 0))],
        out_specs=pl.BlockSpec((tt, pw), lambda b, t: (b * nt + t, 0)),
        out_shape=jax.ShapeDtypeStruct((n_b * seq, pw), BF16),
        scratch_shapes=[pltpu.VMEM((POOL_CARRY, pw), F32)],
        compiler_params=_cparams(("arbitrary", "arbitrary")),
        name="pool_prompt",
    )(p, w_pool, pool_scale.reshape(1, pw))


def _pool_sample_kernel(u_ref, hist_ref, w_ref, sc_ref, o_ref, *, gw):
    u = u_ref[...]
    for gi, win in enumerate(POOL_WINDOWS):
        cols = slice(gi * gw, (gi + 1) * gw)
        ug = u[:, cols]
        wsum = ug
        for dback in range(1, win):
            wsum = wsum + hist_ref[POOL_BUF - dback, :, cols]
        pooled = wsum / float(win) - ug
        y = _dot(pooled.astype(BF16), w_ref[gi].astype(BF16)) * sc_ref[:, cols]
        o_ref[:, cols] = y.astype(BF16)


def _pool_sample(p, n_p, n_s, pw, hist_t, w_pool, pool_scale):
    gw = pw // len(POOL_WINDOWS)
    full = lambda s: pl.BlockSpec(s, lambda i: (0,) * len(s))
    return pl.pallas_call(
        functools.partial(_pool_sample_kernel, gw=gw),
        grid=(1,),
        in_specs=[pl.BlockSpec((n_s, pw), lambda i: (n_p // n_s, 0)),
                  full(hist_t.shape), full(w_pool.shape), full((1, pw))],
        out_specs=full((n_s, pw)),
        out_shape=jax.ShapeDtypeStruct((n_s, pw), BF16),
        compiler_params=_cparams(("arbitrary",)),
        name="pool_sample",
    )(p, hist_t, w_pool, pool_scale.reshape(1, pw))


def _router_kernel(x_ref, g_ref, wr_ref, h_ref, eid_ref, wt_ref):
    x = x_ref[...]
    h = x * lax.rsqrt(jnp.mean(x * x, axis=-1, keepdims=True) + RMS_EPS) * g_ref[...]
    h_ref[...] = h
    logits = _dot(h.astype(BF16), wr_ref[...])
    lane = lax.broadcasted_iota(jnp.int32, logits.shape, 1)
    neg = jnp.float32(-jnp.inf)
    big = jnp.int32(1 << 20)
    is_g = lane < N_GROUPS
    lg = jnp.where(is_g, logits, neg)
    mg = jnp.max(lg, axis=1, keepdims=True)
    g_sel = jnp.min(jnp.where(is_g & (lg == mg), lane, big), axis=1, keepdims=True)
    p_sel = 1.0 / jnp.sum(jnp.where(is_g, jnp.exp(lg - mg), 0.0), axis=1, keepdims=True)
    e_lane = lane - N_GROUPS
    in_grp = (e_lane >= 0) & (e_lane < N_EXPERTS) & ((e_lane >> 3) == g_sel)
    le = jnp.where(in_grp, logits, neg)
    m1 = jnp.max(le, axis=1, keepdims=True)
    i1 = jnp.min(jnp.where(in_grp & (le == m1), lane, big), axis=1, keepdims=True)
    le2 = jnp.where(lane == i1, neg, le)
    m2 = jnp.max(le2, axis=1, keepdims=True)
    i2 = jnp.min(jnp.where(in_grp & (lane != i1) & (le2 == m2), lane, big), axis=1, keepdims=True)
    e2 = jnp.exp(m2 - m1)
    w1 = p_sel / (1.0 + e2)
    w2 = p_sel * e2 / (1.0 + e2)
    eid_ref[...] = jnp.where(lane == 0, i1 - N_GROUPS, jnp.where(lane == 1, i2 - N_GROUPS, 0))
    wt_ref[...] = jnp.where(lane == 0, w1, jnp.where(lane == 1, w2, 0.0))


def _router(x1, g, w_router):
    n, d = x1.shape
    tr = _tile(n, 512)
    return pl.pallas_call(
        _router_kernel,
        grid=(n // tr,),
        in_specs=[pl.BlockSpec((tr, d), lambda i: (i, 0)),
                  pl.BlockSpec((1, d), lambda i: (0, 0)),
                  pl.BlockSpec((d, LANES), lambda i: (0, 0))],
        out_specs=[pl.BlockSpec((tr, d), lambda i: (i, 0)),
                   pl.BlockSpec((tr, LANES), lambda i: (i, 0)),
                   pl.BlockSpec((tr, LANES), lambda i: (i, 0))],
        out_shape=[jax.ShapeDtypeStruct((n, d), F32),
                   jax.ShapeDtypeStruct((n, LANES), jnp.int32),
                   jax.ShapeDtypeStruct((n, LANES), F32)],
        compiler_params=_cparams(("arbitrary",)),
        name="router",
    )(x1, g.reshape(1, d), w_router)


def _moe_kernel(ge_ref, gs_ref, gn_ref, tok_ref, dst_ref,
                h_hbm, wg_ref, wu_ref, wd_ref, out_hbm,
                xf_scr, xb_scr, acc_scr, sem_g, sem_s, *, rows, n_ftiles):
    g = pl.program_id(0)
    f = pl.program_id(1)
    nrows = gn_ref[g]
    start = gs_ref[g]

    def gather_copy(tok, i):
        return pltpu.make_async_copy(h_hbm.at[pl.ds(tok, 1)], xf_scr.at[pl.ds(i, 1)], sem_g)

    def scatter_copy(i, dst):
        return pltpu.make_async_copy(acc_scr.at[pl.ds(i, 1)], out_hbm.at[pl.ds(dst, 1)], sem_s)

    @pl.when(nrows > 0)
    def _group():
        def issue_gather(first):
            def issue(i, carry):
                gather_copy(tok_ref[first + i], i).start()
                return carry
            lax.fori_loop(0, rows, issue, 0, unroll=8)

        @pl.when(f == 0)
        def _rows_in():
            @pl.when(g == 0)
            def _():
                issue_gather(start)
            pltpu.make_async_copy(h_hbm.at[pl.ds(0, rows)], xf_scr, sem_g).wait()
            xb_scr[...] = xf_scr[...].astype(BF16)

        @pl.when((f == 1) & (gn_ref[g + 1] > 0))
        def _prefetch():
            issue_gather(gs_ref[g + 1])

        def ffn_slice(first):
            x = xb_scr[...]
            hg = _dot(x, wg_ref[0].astype(BF16))
            hu = _dot(x, wu_ref[0].astype(BF16))
            act = (hg * _sigmoid(hg) * hu).astype(BF16)
            part = _dot(act, wd_ref[0].astype(BF16))
            if first:
                acc_scr[...] = part
            else:
                acc_scr[...] += part

        @pl.when(f == 0)
        def _():
            ffn_slice(True)

        @pl.when(f > 0)
        def _():
            ffn_slice(False)

        @pl.when(f == n_ftiles - 1)
        def _scatter():
            nblk = nrows // SCATTER_BLOCK

            def issue_block(b, carry):
                for j in range(SCATTER_BLOCK):
                    i = b * SCATTER_BLOCK + j
                    scatter_copy(i, dst_ref[start + i]).start(priority=j % 2)
                return carry
            lax.fori_loop(0, nblk, issue_block, 0)

            def issue_one(i, carry):
                scatter_copy(i, dst_ref[start + i]).start()
                return carry
            lax.fori_loop(nblk * SCATTER_BLOCK, nrows, issue_one, 0)

            def wait_block(b, carry):
                pltpu.make_async_copy(acc_scr.at[pl.ds(0, SCATTER_BLOCK)], out_hbm.at[pl.ds(0, SCATTER_BLOCK)],
                                      sem_s).wait()
                return carry
            lax.fori_loop(0, nblk, wait_block, 0)

            def wait_one(i, carry):
                scatter_copy(0, 0).wait()
                return carry
            lax.fori_loop(nblk * SCATTER_BLOCK, nrows, wait_one, 0)


def _moe(h2, eid, n_p, w_gate, w_up, w_down, rows, tf):
    n, d = h2.shape
    n_s = n - n_p
    d_e = w_gate.shape[2]
    n_assign = n * TOP_K
    n_ftiles = d_e // tf
    assert n_ftiles >= 2
    max_groups = -(-n_assign // rows) + N_EXPERTS
    eflat = eid.reshape(-1)
    order = jnp.argsort(eflat, stable=True).astype(jnp.int32)
    tok_sorted = order // TOP_K
    slot_sorted = order % TOP_K
    dst_sorted = jnp.where(tok_sorted < n_p, slot_sorted * n_p + tok_sorted,
                           TOP_K * n_p + slot_sorted * n_s + (tok_sorted - n_p))
    experts = jnp.arange(N_EXPERTS, dtype=jnp.int32)
    counts = jnp.sum((eflat[:, None] == experts[None, :]).astype(jnp.int32), axis=0)
    starts = jnp.cumsum(counts) - counts
    groups_per_e = (counts + rows - 1) // rows
    g_ends = jnp.cumsum(groups_per_e)
    gidx = jnp.arange(max_groups, dtype=jnp.int32)
    n_groups = g_ends[-1]
    g_e = jnp.minimum(jnp.sum((g_ends[None, :] <= gidx[:, None]).astype(jnp.int32), axis=1), N_EXPERTS - 1)
    local = gidx - (g_ends[g_e] - groups_per_e[g_e])
    g_start = starts[g_e] + local * rows
    g_n = jnp.clip(counts[g_e] - local * rows, 0, rows)
    valid = gidx < n_groups
    last_e = g_e[jnp.maximum(n_groups - 1, 0)]
    g_e = jnp.where(valid, g_e, last_e).astype(jnp.int32)
    one = jnp.zeros((1,), jnp.int32)
    g_start = jnp.concatenate([jnp.where(valid, g_start, 0).astype(jnp.int32), one])
    g_n = jnp.concatenate([jnp.where(valid, g_n, 0).astype(jnp.int32), one])
    pad = jnp.zeros((rows,), jnp.int32)
    tok_sorted = jnp.concatenate([tok_sorted, pad])
    dst_sorted = jnp.concatenate([dst_sorted, pad])

    def fsel(g, f, gn):
        return jnp.where(gn[g] > 0, f, n_ftiles - 1)

    grid_spec = pltpu.PrefetchScalarGridSpec(
        num_scalar_prefetch=5,
        grid=(max_groups, n_ftiles),
        in_specs=[pl.BlockSpec(memory_space=pl.ANY),
                  pl.BlockSpec((1, d, tf), lambda g, f, ge, gs, gn, tk, ds: (ge[g], 0, fsel(g, f, gn))),
                  pl.BlockSpec((1, d, tf), lambda g, f, ge, gs, gn, tk, ds: (ge[g], 0, fsel(g, f, gn))),
                  pl.BlockSpec((1, tf, d), lambda g, f, ge, gs, gn, tk, ds: (ge[g], fsel(g, f, gn), 0))],
        out_specs=pl.BlockSpec(memory_space=pl.ANY),
        scratch_shapes=[pltpu.VMEM((rows, d), F32), pltpu.VMEM((rows, d), BF16), pltpu.VMEM((rows, d), F32),
                        pltpu.SemaphoreType.DMA(()), pltpu.SemaphoreType.DMA(())],
    )
    return pl.pallas_call(
        functools.partial(_moe_kernel, rows=rows, n_ftiles=n_ftiles),
        grid_spec=grid_spec,
        out_shape=jax.ShapeDtypeStruct((TOP_K * n, d), F32),
        compiler_params=_cparams(("arbitrary", "arbitrary")),
        name="moe_experts",
    )(g_e, g_start, g_n, tok_sorted, dst_sorted, h2, w_gate, w_up, w_down)


def _combine_kernel(x_ref, y0_ref, y1_ref, wt_ref, g_ref, o_ref):
    wt = wt_ref[...]
    x = x_ref[...] + (y0_ref[...] * wt[:, 0:1] + y1_ref[...] * wt[:, 1:2])
    o_ref[...] = x * lax.rsqrt(jnp.mean(x * x, axis=-1, keepdims=True) + RMS_EPS) * g_ref[...]


def _combine(x1, y2, wt, g, row0, rows, y_row0):
    d = x1.shape[1]
    tr = _tile(rows, 256)
    assert row0 % tr == 0 and y_row0 % tr == 0
    x0, y0, y1 = row0 // tr, y_row0 // tr, (y_row0 + rows) // tr
    return pl.pallas_call(
        _combine_kernel,
        grid=(rows // tr,),
        in_specs=[pl.BlockSpec((tr, d), lambda i: (x0 + i, 0)),
                  pl.BlockSpec((tr, d), lambda i: (y0 + i, 0)),
                  pl.BlockSpec((tr, d), lambda i: (y1 + i, 0)),
                  pl.BlockSpec((tr, LANES), lambda i: (x0 + i, 0)),
                  pl.BlockSpec((1, d), lambda i: (0, 0))],
        out_specs=pl.BlockSpec((tr, d), lambda i: (i, 0)),
        out_shape=jax.ShapeDtypeStruct((rows, d), F32),
        compiler_params=_cparams(("arbitrary",)),
        name="combine",
    )(x1, y2, y2, wt, g.reshape(1, d))


def _layer(xp, xs, st_wkv, st_shift, st_pool, norm_out_g, norm1_g, w_in, mu_shift, w0, w_decay_up, a0,
           w_iclr_up, w_gate_up, k_k, k_a, r_k, ln_x_g, ln_x_b, w_pool, pool_scale,
           w_branch_a, w_branch_b, w_out, norm2_g, w_router_group, w_router_expert,
           w_exp_gate, w_exp_up, w_exp_down):
    n_b, seq, d = xp.shape
    n_sb = xs.shape[0]
    n_p, n_s = n_b * seq, n_sb * xs.shape[1]
    n = n_p + n_s
    d_a = w_branch_a.shape[0]
    pw = w_branch_b.shape[0]
    sw = 3 * d_a + LORA_W
    assert d_a == pw and d == 2 * d_a and xs.shape[1] == 1 and w_in.shape[1] == sw + pw + 2 * d

    zpad = LORA_PAD - LORA_W
    w_in_t = w_in.T
    head_w = 3 * d_a + LORA_PAD
    pieces = lambda t: ([t[..., i * d_a:(i + 1) * d_a] for i in range(3)],
                        jnp.pad(t[..., 3 * d_a:sw], [(0, 0)] * (t.ndim - 1) + [(0, zpad)]))
    mu_rkv, mu_z = pieces(mu_shift.reshape(1, sw))
    ss_rkv, ss_z = pieces(st_shift)
    vec = lambda t: t.reshape(1, d_a)
    prm = (vec(w0), vec(a0), vec(k_k), vec(k_a), vec(r_k), vec(ln_x_g), vec(ln_x_b))
    lpad = lambda w, r0: jnp.zeros((LORA_PAD, d_a), F32).at[r0:r0 + w.shape[0]].set(w).astype(BF16)
    lora = (lpad(w_decay_up, 0), lpad(w_iclr_up, DECAY_RANK), lpad(w_gate_up, DECAY_RANK + ICLR_RANK))
    lane_head = np.arange(LANES) // HEAD_DIM
    bsum = jnp.asarray(np.tile((lane_head[:, None] == lane_head[None, :]).astype(np.float32), (2, 1)), BF16)

    tm = _tile(n, 1664, 8)
    xp2, xs2 = xp.reshape(n_p, d), xs.reshape(n_s, d)
    h = _norm1(xp2, xs2, norm1_g)
    tn = _tile(d, 256, LANES)
    p_head = _matmul_nt(h, w_in_t, 0, head_w, tm, tn, "in_proj_head")
    p_tail = _matmul_nt(h, w_in_t, sw, pw + 2 * d, tm, tn, "in_proj_tail")

    ya_p, new_wkv_p = _wkv_prompt(p_head, n_b, seq, d_a, mu_rkv, mu_z, prm, lora, bsum)
    ya_s, new_wkv_s = _wkv_sample(p_head, n_p, n_s, d_a, (*ss_rkv, ss_z), jnp.transpose(st_wkv, (1, 2, 3, 0)),
                                  mu_rkv, mu_z, prm, lora, bsum)
    yb_p = _pool_prompt(p_tail, n_b, seq, pw, w_pool, pool_scale)
    yb_s = _pool_sample(p_tail, n_p, n_s, pw, jnp.swapaxes(st_pool, 0, 1), w_pool, pool_scale)

    merged = _merge(ya_p, ya_s, yb_p, yb_s, w_branch_a, w_branch_b, p_tail, pw, d, tm, tn)
    x1 = _outproj(merged, w_out, xp2, xs2, tm, tn)

    w_router = jnp.concatenate([w_router_group, w_router_expert,
                                jnp.zeros((d, LANES - N_GROUPS - N_EXPERTS), F32)], axis=1).astype(BF16)
    h2, eid, wt = _router(x1, norm2_g, w_router)
    d_e = w_exp_gate.shape[2]
    y2 = _moe(h2, eid[:, :TOP_K], n_p, w_exp_gate, w_exp_up, w_exp_down,
              rows=MOE_GROUP_ROWS if n * TOP_K >= 4096 else 64, tf=_tile(d_e, min(256, d_e // 2), LANES))
    y_p = _combine(x1, y2, wt, norm_out_g, 0, n_p, 0)
    y_s = _combine(x1, y2, wt, norm_out_g, n_p, n_s, TOP_K * n_p)

    new_wkv_s = jnp.transpose(new_wkv_s, (3, 0, 1, 2))
    new_shift_p = jnp.stack([p_head[b * seq + seq - 1, :sw] for b in range(n_b)])
    new_shift_s = p_head[n_p:, :sw]
    new_pool_p = jnp.stack([p_tail[b * seq + seq - POOL_BUF:(b + 1) * seq, :pw] for b in range(n_b)])
    new_pool_s = jnp.concatenate([st_pool[:, 1:], p_tail[n_p:, :pw][:, None, :]], axis=1)
    return (y_p.reshape(n_b, seq, d), y_s.reshape(n_sb, 1, d),
            new_wkv_p, new_shift_p, new_pool_p, new_wkv_s, new_shift_s, new_pool_s)


def kernel(x_prompt, x_sample, state_wkv, state_shift, state_pool, norm1_g, w_in, mu_shift, w0, w_decay_up, a0, w_iclr_up, w_gate_up, k_k, k_a, r_k, ln_x_g, ln_x_b, w_pool, pool_scale, w_branch_a, w_branch_b, w_out, norm2_g, w_router_group, w_router_expert, w_exp_gate, w_exp_up, w_exp_down, norm_f_g):
    assert norm1_g.shape[0] == 1, "single-layer trunk"
    outs = _layer(x_prompt, x_sample, state_wkv[0], state_shift[0], state_pool[0], norm_f_g,
                  norm1_g[0], w_in[0], mu_shift[0], w0[0], w_decay_up[0], a0[0], w_iclr_up[0], w_gate_up[0],
                  k_k[0], k_a[0], r_k[0], ln_x_g[0], ln_x_b[0], w_pool[0], pool_scale[0],
                  w_branch_a[0], w_branch_b[0], w_out[0], norm2_g[0], w_router_group[0], w_router_expert[0],
                  w_exp_gate[0], w_exp_up[0], w_exp_down[0])
    y_p, y_s = outs[0], outs[1]
    return (y_p, y_s) + tuple(o[None] for o in outs[2:])
```

```python
import functools
import math

import jax
import jax.numpy as jnp
import numpy as np
from jax import lax
from jax.experimental import pallas as pl
from jax.experimental.pallas import tpu as pltpu

F32 = jnp.float32
BF16 = jnp.bfloat16

HEAD_DIM = 64
LANES = 128
DECAY_RANK = 96
ICLR_RANK = 96
GATE_RANK = 256
LORA_W = DECAY_RANK + ICLR_RANK + GATE_RANK
LORA_PAD = 512
POOL_WINDOWS = (2, 4, 8, 16)
POOL_BUF = 15
POOL_CARRY = 16
N_GROUPS = 4
EXPERTS_PER_GROUP = 8
N_EXPERTS = N_GROUPS * EXPERTS_PER_GROUP
TOP_K = 2
RMS_EPS = 1e-6
GN_EPS = 6.4e-4
L2_EPS = 1e-12
WKV_CHUNK = 64
WKV_PAIR_UNROLL = 8
PREP_PIECES_PER_ROUND = 1
PREP_GROUP_SLABS = 4
SCATTER_BLOCK = 8
MOE_GROUP_ROWS = 576
MXU_COLS = 256
DENSE_ROW_TILE = 1664
STREAM_ROWS = 512
NORM_ROWS = 256
VMEM_LIMIT = 56 * 1024 * 1024


def _cparams(sem, vmem=VMEM_LIMIT):
    return pltpu.CompilerParams(dimension_semantics=sem, vmem_limit_bytes=vmem)


def _tile(n, target, mult=8):
    best = None
    for t in range(mult, min(n, target) + 1, mult):
        if n % t == 0:
            best = t
    return best if best is not None else n


def _dot(a, b):
    return jnp.dot(a, b, preferred_element_type=F32)


def _sigmoid(x):
    return 0.5 * jnp.tanh(0.5 * x) + 0.5


def _norm1_kernel(xp_ref, xs_ref, g_ref, h_ref, *, n_prompt_tiles):
    i = pl.program_id(0)

    def body(x):
        xn = x * lax.rsqrt(jnp.mean(x * x, axis=-1, keepdims=True) + RMS_EPS)
        h_ref[...] = (xn * g_ref[...]).astype(BF16)

    @pl.when(i < n_prompt_tiles)
    def _():
        body(xp_ref[...])

    @pl.when(i >= n_prompt_tiles)
    def _():
        body(xs_ref[...])


def _norm1(xp, xs, g):
    n_p, d = xp.shape
    n_s = xs.shape[0]
    tr = _tile(n_s, NORM_ROWS)
    assert n_p % tr == 0
    npt, nst = n_p // tr, n_s // tr
    return pl.pallas_call(
        functools.partial(_norm1_kernel, n_prompt_tiles=npt),
        grid=(npt + nst,),
        in_specs=[pl.BlockSpec((tr, d), lambda i: (jnp.minimum(i, npt - 1), 0)),
                  pl.BlockSpec((tr, d), lambda i: (jnp.maximum(i - npt, 0), 0)),
                  pl.BlockSpec((1, d), lambda i: (0, 0))],
        out_specs=pl.BlockSpec((tr, d), lambda i: (i, 0)),
        out_shape=jax.ShapeDtypeStruct((n_p + n_s, d), BF16),
        compiler_params=_cparams(("arbitrary",)),
        name="norm1",
    )(xp, xs, g.reshape(1, d))


def _mm_nt_kernel(a_ref, wt_ref, o_ref):
    o_ref[...] = lax.dot_general(a_ref[...], wt_ref[...].astype(BF16), _NT, preferred_element_type=F32)


def _matmul_nt(a, wt, row0, n_out, tm, tn, name):
    m, k = a.shape
    assert row0 % 8 == 0
    return pl.pallas_call(
        _mm_nt_kernel,
        grid=(m // tm, n_out // tn),
        in_specs=[pl.BlockSpec((tm, k), lambda i, j: (i, 0)),
                  pl.BlockSpec((pl.Element(tn), pl.Element(k)), lambda i, j: (pl.multiple_of(row0 + j * tn, 8), 0))],
        out_specs=pl.BlockSpec((tm, tn), lambda i, j: (i, j)),
        out_shape=jax.ShapeDtypeStruct((m, n_out), F32),
        compiler_params=_cparams(("arbitrary", "arbitrary")),
        name=name,
    )(a, wt)


def _merge_kernel(yap_ref, yas_ref, ybp_ref, ybs_ref, wa_ref, wb_ref, ga_ref, gb_ref, o_ref,
                  *, n_prompt_tiles, n_prompt_tail):
    i = pl.program_id(0)
    wa = wa_ref[...].astype(BF16)
    wb = wb_ref[...].astype(BF16)

    def gated(ya, yb, rows):
        a = _dot(ya, wa)
        b = _dot(yb, wb)
        o_ref[rows, :] = (_sigmoid(ga_ref[rows, :]) * a + _sigmoid(gb_ref[rows, :]) * b).astype(BF16)

    @pl.when(i < n_prompt_tiles)
    def _():
        gated(yap_ref[...], ybp_ref[...], slice(None))

    @pl.when(i == n_prompt_tiles)
    def _():
        if n_prompt_tail:
            gated(yap_ref[:n_prompt_tail, :], ybp_ref[:n_prompt_tail, :], slice(0, n_prompt_tail))
        gated(yas_ref[...], ybs_ref[...], slice(n_prompt_tail, None))


def _merge(ya_p, ya_s, yb_p, yb_s, wa, wb, p, gate_col, d, tm, tn):
    n_p, k = ya_p.shape
    n_s = ya_s.shape[0]
    m = n_p + n_s
    full, tail = divmod(n_p, tm)
    assert tail + n_s == tm and tail % 16 == 0
    ga0 = gate_col // tn
    gb0 = (gate_col + d) // tn
    prompt_rows = pl.BlockSpec((tm, k), lambda i, j: (jnp.minimum(i, (n_p - 1) // tm), 0))
    sample_rows = pl.BlockSpec((n_s, k), lambda i, j: (0, 0))
    return pl.pallas_call(
        functools.partial(_merge_kernel, n_prompt_tiles=full, n_prompt_tail=tail),
        grid=(m // tm, d // tn),
        in_specs=[prompt_rows, sample_rows, prompt_rows, sample_rows,
                  pl.BlockSpec((k, tn), lambda i, j: (0, j)),
                  pl.BlockSpec((k, tn), lambda i, j: (0, j)),
                  pl.BlockSpec((tm, tn), lambda i, j: (i, ga0 + j)),
                  pl.BlockSpec((tm, tn), lambda i, j: (i, gb0 + j))],
        out_specs=pl.BlockSpec((tm, tn), lambda i, j: (i, j)),
        out_shape=jax.ShapeDtypeStruct((m, d), BF16),
        compiler_params=_cparams(("arbitrary", "arbitrary")),
        name="merge",
    )(ya_p, ya_s, yb_p, yb_s, wa, wb, p, p)


def _outproj_kernel(m_ref, w_ref, xp_ref, xs_ref, o_ref, *, n_prompt_tiles, n_prompt_tail):
    i = pl.program_id(0)
    y = _dot(m_ref[...], w_ref[...].astype(BF16))

    @pl.when(i < n_prompt_tiles)
    def _():
        o_ref[...] = xp_ref[...] + y

    @pl.when(i == n_prompt_tiles)
    def _():
        if n_prompt_tail:
            o_ref[:n_prompt_tail, :] = xp_ref[:n_prompt_tail, :] + y[:n_prompt_tail]
        o_ref[n_prompt_tail:, :] = xs_ref[...] + y[n_prompt_tail:]


def _outproj(merged, w, xp, xs, tm, tn):
    m, k = merged.shape
    n = w.shape[1]
    n_p, n_s = xp.shape[0], xs.shape[0]
    full, tail = divmod(n_p, tm)
    assert m == n_p + n_s and tail + n_s == tm and tail % 8 == 0
    return pl.pallas_call(
        functools.partial(_outproj_kernel, n_prompt_tiles=full, n_prompt_tail=tail),
        grid=(m // tm, n // tn),
        in_specs=[pl.BlockSpec((tm, k), lambda i, j: (i, 0)),
                  pl.BlockSpec((k, tn), lambda i, j: (0, j)),
                  pl.BlockSpec((tm, tn), lambda i, j: (jnp.minimum(i, (n_p - 1) // tm), j)),
                  pl.BlockSpec((n_s, tn), lambda i, j: (0, j))],
        out_specs=pl.BlockSpec((tm, tn), lambda i, j: (i, j)),
        out_shape=jax.ShapeDtypeStruct((m, n), F32),
        compiler_params=_cparams(("arbitrary", "arbitrary")),
        name="out_proj",
    )(merged, w, xp, xs)


def _split2(x):
    hi = x.astype(BF16)
    lo = (x - hi.astype(F32)).astype(BF16)
    return hi, lo


def _split3(x):
    hi = x.astype(BF16)
    r1 = x - hi.astype(F32)
    mid = r1.astype(BF16)
    lo = (r1 - mid.astype(F32)).astype(BF16)
    return hi, mid, lo


def _dot_bf16(a, b, dims=(((1,), (0,)), ((), ()))):
    return lax.dot_general(a.astype(BF16), b.astype(BF16), dims, preferred_element_type=F32)


_NT = (((1,), (1,)), ((), ()))
_TN = (((0,), (0,)), ((), ()))


def _seg_sum(x, bsum_ref):
    rows = x.shape[0]
    n_slabs = x.shape[1] // LANES
    xs = jnp.concatenate([x[:, q * LANES:(q + 1) * LANES] for q in range(n_slabs)], axis=0)
    hi, lo = _split2(xs)
    s = _dot(jnp.concatenate([hi, lo], axis=1), bsum_ref[...])
    return jnp.concatenate([s[q * rows:(q + 1) * rows] for q in range(n_slabs)], axis=1)


def _rwkv_prep(r, k, v, z, w0, a0, k_k, k_a, r_k, wd, wa, wg, bsum_ref):
    def lora_up(act, w_ref, first_row, n_rows):
        lo = first_row // LANES * LANES
        hi = -(-(first_row + n_rows) // LANES) * LANES
        return _dot(act(z[:, lo:hi]).astype(BF16), w_ref[lo:hi, :])

    lora_w = lora_up(jnp.tanh, wd, 0, DECAY_RANK)
    lora_a = lora_up(lambda t: t, wa, DECAY_RANK, ICLR_RANK)
    gate = lora_up(_sigmoid, wg, DECAY_RANK + ICLR_RANK, GATE_RANK)
    log_decay = -_sigmoid(w0 + lora_w) * math.exp(-0.5)
    a = _sigmoid(a0 + lora_a)
    kk = k * k_k
    kk = kk * jnp.minimum(lax.rsqrt(_seg_sum(kk * kk, bsum_ref)), 1.0 / L2_EPS)
    k2 = k * (1.0 + (a - 1.0) * k_a)
    bonus = _seg_sum(r * k2 * r_k, bsum_ref) * v
    return log_decay, a, gate, kk, k2, bonus


def _rwkv_finish(y, bonus, gate, ln_g, ln_b, bsum_ref):
    inv = 1.0 / HEAD_DIM
    mu = _seg_sum(y, bsum_ref) * inv
    yc = y - mu
    var = _seg_sum(yc * yc, bsum_ref) * inv
    yn = yc * lax.rsqrt(var + GN_EPS) * ln_g + ln_b
    return (yn + bonus) * gate


def _wkv_prompt_kernel(pr_ref, pk_ref, pv_ref, pz_ref, nr_ref, nk_ref, nv_ref, nz_ref,
                       mur_ref, muk_ref, muv_ref, muz_ref,
                       w0_ref, a0_ref, kk_ref, ka_ref, rk_ref, lng_ref, lnb_ref,
                       wd_ref, wa_ref, wg_ref, bsum_ref, tri_ref,
                       ya_ref, sout_ref,
                       s_scr, cr_scr, ck_scr, cv_scr, cz_scr,
                       at_scr, bt_scr, kt_scr, rt_scr, bh_scr, kh_scr, v_scr, gc_scr, bonus_scr, gate_scr, y_scr,
                       *, n_pairs, pair_unroll):
    c = pl.program_id(1)
    n_chunks = pl.num_programs(1)
    C = WKV_CHUNK
    slot = c % 2
    group = min(PREP_GROUP_SLABS, n_pairs)

    def shift(p_ref, carry, mu_ref, cols):
        p = p_ref[:, cols]
        prev = pltpu.roll(p, 1, axis=0)
        row = lax.broadcasted_iota(jnp.int32, p.shape, 0)
        prev = jnp.where(row == 0, carry[0:1, cols], prev)
        carry[0:1, cols] = p[C - 1:C, :]
        return p + (prev - p) * mu_ref[:, cols]

    def prepare(refs, dst):
        r_ref, k_ref, v_ref, z_ref = refs
        z = shift(z_ref, cz_scr, muz_ref, slice(None))

        def lora_up(act, w_ref, first_row, n_rows):
            lo = first_row // LANES * LANES
            hi = -(-(first_row + n_rows) // LANES) * LANES
            return _dot(act(z[:, lo:hi]).astype(BF16), w_ref[lo:hi, :])

        lora_w = lora_up(jnp.tanh, wd_ref, 0, DECAY_RANK)
        yield
        lora_a = lora_up(lambda t: t, wa_ref, DECAY_RANK, ICLR_RANK)
        yield
        gate_scr[dst] = lora_up(_sigmoid, wg_ref, DECAY_RANK + ICLR_RANK, GATE_RANK)
        yield
        for q0 in range(0, n_pairs, group):
            cols = slice(q0 * LANES, (q0 + group) * LANES)
            r = shift(r_ref, cr_scr, mur_ref, cols)
            k = shift(k_ref, ck_scr, muk_ref, cols)
            v = shift(v_ref, cv_scr, muv_ref, cols)
            yield
            lw = -_sigmoid(w0_ref[:, cols] + lora_w[:, cols]) * math.exp(-0.5)
            a = _sigmoid(a0_ref[:, cols] + lora_a[:, cols])
            kk = k * kk_ref[:, cols]
            yield
            kk = kk * jnp.minimum(lax.rsqrt(_seg_sum(kk * kk, bsum_ref)), 1.0 / L2_EPS)
            yield
            k2 = k * (1.0 + (a - 1.0) * ka_ref[:, cols])
            bonus_scr[dst, :, cols] = _seg_sum(r * k2 * rk_ref[:, cols], bsum_ref) * v
            yield
            cl = _dot(tri_ref[...], jnp.concatenate(_split3(lw), axis=0))
            yield
            cl_end = cl[C - 1:C, :]
            beta = kk * a
            e_neg = jnp.exp(-cl)
            e_hat = jnp.exp(cl_end - cl)
            outs = ((at_scr, -kk * jnp.exp(cl - lw)), (bt_scr, beta * e_neg), (kt_scr, k2 * e_neg),
                    (rt_scr, r * jnp.exp(cl)), (bh_scr, beta * e_hat), (kh_scr, k2 * e_hat), (v_scr, v),
                    (gc_scr, jnp.broadcast_to(jnp.exp(cl_end), (2 * C, cl.shape[1]))))
            for scr, x in outs:
                for q in range(group):
                    scr[dst, q0 + q] = x[:, q * LANES:(q + 1) * LANES]
                yield

    @pl.when(c == 0)
    def _first_chunk():
        s_scr[...] = jnp.zeros_like(s_scr)
        cr_scr[...] = jnp.zeros_like(cr_scr)
        ck_scr[...] = jnp.zeros_like(ck_scr)
        cv_scr[...] = jnp.zeros_like(cv_scr)
        cz_scr[...] = jnp.zeros_like(cz_scr)
        for _ in prepare((pr_ref, pk_ref, pv_ref, pz_ref), 0):
            pass

    lane = lax.broadcasted_iota(jnp.int32, (C, LANES), 1)
    first = lane < HEAD_DIM
    row2 = lax.broadcasted_iota(jnp.int32, (2 * C, 2 * C), 0)
    col2 = lax.broadcasted_iota(jnp.int32, (2 * C, 2 * C), 1)
    cbits = C.bit_length() - 1
    same = (row2 >> cbits) == (col2 >> cbits)
    tri_strict = same & ((row2 & (C - 1)) > (col2 & (C - 1)))
    tri_incl = same & ((row2 & (C - 1)) >= (col2 & (C - 1)))
    eye = row2 == col2

    def stack(x):
        return jnp.concatenate([jnp.where(first, x, 0.0), jnp.where(first, 0.0, x)], axis=0)

    def load_pair(p):
        return (at_scr[slot, p], bt_scr[slot, p], kt_scr[slot, p], rt_scr[slot, p], v_scr[slot, p],
                bh_scr[slot, p], kh_scr[slot, p], gc_scr[slot, p], s_scr[p])

    def compute_pair(vals):
        a_s, b_s, k_s, r_s, v_s, bh_s, kh_s = (stack(x) for x in vals[:7])
        gc, s_t = vals[7], vals[8]
        bk = jnp.concatenate([b_s, k_s], axis=0)
        ga = _dot_bf16(a_s, bk, _NT)
        yield
        gr = lax.dot_general(r_s.astype(BF16), bk.astype(BF16), _NT, preferred_element_type=F32)
        yield
        l_ba = jnp.where(tri_strict, ga[:, :2 * C], 0.0)
        l_ka = jnp.where(tri_strict, ga[:, 2 * C:], 0.0)
        m_br = jnp.where(tri_incl, gr[:, :2 * C], 0.0)
        m_kr = jnp.where(tri_incl, gr[:, 2 * C:], 0.0)
        lkv = _dot_bf16(l_ka, v_s)
        yield
        pw = _dot_bf16(l_ba, l_ba)
        yield
        t_inv = jnp.where(eye, 1.0, l_ba)
        steps = (C - 1).bit_length() - 1
        for step in range(1, steps + 1):
            if step < steps:
                both = _dot_bf16(pw, jnp.concatenate([pw, t_inv], axis=1))
                pw, t_inv = both[:, :2 * C], t_inv + both[:, 2 * C:]
            else:
                t_inv = t_inv + _dot_bf16(pw, t_inv)
            yield
        tx = _dot_bf16(t_inv, jnp.concatenate([a_s, lkv], axis=1))
        yield
        a_hat, u_hat = tx[:, :LANES], tx[:, LANES:]
        u = _dot_bf16(a_hat, s_t) + u_hat
        yield
        uv = jnp.concatenate([u, v_s], axis=0)
        ys = _dot(jnp.concatenate([r_s, m_br, m_kr], axis=1).astype(BF16),
                  jnp.concatenate([s_t, uv], axis=0).astype(BF16))
        yield
        s_new = s_t * gc.T + _dot_bf16(jnp.concatenate([bh_s, kh_s], axis=0), uv, _TN)
        return ys[:C] + ys[C:], s_new

    ahead = prepare((nr_ref, nk_ref, nv_ref, nz_ref), 1 - slot)
    for first_pair in range(0, n_pairs, pair_unroll):
        pairs = range(first_pair, first_pair + pair_unroll)
        gens = [compute_pair(load_pair(p)) for p in pairs]
        results = {}
        while len(results) < len(gens):
            for j, gen in enumerate(gens):
                if j not in results:
                    try:
                        next(gen)
                    except StopIteration as done:
                        results[j] = done.value
            for _ in range(PREP_PIECES_PER_ROUND):
                next(ahead, None)
        for j, p in enumerate(pairs):
            y_scr[p], s_scr[p] = results[j]
    for _ in ahead:
        pass

    y = jnp.concatenate([y_scr[q] for q in range(n_pairs)], axis=1)
    out = _rwkv_finish(y, bonus_scr[slot], gate_scr[slot], lng_ref[...], lnb_ref[...], bsum_ref)
    ya_ref[...] = out.astype(BF16)

    @pl.when(c == n_chunks - 1)
    def _store_state():
        for q in range(n_pairs):
            t = s_scr[q].T
            sout_ref[0, 2 * q] = t[:HEAD_DIM, :HEAD_DIM]
            sout_ref[0, 2 * q + 1] = t[HEAD_DIM:, HEAD_DIM:]


def _wkv_prompt(p, n_b, seq, d_a, mu_rkv, mu_z, prm, lora, bsum):
    C = WKV_CHUNK
    n_chunks = seq // C
    n_pairs = d_a // LANES
    zb = (3 * d_a) // LORA_PAD
    row = lambda b, c: b * n_chunks + c
    nxt = lambda b, c: b * n_chunks + jnp.minimum(c + 1, n_chunks - 1)
    vec = lambda w: pl.BlockSpec((1, w), lambda b, c: (0, 0))
    full = lambda s: pl.BlockSpec(s, lambda b, c: (0,) * len(s))
    tri = jnp.asarray(np.tile(np.tril(np.ones((C, C), np.float32)), (1, 3)), BF16)

    def chunk_specs(rows):
        return [pl.BlockSpec((C, d_a), lambda b, c: (rows(b, c), 0)),
                pl.BlockSpec((C, d_a), lambda b, c: (rows(b, c), 1)),
                pl.BlockSpec((C, d_a), lambda b, c: (rows(b, c), 2)),
                pl.BlockSpec((C, LORA_PAD), lambda b, c: (rows(b, c), zb))]

    in_specs = chunk_specs(row) + chunk_specs(nxt) + [vec(d_a), vec(d_a), vec(d_a), vec(LORA_PAD)] + [
        vec(d_a)] * 7 + [full((LORA_PAD, d_a))] * 3 + [full((2 * LANES, LANES)), full((C, 3 * C))]
    scr = [pltpu.VMEM((n_pairs, LANES, LANES), F32),
           pltpu.VMEM((8, d_a), F32), pltpu.VMEM((8, d_a), F32), pltpu.VMEM((8, d_a), F32),
           pltpu.VMEM((8, LORA_PAD), F32)] + [pltpu.VMEM((2, n_pairs, C, LANES), F32)] * 7 + [
           pltpu.VMEM((2, n_pairs, 2 * C, LANES), F32), pltpu.VMEM((2, C, d_a), F32),
           pltpu.VMEM((2, C, d_a), F32), pltpu.VMEM((n_pairs, C, LANES), F32)]
    pair_unroll = math.gcd(n_pairs, WKV_PAIR_UNROLL)
    ya, s_out = pl.pallas_call(
        functools.partial(_wkv_prompt_kernel, n_pairs=n_pairs, pair_unroll=pair_unroll),
        grid=(n_b, n_chunks),
        in_specs=in_specs,
        out_specs=[pl.BlockSpec((C, d_a), lambda b, c: (row(b, c), 0)),
                   pl.BlockSpec((1, 2 * n_pairs, HEAD_DIM, HEAD_DIM), lambda b, c: (b, 0, 0, 0))],
        out_shape=[jax.ShapeDtypeStruct((n_b * seq, d_a), BF16),
                   jax.ShapeDtypeStruct((n_b, 2 * n_pairs, HEAD_DIM, HEAD_DIM), F32)],
        scratch_shapes=scr,
        compiler_params=_cparams(("arbitrary", "arbitrary")),
        name="wkv_prompt",
    )(p, p, p, p, p, p, p, p, mu_rkv[0], mu_rkv[1], mu_rkv[2], mu_z, *prm, *lora, bsum, tri)
    return ya, s_out


def _wkv_sample_prep_kernel(pr_ref, pk_ref, pv_ref, pz_ref, sr_ref, sk_ref, sv_ref, sz_ref,
                            mur_ref, muk_ref, muv_ref, muz_ref,
                            w0_ref, a0_ref, kk_ref, ka_ref, rk_ref,
                            wd_ref, wa_ref, wg_ref, bsum_ref,
                            w_out, nkk_out, kka_out, k2_out, r_out, vt_out, bonus_out, gate_out):
    def shift(p_ref, s_ref, mu_ref):
        p = p_ref[...]
        return p + (s_ref[...] - p) * mu_ref[...]

    r = shift(pr_ref, sr_ref, mur_ref)
    k = shift(pk_ref, sk_ref, muk_ref)
    v = shift(pv_ref, sv_ref, muv_ref)
    z = shift(pz_ref, sz_ref, muz_ref)
    lw, a, gate, kk, k2, bonus = _rwkv_prep(
        r, k, v, z, w0_ref[...], a0_ref[...], kk_ref[...], ka_ref[...], rk_ref[...],
        wd_ref, wa_ref, wg_ref, bsum_ref)
    w_out[...] = jnp.exp(lw).T
    nkk_out[...] = (-kk).T
    kka_out[...] = (kk * a).T
    k2_out[...] = k2.T
    r_out[...] = r.T
    vt_out[...] = v.T
    bonus_out[...] = bonus
    gate_out[...] = gate


def _wkv_sample_step_kernel(s_ref, w_ref, nkk_ref, kka_ref, k2_ref, r_ref, vt_ref, snew_ref, yt_ref):
    def body(i, carry):
        s = s_ref[0, i]
        sa = jnp.sum(s * nkk_ref[...], axis=0, keepdims=True)
        s_new = s * w_ref[...] + sa * kka_ref[...] + vt_ref[pl.ds(i, 1), :] * k2_ref[...]
        snew_ref[0, i] = s_new
        yt_ref[pl.ds(i, 1), :] = jnp.sum(s_new * r_ref[...], axis=0, keepdims=True)
        return carry

    lax.fori_loop(0, HEAD_DIM, body, 0, unroll=8)


def _wkv_sample_finish_kernel(yt_ref, bonus_ref, gate_ref, lng_ref, lnb_ref, bsum_ref, ya_ref):
    y = yt_ref[...].T
    out = _rwkv_finish(y, bonus_ref[...], gate_ref[...], lng_ref[...], lnb_ref[...], bsum_ref)
    ya_ref[...] = out.astype(BF16)


def _wkv_sample(p, n_p, n_s, d_a, s_shift, state_hijb, mu_rkv, mu_z, prm, lora, bsum):
    assert n_p % n_s == 0
    rb = n_p // n_s
    zb = (3 * d_a) // LORA_PAD
    n_heads = d_a // HEAD_DIM
    w0, a0, k_k, k_a, r_k, ln_g, ln_b = prm
    full = lambda s: pl.BlockSpec(s, lambda i: (0,) * len(s))
    tok = jax.ShapeDtypeStruct((n_s, d_a), F32)
    outs = pl.pallas_call(
        _wkv_sample_prep_kernel,
        grid=(1,),
        in_specs=[pl.BlockSpec((n_s, d_a), lambda i: (rb, 0)),
                  pl.BlockSpec((n_s, d_a), lambda i: (rb, 1)),
                  pl.BlockSpec((n_s, d_a), lambda i: (rb, 2)),
                  pl.BlockSpec((n_s, LORA_PAD), lambda i: (rb, zb)),
                  full((n_s, d_a)), full((n_s, d_a)), full((n_s, d_a)), full((n_s, LORA_PAD)),
                  full((1, d_a)), full((1, d_a)), full((1, d_a)), full((1, LORA_PAD))]
                 + [full((1, d_a))] * 5 + [full((LORA_PAD, d_a))] * 3
                 + [full((2 * LANES, LANES))],
        out_specs=[full((d_a, n_s))] * 6 + [full((n_s, d_a))] * 2,
        out_shape=[jax.ShapeDtypeStruct((d_a, n_s), F32)] * 6 + [tok] * 2,
        compiler_params=_cparams(("arbitrary",)),
        name="wkv_sample_prep",
    )(p, p, p, p, *s_shift, mu_rkv[0], mu_rkv[1], mu_rkv[2], mu_z,
      w0, a0, k_k, k_a, r_k, *lora, bsum)
    w, nkk, kka, k2, r, vt, bonus, gate = outs
    headspec = pl.BlockSpec((HEAD_DIM, n_s), lambda h: (h, 0))
    stspec = pl.BlockSpec((1, HEAD_DIM, HEAD_DIM, n_s), lambda h: (h, 0, 0, 0))
    s_new, yt = pl.pallas_call(
        _wkv_sample_step_kernel,
        grid=(n_heads,),
        in_specs=[stspec] + [headspec] * 6,
        out_specs=[stspec, headspec],
        out_shape=[jax.ShapeDtypeStruct(state_hijb.shape, F32), jax.ShapeDtypeStruct((d_a, n_s), F32)],
        compiler_params=_cparams(("arbitrary",)),
        name="wkv_sample_step",
    )(state_hijb, w, nkk, kka, k2, r, vt)
    ya = pl.pallas_call(
        _wkv_sample_finish_kernel,
        grid=(1,),
        in_specs=[full((d_a, n_s)), full((n_s, d_a)), full((n_s, d_a)), full((1, d_a)), full((1, d_a)),
                  full((2 * LANES, LANES))],
        out_specs=full((n_s, d_a)),
        out_shape=jax.ShapeDtypeStruct((n_s, d_a), BF16),
        compiler_params=_cparams(("arbitrary",)),
        name="wkv_sample_finish",
    )(yt, bonus, gate, ln_g, ln_b, bsum)
    return ya, s_new


def _pool_prompt_kernel(u_ref, w_ref, sc_ref, o_ref, carry_scr, *, tt, gw):
    t = pl.program_id(1)

    @pl.when(t == 0)
    def _():
        carry_scr[...] = jnp.zeros_like(carry_scr)

    u = u_ref[...]
    pos = t * tt + lax.broadcasted_iota(jnp.int32, (tt, gw), 0)
    for gi, win in enumerate(POOL_WINDOWS):
        cols = slice(gi * gw, (gi + 1) * gw)
        ug = u[:, cols]
        cur = jnp.concatenate([carry_scr[:, cols], ug], axis=0)
        off = 0
        step = 1
        while step < win:
            cur = cur[step:] + cur[:-step]
            off += step
            step *= 2
        wsum = cur[POOL_CARRY - off:POOL_CARRY - off + tt]
        cnt = jnp.minimum(pos + 1, win).astype(F32)
        pooled = wsum / cnt - ug
        y = _dot(pooled.astype(BF16), w_ref[gi].astype(BF16)) * sc_ref[:, cols]
        o_ref[:, cols] = y.astype(BF16)
    carry_scr[...] = u[tt - POOL_CARRY:, :]


def _pool_prompt(p, n_b, seq, pw, w_pool, pool_scale):
    tt = _tile(seq, STREAM_ROWS)
    nt = seq // tt
    gw = pw // len(POOL_WINDOWS)
    return pl.pallas_call(
        functools.partial(_pool_prompt_kernel, tt=tt, gw=gw),
        grid=(n_b, nt),
        in_specs=[pl.BlockSpec((tt, pw), lambda b, t: (b * nt + t, 0)),
                  pl.BlockSpec(w_pool.shape, lambda b, t: (0, 0, 0)),
                  pl.BlockSpec((1, pw), lambda b, t: (0, 0))],
        out_specs=pl.BlockSpec((tt, pw), lambda b, t: (b * nt + t, 0)),
        out_shape=jax.ShapeDtypeStruct((n_b * seq, pw), BF16),
        scratch_shapes=[pltpu.VMEM((POOL_CARRY, pw), F32)],
        compiler_params=_cparams(("arbitrary", "arbitrary")),
        name="pool_prompt",
    )(p, w_pool, pool_scale.reshape(1, pw))


def _pool_sample_kernel(u_ref, hist_ref, w_ref, sc_ref, o_ref, *, gw):
    u = u_ref[...]
    for gi, win in enumerate(POOL_WINDOWS):
        cols = slice(gi * gw, (gi + 1) * gw)
        ug = u[:, cols]
        wsum = ug
        for dback in range(1, win):
            wsum = wsum + hist_ref[POOL_BUF - dback, :, cols]
        pooled = wsum / float(win) - ug
        y = _dot(pooled.astype(BF16), w_ref[gi].astype(BF16)) * sc_ref[:, cols]
        o_ref[:, cols] = y.astype(BF16)


def _pool_sample(p, n_p, n_s, pw, hist_t, w_pool, pool_scale):
    gw = pw // len(POOL_WINDOWS)
    full = lambda s: pl.BlockSpec(s, lambda i: (0,) * len(s))
    return pl.pallas_call(
        functools.partial(_pool_sample_kernel, gw=gw),
        grid=(1,),
        in_specs=[pl.BlockSpec((n_s, pw), lambda i: (n_p // n_s, 0)),
                  full(hist_t.shape), full(w_pool.shape), full((1, pw))],
        out_specs=full((n_s, pw)),
        out_shape=jax.ShapeDtypeStruct((n_s, pw), BF16),
        compiler_params=_cparams(("arbitrary",)),
        name="pool_sample",
    )(p, hist_t, w_pool, pool_scale.reshape(1, pw))


def _router_kernel(x_ref, g_ref, wr_ref, h_ref, eid_ref, wt_ref):
    x = x_ref[...]
    h = x * lax.rsqrt(jnp.mean(x * x, axis=-1, keepdims=True) + RMS_EPS) * g_ref[...]
    h_ref[...] = h
    logits = _dot(h.astype(BF16), wr_ref[...])
    lane = lax.broadcasted_iota(jnp.int32, logits.shape, 1)
    neg = jnp.float32(-jnp.inf)
    big = jnp.int32(1 << 20)
    is_g = lane < N_GROUPS
    lg = jnp.where(is_g, logits, neg)
    mg = jnp.max(lg, axis=1, keepdims=True)
    g_sel = jnp.min(jnp.where(is_g & (lg == mg), lane, big), axis=1, keepdims=True)
    p_sel = 1.0 / jnp.sum(jnp.where(is_g, jnp.exp(lg - mg), 0.0), axis=1, keepdims=True)
    e_lane = lane - N_GROUPS
    in_grp = (e_lane >= 0) & (e_lane < N_EXPERTS) & ((e_lane >> 3) == g_sel)
    le = jnp.where(in_grp, logits, neg)
    m1 = jnp.max(le, axis=1, keepdims=True)
    i1 = jnp.min(jnp.where(in_grp & (le == m1), lane, big), axis=1, keepdims=True)
    le2 = jnp.where(lane == i1, neg, le)
    m2 = jnp.max(le2, axis=1, keepdims=True)
    i2 = jnp.min(jnp.where(in_grp & (lane != i1) & (le2 == m2), lane, big), axis=1, keepdims=True)
    e2 = jnp.exp(m2 - m1)
    w1 = p_sel / (1.0 + e2)
    w2 = p_sel * e2 / (1.0 + e2)
    eid_ref[...] = jnp.where(lane == 0, i1 - N_GROUPS, jnp.where(lane == 1, i2 - N_GROUPS, 0))
    wt_ref[...] = jnp.where(lane == 0, w1, jnp.where(lane == 1, w2, 0.0))


def _router(x1, g, w_router):
    n, d = x1.shape
    tr = _tile(n, STREAM_ROWS)
    return pl.pallas_call(
        _router_kernel,
        grid=(n // tr,),
        in_specs=[pl.BlockSpec((tr, d), lambda i: (i, 0)),
                  pl.BlockSpec((1, d), lambda i: (0, 0)),
                  pl.BlockSpec((d, LANES), lambda i: (0, 0))],
        out_specs=[pl.BlockSpec((tr, d), lambda i: (i, 0)),
                   pl.BlockSpec((tr, LANES), lambda i: (i, 0)),
                   pl.BlockSpec((tr, LANES), lambda i: (i, 0))],
        out_shape=[jax.ShapeDtypeStruct((n, d), F32),
                   jax.ShapeDtypeStruct((n, LANES), jnp.int32),
                   jax.ShapeDtypeStruct((n, LANES), F32)],
        compiler_params=_cparams(("arbitrary",)),
        name="router",
    )(x1, g.reshape(1, d), w_router)


def _moe_kernel(ge_ref, gs_ref, gn_ref, tok_ref, dst_ref,
                h_hbm, wg_ref, wu_ref, wd_ref, out_hbm,
                xf_scr, xb_scr, acc_scr, sem_g, sem_s, *, rows, n_ftiles):
    g = pl.program_id(0)
    f = pl.program_id(1)
    nrows = gn_ref[g]
    start = gs_ref[g]

    def gather_copy(tok, i):
        return pltpu.make_async_copy(h_hbm.at[pl.ds(tok, 1)], xf_scr.at[pl.ds(i, 1)], sem_g)

    def scatter_copy(i, dst):
        return pltpu.make_async_copy(acc_scr.at[pl.ds(i, 1)], out_hbm.at[pl.ds(dst, 1)], sem_s)

    @pl.when(nrows > 0)
    def _group():
        def issue_gather(first):
            def issue(i, carry):
                gather_copy(tok_ref[first + i], i).start()
                return carry
            lax.fori_loop(0, rows, issue, 0, unroll=8)

        @pl.when(f == 0)
        def _rows_in():
            @pl.when(g == 0)
            def _():
                issue_gather(start)
            pltpu.make_async_copy(h_hbm.at[pl.ds(0, rows)], xf_scr, sem_g).wait()
            xb_scr[...] = xf_scr[...].astype(BF16)

        @pl.when((f == 1) & (gn_ref[g + 1] > 0))
        def _prefetch():
            issue_gather(gs_ref[g + 1])

        def ffn_slice(first):
            x = xb_scr[...]
            hg = _dot(x, wg_ref[0].astype(BF16))
            hu = _dot(x, wu_ref[0].astype(BF16))
            act = (hg * _sigmoid(hg) * hu).astype(BF16)
            part = _dot(act, wd_ref[0].astype(BF16))
            if first:
                acc_scr[...] = part
            else:
                acc_scr[...] += part

        @pl.when(f == 0)
        def _():
            ffn_slice(True)

        @pl.when(f > 0)
        def _():
            ffn_slice(False)

        @pl.when(f == n_ftiles - 1)
        def _scatter():
            nblk = nrows // SCATTER_BLOCK

            def issue_block(b, carry):
                for j in range(SCATTER_BLOCK):
                    i = b * SCATTER_BLOCK + j
                    scatter_copy(i, dst_ref[start + i]).start(priority=j % 2)
                return carry
            lax.fori_loop(0, nblk, issue_block, 0)

            def issue_one(i, carry):
                scatter_copy(i, dst_ref[start + i]).start()
                return carry
            lax.fori_loop(nblk * SCATTER_BLOCK, nrows, issue_one, 0)

            def wait_block(b, carry):
                pltpu.make_async_copy(acc_scr.at[pl.ds(0, SCATTER_BLOCK)], out_hbm.at[pl.ds(0, SCATTER_BLOCK)],
                                      sem_s).wait()
                return carry
            lax.fori_loop(0, nblk, wait_block, 0)

            def wait_one(i, carry):
                scatter_copy(0, 0).wait()
                return carry
            lax.fori_loop(nblk * SCATTER_BLOCK, nrows, wait_one, 0)


def _moe(h2, eid, n_p, w_gate, w_up, w_down, rows, tf):
    n, d = h2.shape
    n_s = n - n_p
    d_e = w_gate.shape[2]
    n_assign = n * TOP_K
    n_ftiles = d_e // tf
    assert n_ftiles >= 2
    max_groups = -(-n_assign // rows) + N_EXPERTS
    eflat = eid.reshape(-1)
    order = jnp.argsort(eflat, stable=True).astype(jnp.int32)
    tok_sorted = order // TOP_K
    slot_sorted = order % TOP_K
    dst_sorted = jnp.where(tok_sorted < n_p, slot_sorted * n_p + tok_sorted,
                           TOP_K * n_p + slot_sorted * n_s + (tok_sorted - n_p))
    experts = jnp.arange(N_EXPERTS, dtype=jnp.int32)
    counts = jnp.sum((eflat[:, None] == experts[None, :]).astype(jnp.int32), axis=0)
    starts = jnp.cumsum(counts) - counts
    groups_per_e = (counts + rows - 1) // rows
    g_ends = jnp.cumsum(groups_per_e)
    gidx = jnp.arange(max_groups, dtype=jnp.int32)
    n_groups = g_ends[-1]
    g_e = jnp.minimum(jnp.sum((g_ends[None, :] <= gidx[:, None]).astype(jnp.int32), axis=1), N_EXPERTS - 1)
    local = gidx - (g_ends[g_e] - groups_per_e[g_e])
    g_start = starts[g_e] + local * rows
    g_n = jnp.clip(counts[g_e] - local * rows, 0, rows)
    valid = gidx < n_groups
    last_e = g_e[jnp.maximum(n_groups - 1, 0)]
    g_e = jnp.where(valid, g_e, last_e).astype(jnp.int32)
    one = jnp.zeros((1,), jnp.int32)
    g_start = jnp.concatenate([jnp.where(valid, g_start, 0).astype(jnp.int32), one])
    g_n = jnp.concatenate([jnp.where(valid, g_n, 0).astype(jnp.int32), one])
    pad = jnp.zeros((rows,), jnp.int32)
    tok_sorted = jnp.concatenate([tok_sorted, pad])
    dst_sorted = jnp.concatenate([dst_sorted, pad])

    def fsel(g, f, gn):
        return jnp.where(gn[g] > 0, f, n_ftiles - 1)

    grid_spec = pltpu.PrefetchScalarGridSpec(
        num_scalar_prefetch=5,
        grid=(max_groups, n_ftiles),
        in_specs=[pl.BlockSpec(memory_space=pl.ANY),
                  pl.BlockSpec((1, d, tf), lambda g, f, ge, gs, gn, tk, ds: (ge[g], 0, fsel(g, f, gn))),
                  pl.BlockSpec((1, d, tf), lambda g, f, ge, gs, gn, tk, ds: (ge[g], 0, fsel(g, f, gn))),
                  pl.BlockSpec((1, tf, d), lambda g, f, ge, gs, gn, tk, ds: (ge[g], fsel(g, f, gn), 0))],
        out_specs=pl.BlockSpec(memory_space=pl.ANY),
        scratch_shapes=[pltpu.VMEM((rows, d), F32), pltpu.VMEM((rows, d), BF16), pltpu.VMEM((rows, d), F32),
                        pltpu.SemaphoreType.DMA(()), pltpu.SemaphoreType.DMA(())],
    )
    return pl.pallas_call(
        functools.partial(_moe_kernel, rows=rows, n_ftiles=n_ftiles),
        grid_spec=grid_spec,
        out_shape=jax.ShapeDtypeStruct((TOP_K * n, d), F32),
        compiler_params=_cparams(("arbitrary", "arbitrary")),
        name="moe_experts",
    )(g_e, g_start, g_n, tok_sorted, dst_sorted, h2, w_gate, w_up, w_down)


def _combine_kernel(x_ref, y0_ref, y1_ref, wt_ref, g_ref, o_ref):
    wt = wt_ref[...]
    x = x_ref[...] + (y0_ref[...] * wt[:, 0:1] + y1_ref[...] * wt[:, 1:2])
    o_ref[...] = x * lax.rsqrt(jnp.mean(x * x, axis=-1, keepdims=True) + RMS_EPS) * g_ref[...]


def _combine(x1, y2, wt, g, row0, rows, y_row0):
    d = x1.shape[1]
    tr = _tile(rows, NORM_ROWS)
    assert row0 % tr == 0 and y_row0 % tr == 0
    x0, y0, y1 = row0 // tr, y_row0 // tr, (y_row0 + rows) // tr
    return pl.pallas_call(
        _combine_kernel,
        grid=(rows // tr,),
        in_specs=[pl.BlockSpec((tr, d), lambda i: (x0 + i, 0)),
                  pl.BlockSpec((tr, d), lambda i: (y0 + i, 0)),
                  pl.BlockSpec((tr, d), lambda i: (y1 + i, 0)),
                  pl.BlockSpec((tr, LANES), lambda i: (x0 + i, 0)),
                  pl.BlockSpec((1, d), lambda i: (0, 0))],
        out_specs=pl.BlockSpec((tr, d), lambda i: (i, 0)),
        out_shape=jax.ShapeDtypeStruct((rows, d), F32),
        compiler_params=_cparams(("arbitrary",)),
        name="combine",
    )(x1, y2, y2, wt, g.reshape(1, d))


def _layer(xp, xs, st_wkv, st_shift, st_pool, norm_out_g, norm1_g, w_in, mu_shift, w0, w_decay_up, a0,
           w_iclr_up, w_gate_up, k_k, k_a, r_k, ln_x_g, ln_x_b, w_pool, pool_scale,
           w_branch_a, w_branch_b, w_out, norm2_g, w_router_group, w_router_expert,
           w_exp_gate, w_exp_up, w_exp_down):
    n_b, seq, d = xp.shape
    n_sb = xs.shape[0]
    n_p, n_s = n_b * seq, n_sb * xs.shape[1]
    n = n_p + n_s
    d_a = w_branch_a.shape[0]
    pw = w_branch_b.shape[0]
    sw = 3 * d_a + LORA_W
    assert d_a == pw and d == 2 * d_a and xs.shape[1] == 1 and w_in.shape[1] == sw + pw + 2 * d

    zpad = LORA_PAD - LORA_W
    w_in_t = w_in.T
    head_w = 3 * d_a + LORA_PAD
    pieces = lambda t: ([t[..., i * d_a:(i + 1) * d_a] for i in range(3)],
                        jnp.pad(t[..., 3 * d_a:sw], [(0, 0)] * (t.ndim - 1) + [(0, zpad)]))
    mu_rkv, mu_z = pieces(mu_shift.reshape(1, sw))
    ss_rkv, ss_z = pieces(st_shift)
    vec = lambda t: t.reshape(1, d_a)
    prm = (vec(w0), vec(a0), vec(k_k), vec(k_a), vec(r_k), vec(ln_x_g), vec(ln_x_b))
    lpad = lambda w, r0: jnp.zeros((LORA_PAD, d_a), F32).at[r0:r0 + w.shape[0]].set(w).astype(BF16)
    lora = (lpad(w_decay_up, 0), lpad(w_iclr_up, DECAY_RANK), lpad(w_gate_up, DECAY_RANK + ICLR_RANK))
    lane_head = np.arange(LANES) // HEAD_DIM
    bsum = jnp.asarray(np.tile((lane_head[:, None] == lane_head[None, :]).astype(np.float32), (2, 1)), BF16)

    tm = _tile(n, DENSE_ROW_TILE, 8)
    xp2, xs2 = xp.reshape(n_p, d), xs.reshape(n_s, d)
    h = _norm1(xp2, xs2, norm1_g)
    tn = _tile(d, MXU_COLS, LANES)
    p_head = _matmul_nt(h, w_in_t, 0, head_w, tm, tn, "in_proj_head")
    p_tail = _matmul_nt(h, w_in_t, sw, pw + 2 * d, tm, tn, "in_proj_tail")

    ya_p, new_wkv_p = _wkv_prompt(p_head, n_b, seq, d_a, mu_rkv, mu_z, prm, lora, bsum)
    ya_s, new_wkv_s = _wkv_sample(p_head, n_p, n_s, d_a, (*ss_rkv, ss_z), jnp.transpose(st_wkv, (1, 2, 3, 0)),
                                  mu_rkv, mu_z, prm, lora, bsum)
    yb_p = _pool_prompt(p_tail, n_b, seq, pw, w_pool, pool_scale)
    yb_s = _pool_sample(p_tail, n_p, n_s, pw, jnp.swapaxes(st_pool, 0, 1), w_pool, pool_scale)

    merged = _merge(ya_p, ya_s, yb_p, yb_s, w_branch_a, w_branch_b, p_tail, pw, d, tm, tn)
    x1 = _outproj(merged, w_out, xp2, xs2, tm, tn)

    w_router = jnp.concatenate([w_router_group, w_router_expert,
                                jnp.zeros((d, LANES - N_GROUPS - N_EXPERTS), F32)], axis=1).astype(BF16)
    h2, eid, wt = _router(x1, norm2_g, w_router)
    d_e = w_exp_gate.shape[2]
    y2 = _moe(h2, eid[:, :TOP_K], n_p, w_exp_gate, w_exp_up, w_exp_down,
              rows=min(MOE_GROUP_ROWS, 2 * (n * TOP_K // N_EXPERTS // SCATTER_BLOCK) * SCATTER_BLOCK),
              tf=_tile(d_e, min(MXU_COLS, d_e // 2), LANES))
    y_p = _combine(x1, y2, wt, norm_out_g, 0, n_p, 0)
    y_s = _combine(x1, y2, wt, norm_out_g, n_p, n_s, TOP_K * n_p)

    new_wkv_s = jnp.transpose(new_wkv_s, (3, 0, 1, 2))
    new_shift_p = jnp.stack([p_head[b * seq + seq - 1, :sw] for b in range(n_b)])
    new_shift_s = p_head[n_p:, :sw]
    new_pool_p = jnp.stack([p_tail[b * seq + seq - POOL_BUF:(b + 1) * seq, :pw] for b in range(n_b)])
    new_pool_s = jnp.concatenate([st_pool[:, 1:], p_tail[n_p:, :pw][:, None, :]], axis=1)
    return (y_p.reshape(n_b, seq, d), y_s.reshape(n_sb, 1, d),
            new_wkv_p, new_shift_p, new_pool_p, new_wkv_s, new_shift_s, new_pool_s)


def kernel(x_prompt, x_sample, state_wkv, state_shift, state_pool, norm1_g, w_in, mu_shift, w0, w_decay_up, a0, w_iclr_up, w_gate_up, k_k, k_a, r_k, ln_x_g, ln_x_b, w_pool, pool_scale, w_branch_a, w_branch_b, w_out, norm2_g, w_router_group, w_router_expert, w_exp_gate, w_exp_up, w_exp_down, norm_f_g):
    assert norm1_g.shape[0] == 1, "single-layer trunk"
    outs = _layer(x_prompt, x_sample, state_wkv[0], state_shift[0], state_pool[0], norm_f_g,
                  norm1_g[0], w_in[0], mu_shift[0], w0[0], w_decay_up[0], a0[0], w_iclr_up[0], w_gate_up[0],
                  k_k[0], k_a[0], r_k[0], ln_x_g[0], ln_x_b[0], w_pool[0], pool_scale[0],
                  w_branch_a[0], w_branch_b[0], w_out[0], norm2_g[0], w_router_group[0], w_router_expert[0],
                  w_exp_gate[0], w_exp_up[0], w_exp_down[0])
    y_p, y_s = outs[0], outs[1]
    return (y_p, y_s) + tuple(o[None] for o in outs[2:])
```

```python
import functools
import math

import jax
import jax.numpy as jnp
import numpy as np
from jax import lax
from jax.experimental import pallas as pl
from jax.experimental.pallas import tpu as pltpu

F32 = jnp.float32
BF16 = jnp.bfloat16

HEAD_DIM = 64
LANES = 128
DECAY_RANK = 96
ICLR_RANK = 96
GATE_RANK = 256
LORA_W = DECAY_RANK + ICLR_RANK + GATE_RANK
LORA_PAD = 512
POOL_WINDOWS = (2, 4, 8, 16)
POOL_BUF = 15
POOL_CARRY = 16
N_GROUPS = 4
EXPERTS_PER_GROUP = 8
N_EXPERTS = N_GROUPS * EXPERTS_PER_GROUP
TOP_K = 2
RMS_EPS = 1e-6
GN_EPS = 6.4e-4
L2_EPS = 1e-12
WKV_CHUNK = 64
WKV_PAIR_UNROLL = 8
PREP_PIECES_PER_ROUND = 1
PREP_GROUP_SLABS = 4
SCATTER_BLOCK = 8
MOE_GROUP_ROWS = 576
MXU_COLS = 256
DENSE_ROW_TILE = 1664
STREAM_ROWS = 512
NORM_ROWS = 256
VMEM_LIMIT = 56 * 1024 * 1024


def _cparams(sem, vmem=VMEM_LIMIT):
    return pltpu.CompilerParams(dimension_semantics=sem, vmem_limit_bytes=vmem)


def _tile(n, target, mult=8):
    best = None
    for t in range(mult, min(n, target) + 1, mult):
        if n % t == 0:
            best = t
    return best if best is not None else n


def _dot(a, b):
    return jnp.dot(a, b, preferred_element_type=F32)


def _sigmoid(x):
    return 0.5 * jnp.tanh(0.5 * x) + 0.5


def _norm1_kernel(xp_ref, xs_ref, g_ref, h_ref, *, n_prompt_tiles):
    i = pl.program_id(0)

    def body(x):
        xn = x * lax.rsqrt(jnp.mean(x * x, axis=-1, keepdims=True) + RMS_EPS)
        h_ref[...] = (xn * g_ref[...]).astype(BF16)

    @pl.when(i < n_prompt_tiles)
    def _():
        body(xp_ref[...])

    @pl.when(i >= n_prompt_tiles)
    def _():
        body(xs_ref[...])


def _norm1(xp, xs, g):
    n_p, d = xp.shape
    n_s = xs.shape[0]
    tr = _tile(n_s, NORM_ROWS)
    assert n_p % tr == 0
    npt, nst = n_p // tr, n_s // tr
    return pl.pallas_call(
        functools.partial(_norm1_kernel, n_prompt_tiles=npt),
        grid=(npt + nst,),
        in_specs=[pl.BlockSpec((tr, d), lambda i: (jnp.minimum(i, npt - 1), 0)),
                  pl.BlockSpec((tr, d), lambda i: (jnp.maximum(i - npt, 0), 0)),
                  pl.BlockSpec((1, d), lambda i: (0, 0))],
        out_specs=pl.BlockSpec((tr, d), lambda i: (i, 0)),
        out_shape=jax.ShapeDtypeStruct((n_p + n_s, d), BF16),
        compiler_params=_cparams(("arbitrary",)),
        name="norm1",
    )(xp, xs, g.reshape(1, d))


def _mm_nt_kernel(a_ref, wt_ref, o_ref):
    o_ref[...] = lax.dot_general(a_ref[...], wt_ref[...].astype(BF16), _NT, preferred_element_type=F32)


def _matmul_nt(a, wt, row0, n_out, tm, tn, name):
    m, k = a.shape
    assert row0 % 8 == 0
    return pl.pallas_call(
        _mm_nt_kernel,
        grid=(m // tm, n_out // tn),
        in_specs=[pl.BlockSpec((tm, k), lambda i, j: (i, 0)),
                  pl.BlockSpec((pl.Element(tn), pl.Element(k)), lambda i, j: (pl.multiple_of(row0 + j * tn, 8), 0))],
        out_specs=pl.BlockSpec((tm, tn), lambda i, j: (i, j)),
        out_shape=jax.ShapeDtypeStruct((m, n_out), F32),
        compiler_params=_cparams(("arbitrary", "arbitrary")),
        name=name,
    )(a, wt)


def _merge_kernel(yap_ref, yas_ref, ybp_ref, ybs_ref, wa_ref, wb_ref, ga_ref, gb_ref, o_ref,
                  *, n_prompt_tiles, n_prompt_tail):
    i = pl.program_id(0)
    wa = wa_ref[...].astype(BF16)
    wb = wb_ref[...].astype(BF16)

    def gated(ya, yb, rows):
        a = _dot(ya, wa)
        b = _dot(yb, wb)
        o_ref[rows, :] = (_sigmoid(ga_ref[rows, :]) * a + _sigmoid(gb_ref[rows, :]) * b).astype(BF16)

    @pl.when(i < n_prompt_tiles)
    def _():
        gated(yap_ref[...], ybp_ref[...], slice(None))

    @pl.when(i == n_prompt_tiles)
    def _():
        if n_prompt_tail:
            gated(yap_ref[:n_prompt_tail, :], ybp_ref[:n_prompt_tail, :], slice(0, n_prompt_tail))
        gated(yas_ref[...], ybs_ref[...], slice(n_prompt_tail, None))


def _merge(ya_p, ya_s, yb_p, yb_s, wa, wb, p, gate_col, d, tm, tn):
    n_p, k = ya_p.shape
    n_s = ya_s.shape[0]
    m = n_p + n_s
    full, tail = divmod(n_p, tm)
    assert tail + n_s == tm and tail % 16 == 0
    ga0 = gate_col // tn
    gb0 = (gate_col + d) // tn
    prompt_rows = pl.BlockSpec((tm, k), lambda i, j: (jnp.minimum(i, (n_p - 1) // tm), 0))
    sample_rows = pl.BlockSpec((n_s, k), lambda i, j: (0, 0))
    return pl.pallas_call(
        functools.partial(_merge_kernel, n_prompt_tiles=full, n_prompt_tail=tail),
        grid=(m // tm, d // tn),
        in_specs=[prompt_rows, sample_rows, prompt_rows, sample_rows,
                  pl.BlockSpec((k, tn), lambda i, j: (0, j)),
                  pl.BlockSpec((k, tn), lambda i, j: (0, j)),
                  pl.BlockSpec((tm, tn), lambda i, j: (i, ga0 + j)),
                  pl.BlockSpec((tm, tn), lambda i, j: (i, gb0 + j))],
        out_specs=pl.BlockSpec((tm, tn), lambda i, j: (i, j)),
        out_shape=jax.ShapeDtypeStruct((m, d), BF16),
        compiler_params=_cparams(("arbitrary", "arbitrary")),
        name="merge",
    )(ya_p, ya_s, yb_p, yb_s, wa, wb, p, p)


def _outproj_kernel(m_ref, w_ref, xp_ref, xs_ref, o_ref, *, n_prompt_tiles, n_prompt_tail):
    i = pl.program_id(0)
    y = _dot(m_ref[...], w_ref[...].astype(BF16))

    @pl.when(i < n_prompt_tiles)
    def _():
        o_ref[...] = xp_ref[...] + y

    @pl.when(i == n_prompt_tiles)
    def _():
        if n_prompt_tail:
            o_ref[:n_prompt_tail, :] = xp_ref[:n_prompt_tail, :] + y[:n_prompt_tail]
        o_ref[n_prompt_tail:, :] = xs_ref[...] + y[n_prompt_tail:]


def _outproj(merged, w, xp, xs, tm, tn):
    m, k = merged.shape
    n = w.shape[1]
    n_p, n_s = xp.shape[0], xs.shape[0]
    full, tail = divmod(n_p, tm)
    assert m == n_p + n_s and tail + n_s == tm and tail % 8 == 0
    return pl.pallas_call(
        functools.partial(_outproj_kernel, n_prompt_tiles=full, n_prompt_tail=tail),
        grid=(m // tm, n // tn),
        in_specs=[pl.BlockSpec((tm, k), lambda i, j: (i, 0)),
                  pl.BlockSpec((k, tn), lambda i, j: (0, j)),
                  pl.BlockSpec((tm, tn), lambda i, j: (jnp.minimum(i, (n_p - 1) // tm), j)),
                  pl.BlockSpec((n_s, tn), lambda i, j: (0, j))],
        out_specs=pl.BlockSpec((tm, tn), lambda i, j: (i, j)),
        out_shape=jax.ShapeDtypeStruct((m, n), F32),
        compiler_params=_cparams(("arbitrary", "arbitrary")),
        name="out_proj",
    )(merged, w, xp, xs)


def _split2(x):
    hi = x.astype(BF16)
    lo = (x - hi.astype(F32)).astype(BF16)
    return hi, lo


def _split3(x):
    hi = x.astype(BF16)
    r1 = x - hi.astype(F32)
    mid = r1.astype(BF16)
    lo = (r1 - mid.astype(F32)).astype(BF16)
    return hi, mid, lo


def _dot_bf16(a, b, dims=(((1,), (0,)), ((), ()))):
    return lax.dot_general(a.astype(BF16), b.astype(BF16), dims, preferred_element_type=F32)


_NT = (((1,), (1,)), ((), ()))
_TN = (((0,), (0,)), ((), ()))


def _seg_sum(x, bsum_ref):
    rows = x.shape[0]
    n_slabs = x.shape[1] // LANES
    xs = jnp.concatenate([x[:, q * LANES:(q + 1) * LANES] for q in range(n_slabs)], axis=0)
    hi, lo = _split2(xs)
    s = _dot(jnp.concatenate([hi, lo], axis=1), bsum_ref[...])
    return jnp.concatenate([s[q * rows:(q + 1) * rows] for q in range(n_slabs)], axis=1)


def _rwkv_prep(r, k, v, z, w0, a0, k_k, k_a, r_k, wd, wa, wg, bsum_ref):
    def lora_up(act, w_ref, first_row, n_rows):
        lo = first_row // LANES * LANES
        hi = -(-(first_row + n_rows) // LANES) * LANES
        return _dot(act(z[:, lo:hi]).astype(BF16), w_ref[lo:hi, :])

    lora_w = lora_up(jnp.tanh, wd, 0, DECAY_RANK)
    lora_a = lora_up(lambda t: t, wa, DECAY_RANK, ICLR_RANK)
    gate = lora_up(_sigmoid, wg, DECAY_RANK + ICLR_RANK, GATE_RANK)
    log_decay = -_sigmoid(w0 + lora_w) * math.exp(-0.5)
    a = _sigmoid(a0 + lora_a)
    kk = k * k_k
    kk = kk * jnp.minimum(lax.rsqrt(_seg_sum(kk * kk, bsum_ref)), 1.0 / L2_EPS)
    k2 = k * (1.0 + (a - 1.0) * k_a)
    bonus = _seg_sum(r * k2 * r_k, bsum_ref) * v
    return log_decay, a, gate, kk, k2, bonus


def _rwkv_finish(y, bonus, gate, ln_g, ln_b, bsum_ref):
    inv = 1.0 / HEAD_DIM
    mu = _seg_sum(y, bsum_ref) * inv
    yc = y - mu
    var = _seg_sum(yc * yc, bsum_ref) * inv
    yn = yc * lax.rsqrt(var + GN_EPS) * ln_g + ln_b
    return (yn + bonus) * gate


def _wkv_prompt_kernel(pr_ref, pk_ref, pv_ref, pz_ref, nr_ref, nk_ref, nv_ref, nz_ref,
                       mur_ref, muk_ref, muv_ref, muz_ref,
                       w0_ref, a0_ref, kk_ref, ka_ref, rk_ref, lng_ref, lnb_ref,
                       wd_ref, wa_ref, wg_ref, bsum_ref, tri_ref,
                       ya_ref, sout_ref,
                       s_scr, cr_scr, ck_scr, cv_scr, cz_scr,
                       at_scr, bt_scr, kt_scr, rt_scr, bh_scr, kh_scr, v_scr, gc_scr, bonus_scr, gate_scr, y_scr,
                       *, n_pairs, pair_unroll):
    c = pl.program_id(1)
    n_chunks = pl.num_programs(1)
    C = WKV_CHUNK
    slot = c % 2
    group = min(PREP_GROUP_SLABS, n_pairs)

    def shift(p_ref, carry, mu_ref, cols):
        p = p_ref[:, cols]
        prev = pltpu.roll(p, 1, axis=0)
        row = lax.broadcasted_iota(jnp.int32, p.shape, 0)
        prev = jnp.where(row == 0, carry[0:1, cols], prev)
        carry[0:1, cols] = p[C - 1:C, :]
        return p + (prev - p) * mu_ref[:, cols]

    def prepare(refs, dst):
        r_ref, k_ref, v_ref, z_ref = refs
        z = shift(z_ref, cz_scr, muz_ref, slice(None))

        def lora_up(act, w_ref, first_row, n_rows):
            lo = first_row // LANES * LANES
            hi = -(-(first_row + n_rows) // LANES) * LANES
            return _dot(act(z[:, lo:hi]).astype(BF16), w_ref[lo:hi, :])

        lora_w = lora_up(jnp.tanh, wd_ref, 0, DECAY_RANK)
        yield
        lora_a = lora_up(lambda t: t, wa_ref, DECAY_RANK, ICLR_RANK)
        yield
        gate_scr[dst] = lora_up(_sigmoid, wg_ref, DECAY_RANK + ICLR_RANK, GATE_RANK)
        yield
        for q0 in range(0, n_pairs, group):
            cols = slice(q0 * LANES, (q0 + group) * LANES)
            r = shift(r_ref, cr_scr, mur_ref, cols)
            k = shift(k_ref, ck_scr, muk_ref, cols)
            v = shift(v_ref, cv_scr, muv_ref, cols)
            yield
            lw = -_sigmoid(w0_ref[:, cols] + lora_w[:, cols]) * math.exp(-0.5)
            a = _sigmoid(a0_ref[:, cols] + lora_a[:, cols])
            kk = k * kk_ref[:, cols]
            yield
            kk = kk * jnp.minimum(lax.rsqrt(_seg_sum(kk * kk, bsum_ref)), 1.0 / L2_EPS)
            yield
            k2 = k * (1.0 + (a - 1.0) * ka_ref[:, cols])
            bonus_scr[dst, :, cols] = _seg_sum(r * k2 * rk_ref[:, cols], bsum_ref) * v
            yield
            cl = _dot(tri_ref[...], jnp.concatenate(_split3(lw), axis=0))
            yield
            cl_end = cl[C - 1:C, :]
            beta = kk * a
            e_neg = jnp.exp(-cl)
            e_hat = jnp.exp(cl_end - cl)
            outs = ((at_scr, -kk * jnp.exp(cl - lw)), (bt_scr, beta * e_neg), (kt_scr, k2 * e_neg),
                    (rt_scr, r * jnp.exp(cl)), (bh_scr, beta * e_hat), (kh_scr, k2 * e_hat), (v_scr, v),
                    (gc_scr, jnp.broadcast_to(jnp.exp(cl_end), (2 * C, cl.shape[1]))))
            for scr, x in outs:
                for q in range(group):
                    scr[dst, q0 + q] = x[:, q * LANES:(q + 1) * LANES]
                yield

    @pl.when(c == 0)
    def _first_chunk():
        s_scr[...] = jnp.zeros_like(s_scr)
        cr_scr[...] = jnp.zeros_like(cr_scr)
        ck_scr[...] = jnp.zeros_like(ck_scr)
        cv_scr[...] = jnp.zeros_like(cv_scr)
        cz_scr[...] = jnp.zeros_like(cz_scr)
        for _ in prepare((pr_ref, pk_ref, pv_ref, pz_ref), 0):
            pass

    lane = lax.broadcasted_iota(jnp.int32, (C, LANES), 1)
    first = lane < HEAD_DIM
    row2 = lax.broadcasted_iota(jnp.int32, (2 * C, 2 * C), 0)
    col2 = lax.broadcasted_iota(jnp.int32, (2 * C, 2 * C), 1)
    cbits = C.bit_length() - 1
    same = (row2 >> cbits) == (col2 >> cbits)
    tri_strict = same & ((row2 & (C - 1)) > (col2 & (C - 1)))
    tri_incl = same & ((row2 & (C - 1)) >= (col2 & (C - 1)))
    eye = row2 == col2

    def stack(x):
        return jnp.concatenate([jnp.where(first, x, 0.0), jnp.where(first, 0.0, x)], axis=0)

    def load_pair(p):
        return (at_scr[slot, p], bt_scr[slot, p], kt_scr[slot, p], rt_scr[slot, p], v_scr[slot, p],
                bh_scr[slot, p], kh_scr[slot, p], gc_scr[slot, p], s_scr[p])

    def compute_pair(vals):
        a_s, b_s, k_s, r_s, v_s, bh_s, kh_s = (stack(x) for x in vals[:7])
        gc, s_t = vals[7], vals[8]
        bk = jnp.concatenate([b_s, k_s], axis=0)
        ga = _dot_bf16(a_s, bk, _NT)
        yield
        gr = lax.dot_general(r_s.astype(BF16), bk.astype(BF16), _NT, preferred_element_type=F32)
        yield
        l_ba = jnp.where(tri_strict, ga[:, :2 * C], 0.0)
        l_ka = jnp.where(tri_strict, ga[:, 2 * C:], 0.0)
        m_br = jnp.where(tri_incl, gr[:, :2 * C], 0.0)
        m_kr = jnp.where(tri_incl, gr[:, 2 * C:], 0.0)
        lkv = _dot_bf16(l_ka, v_s)
        yield
        pw = _dot_bf16(l_ba, l_ba)
        yield
        t_inv = jnp.where(eye, 1.0, l_ba)
        steps = (C - 1).bit_length() - 1
        for step in range(1, steps + 1):
            if step < steps:
                both = _dot_bf16(pw, jnp.concatenate([pw, t_inv], axis=1))
                pw, t_inv = both[:, :2 * C], t_inv + both[:, 2 * C:]
            else:
                t_inv = t_inv + _dot_bf16(pw, t_inv)
            yield
        tx = _dot_bf16(t_inv, jnp.concatenate([a_s, lkv], axis=1))
        yield
        a_hat, u_hat = tx[:, :LANES], tx[:, LANES:]
        u = _dot_bf16(a_hat, s_t) + u_hat
        yield
        uv = jnp.concatenate([u, v_s], axis=0)
        ys = _dot(jnp.concatenate([r_s, m_br, m_kr], axis=1).astype(BF16),
                  jnp.concatenate([s_t, uv], axis=0).astype(BF16))
        yield
        s_new = s_t * gc.T + _dot_bf16(jnp.concatenate([bh_s, kh_s], axis=0), uv, _TN)
        return ys[:C] + ys[C:], s_new

    ahead = prepare((nr_ref, nk_ref, nv_ref, nz_ref), 1 - slot)
    for first_pair in range(0, n_pairs, pair_unroll):
        pairs = range(first_pair, first_pair + pair_unroll)
        gens = [compute_pair(load_pair(p)) for p in pairs]
        results = {}
        while len(results) < len(gens):
            for j, gen in enumerate(gens):
                if j not in results:
                    try:
                        next(gen)
                    except StopIteration as done:
                        results[j] = done.value
            for _ in range(PREP_PIECES_PER_ROUND):
                next(ahead, None)
        for j, p in enumerate(pairs):
            y_scr[p], s_scr[p] = results[j]
    for _ in ahead:
        pass

    y = jnp.concatenate([y_scr[q] for q in range(n_pairs)], axis=1)
    out = _rwkv_finish(y, bonus_scr[slot], gate_scr[slot], lng_ref[...], lnb_ref[...], bsum_ref)
    ya_ref[...] = out.astype(BF16)

    @pl.when(c == n_chunks - 1)
    def _store_state():
        for q in range(n_pairs):
            t = s_scr[q].T
            sout_ref[0, 2 * q] = t[:HEAD_DIM, :HEAD_DIM]
            sout_ref[0, 2 * q + 1] = t[HEAD_DIM:, HEAD_DIM:]


def _wkv_prompt(p, n_b, seq, d_a, mu_rkv, mu_z, prm, lora, bsum):
    C = WKV_CHUNK
    n_chunks = seq // C
    n_pairs = d_a // LANES
    zb = (3 * d_a) // LORA_PAD
    row = lambda b, c: b * n_chunks + c
    nxt = lambda b, c: b * n_chunks + jnp.minimum(c + 1, n_chunks - 1)
    vec = lambda w: pl.BlockSpec((1, w), lambda b, c: (0, 0))
    full = lambda s: pl.BlockSpec(s, lambda b, c: (0,) * len(s))
    tri = jnp.asarray(np.tile(np.tril(np.ones((C, C), np.float32)), (1, 3)), BF16)

    def chunk_specs(rows):
        return [pl.BlockSpec((C, d_a), lambda b, c: (rows(b, c), 0)),
                pl.BlockSpec((C, d_a), lambda b, c: (rows(b, c), 1)),
                pl.BlockSpec((C, d_a), lambda b, c: (rows(b, c), 2)),
                pl.BlockSpec((C, LORA_PAD), lambda b, c: (rows(b, c), zb))]

    in_specs = chunk_specs(row) + chunk_specs(nxt) + [vec(d_a), vec(d_a), vec(d_a), vec(LORA_PAD)] + [
        vec(d_a)] * 7 + [full((LORA_PAD, d_a))] * 3 + [full((2 * LANES, LANES)), full((C, 3 * C))]
    scr = [pltpu.VMEM((n_pairs, LANES, LANES), F32),
           pltpu.VMEM((8, d_a), F32), pltpu.VMEM((8, d_a), F32), pltpu.VMEM((8, d_a), F32),
           pltpu.VMEM((8, LORA_PAD), F32)] + [pltpu.VMEM((2, n_pairs, C, LANES), F32)] * 7 + [
           pltpu.VMEM((2, n_pairs, 2 * C, LANES), F32), pltpu.VMEM((2, C, d_a), F32),
           pltpu.VMEM((2, C, d_a), F32), pltpu.VMEM((n_pairs, C, LANES), F32)]
    pair_unroll = math.gcd(n_pairs, WKV_PAIR_UNROLL)
    ya, s_out = pl.pallas_call(
        functools.partial(_wkv_prompt_kernel, n_pairs=n_pairs, pair_unroll=pair_unroll),
        grid=(n_b, n_chunks),
        in_specs=in_specs,
        out_specs=[pl.BlockSpec((C, d_a), lambda b, c: (row(b, c), 0)),
                   pl.BlockSpec((1, 2 * n_pairs, HEAD_DIM, HEAD_DIM), lambda b, c: (b, 0, 0, 0))],
        out_shape=[jax.ShapeDtypeStruct((n_b * seq, d_a), BF16),
                   jax.ShapeDtypeStruct((n_b, 2 * n_pairs, HEAD_DIM, HEAD_DIM), F32)],
        scratch_shapes=scr,
        compiler_params=_cparams(("arbitrary", "arbitrary")),
        name="wkv_prompt",
    )(p, p, p, p, p, p, p, p, mu_rkv[0], mu_rkv[1], mu_rkv[2], mu_z, *prm, *lora, bsum, tri)
    return ya, s_out


def _wkv_sample_prep_kernel(pr_ref, pk_ref, pv_ref, pz_ref, sr_ref, sk_ref, sv_ref, sz_ref,
                            mur_ref, muk_ref, muv_ref, muz_ref,
                            w0_ref, a0_ref, kk_ref, ka_ref, rk_ref,
                            wd_ref, wa_ref, wg_ref, bsum_ref,
                            w_out, nkk_out, kka_out, k2_out, r_out, vt_out, bonus_out, gate_out):
    def shift(p_ref, s_ref, mu_ref):
        p = p_ref[...]
        return p + (s_ref[...] - p) * mu_ref[...]

    r = shift(pr_ref, sr_ref, mur_ref)
    k = shift(pk_ref, sk_ref, muk_ref)
    v = shift(pv_ref, sv_ref, muv_ref)
    z = shift(pz_ref, sz_ref, muz_ref)
    lw, a, gate, kk, k2, bonus = _rwkv_prep(
        r, k, v, z, w0_ref[...], a0_ref[...], kk_ref[...], ka_ref[...], rk_ref[...],
        wd_ref, wa_ref, wg_ref, bsum_ref)
    w_out[...] = jnp.exp(lw).T
    nkk_out[...] = (-kk).T
    kka_out[...] = (kk * a).T
    k2_out[...] = k2.T
    r_out[...] = r.T
    vt_out[...] = v.T
    bonus_out[...] = bonus
    gate_out[...] = gate


def _wkv_sample_step_kernel(s_ref, w_ref, nkk_ref, kka_ref, k2_ref, r_ref, vt_ref, snew_ref, yt_ref):
    def body(i, carry):
        s = s_ref[0, i]
        sa = jnp.sum(s * nkk_ref[...], axis=0, keepdims=True)
        s_new = s * w_ref[...] + sa * kka_ref[...] + vt_ref[pl.ds(i, 1), :] * k2_ref[...]
        snew_ref[0, i] = s_new
        yt_ref[pl.ds(i, 1), :] = jnp.sum(s_new * r_ref[...], axis=0, keepdims=True)
        return carry

    lax.fori_loop(0, HEAD_DIM, body, 0, unroll=8)


def _wkv_sample_finish_kernel(yt_ref, bonus_ref, gate_ref, lng_ref, lnb_ref, bsum_ref, ya_ref):
    y = yt_ref[...].T
    out = _rwkv_finish(y, bonus_ref[...], gate_ref[...], lng_ref[...], lnb_ref[...], bsum_ref)
    ya_ref[...] = out.astype(BF16)


def _wkv_sample(p, n_p, n_s, d_a, s_shift, state_hijb, mu_rkv, mu_z, prm, lora, bsum):
    assert n_p % n_s == 0
    rb = n_p // n_s
    zb = (3 * d_a) // LORA_PAD
    n_heads = d_a // HEAD_DIM
    w0, a0, k_k, k_a, r_k, ln_g, ln_b = prm
    full = lambda s: pl.BlockSpec(s, lambda i: (0,) * len(s))
    tok = jax.ShapeDtypeStruct((n_s, d_a), F32)
    outs = pl.pallas_call(
        _wkv_sample_prep_kernel,
        grid=(1,),
        in_specs=[pl.BlockSpec((n_s, d_a), lambda i: (rb, 0)),
                  pl.BlockSpec((n_s, d_a), lambda i: (rb, 1)),
                  pl.BlockSpec((n_s, d_a), lambda i: (rb, 2)),
                  pl.BlockSpec((n_s, LORA_PAD), lambda i: (rb, zb)),
                  full((n_s, d_a)), full((n_s, d_a)), full((n_s, d_a)), full((n_s, LORA_PAD)),
                  full((1, d_a)), full((1, d_a)), full((1, d_a)), full((1, LORA_PAD))]
                 + [full((1, d_a))] * 5 + [full((LORA_PAD, d_a))] * 3
                 + [full((2 * LANES, LANES))],
        out_specs=[full((d_a, n_s))] * 6 + [full((n_s, d_a))] * 2,
        out_shape=[jax.ShapeDtypeStruct((d_a, n_s), F32)] * 6 + [tok] * 2,
        compiler_params=_cparams(("arbitrary",)),
        name="wkv_sample_prep",
    )(p, p, p, p, *s_shift, mu_rkv[0], mu_rkv[1], mu_rkv[2], mu_z,
      w0, a0, k_k, k_a, r_k, *lora, bsum)
    w, nkk, kka, k2, r, vt, bonus, gate = outs
    headspec = pl.BlockSpec((HEAD_DIM, n_s), lambda h: (h, 0))
    stspec = pl.BlockSpec((1, HEAD_DIM, HEAD_DIM, n_s), lambda h: (h, 0, 0, 0))
    s_new, yt = pl.pallas_call(
        _wkv_sample_step_kernel,
        grid=(n_heads,),
        in_specs=[stspec] + [headspec] * 6,
        out_specs=[stspec, headspec],
        out_shape=[jax.ShapeDtypeStruct(state_hijb.shape, F32), jax.ShapeDtypeStruct((d_a, n_s), F32)],
        compiler_params=_cparams(("arbitrary",)),
        name="wkv_sample_step",
    )(state_hijb, w, nkk, kka, k2, r, vt)
    ya = pl.pallas_call(
        _wkv_sample_finish_kernel,
        grid=(1,),
        in_specs=[full((d_a, n_s)), full((n_s, d_a)), full((n_s, d_a)), full((1, d_a)), full((1, d_a)),
                  full((2 * LANES, LANES))],
        out_specs=full((n_s, d_a)),
        out_shape=jax.ShapeDtypeStruct((n_s, d_a), BF16),
        compiler_params=_cparams(("arbitrary",)),
        name="wkv_sample_finish",
    )(yt, bonus, gate, ln_g, ln_b, bsum)
    return ya, s_new


def _pool_prompt_kernel(u_ref, w_ref, sc_ref, o_ref, carry_scr, *, tt, gw):
    t = pl.program_id(1)

    @pl.when(t == 0)
    def _():
        carry_scr[...] = jnp.zeros_like(carry_scr)

    u = u_ref[...]
    pos = t * tt + lax.broadcasted_iota(jnp.int32, (tt, gw), 0)
    for gi, win in enumerate(POOL_WINDOWS):
        cols = slice(gi * gw, (gi + 1) * gw)
        ug = u[:, cols]
        cur = jnp.concatenate([carry_scr[:, cols], ug], axis=0)
        off = 0
        step = 1
        while step < win:
            cur = cur[step:] + cur[:-step]
            off += step
            step *= 2
        wsum = cur[POOL_CARRY - off:POOL_CARRY - off + tt]
        cnt = jnp.minimum(pos + 1, win).astype(F32)
        pooled = wsum / cnt - ug
        y = _dot(pooled.astype(BF16), w_ref[gi].astype(BF16)) * sc_ref[:, cols]
        o_ref[:, cols] = y.astype(BF16)
    carry_scr[...] = u[tt - POOL_CARRY:, :]


def _pool_prompt(p, n_b, seq, pw, w_pool, pool_scale):
    tt = _tile(seq, STREAM_ROWS)
    nt = seq // tt
    gw = pw // len(POOL_WINDOWS)
    return pl.pallas_call(
        functools.partial(_pool_prompt_kernel, tt=tt, gw=gw),
        grid=(n_b, nt),
        in_specs=[pl.BlockSpec((tt, pw), lambda b, t: (b * nt + t, 0)),
                  pl.BlockSpec(w_pool.shape, lambda b, t: (0, 0, 0)),
                  pl.BlockSpec((1, pw), lambda b, t: (0, 0))],
        out_specs=pl.BlockSpec((tt, pw), lambda b, t: (b * nt + t, 0)),
        out_shape=jax.ShapeDtypeStruct((n_b * seq, pw), BF16),
        scratch_shapes=[pltpu.VMEM((POOL_CARRY, pw), F32)],
        compiler_params=_cparams(("arbitrary", "arbitrary")),
        name="pool_prompt",
    )(p, w_pool, pool_scale.reshape(1, pw))


def _pool_sample_kernel(u_ref, hist_ref, w_ref, sc_ref, o_ref, *, gw):
    u = u_ref[...]
    for gi, win in enumerate(POOL_WINDOWS):
        cols = slice(gi * gw, (gi + 1) * gw)
        ug = u[:, cols]
        wsum = ug
        for dback in range(1, win):
            wsum = wsum + hist_ref[POOL_BUF - dback, :, cols]
        pooled = wsum / float(win) - ug
        y = _dot(pooled.astype(BF16), w_ref[gi].astype(BF16)) * sc_ref[:, cols]
        o_ref[:, cols] = y.astype(BF16)


def _pool_sample(p, n_p, n_s, pw, hist_t, w_pool, pool_scale):
    gw = pw // len(POOL_WINDOWS)
    full = lambda s: pl.BlockSpec(s, lambda i: (0,) * len(s))
    return pl.pallas_call(
        functools.partial(_pool_sample_kernel, gw=gw),
        grid=(1,),
        in_specs=[pl.BlockSpec((n_s, pw), lambda i: (n_p // n_s, 0)),
                  full(hist_t.shape), full(w_pool.shape), full((1, pw))],
        out_specs=full((n_s, pw)),
        out_shape=jax.ShapeDtypeStruct((n_s, pw), BF16),
        compiler_params=_cparams(("arbitrary",)),
        name="pool_sample",
    )(p, hist_t, w_pool, pool_scale.reshape(1, pw))


def _router_kernel(x_ref, g_ref, wr_ref, h_ref, eid_ref, wt_ref):
    x = x_ref[...]
    h = x * lax.rsqrt(jnp.mean(x * x, axis=-1, keepdims=True) + RMS_EPS) * g_ref[...]
    h_ref[...] = h
    logits = _dot(h.astype(BF16), wr_ref[...])
    lane = lax.broadcasted_iota(jnp.int32, logits.shape, 1)
    neg = jnp.float32(-jnp.inf)
    big = jnp.int32(1 << 20)
    is_g = lane < N_GROUPS
    lg = jnp.where(is_g, logits, neg)
    mg = jnp.max(lg, axis=1, keepdims=True)
    g_sel = jnp.min(jnp.where(is_g & (lg == mg), lane, big), axis=1, keepdims=True)
    p_sel = 1.0 / jnp.sum(jnp.where(is_g, jnp.exp(lg - mg), 0.0), axis=1, keepdims=True)
    e_lane = lane - N_GROUPS
    in_grp = (e_lane >= 0) & (e_lane < N_EXPERTS) & ((e_lane >> 3) == g_sel)
    le = jnp.where(in_grp, logits, neg)
    m1 = jnp.max(le, axis=1, keepdims=True)
    i1 = jnp.min(jnp.where(in_grp & (le == m1), lane, big), axis=1, keepdims=True)
    le2 = jnp.where(lane == i1, neg, le)
    m2 = jnp.max(le2, axis=1, keepdims=True)
    i2 = jnp.min(jnp.where(in_grp & (lane != i1) & (le2 == m2), lane, big), axis=1, keepdims=True)
    e2 = jnp.exp(m2 - m1)
    w1 = p_sel / (1.0 + e2)
    w2 = p_sel * e2 / (1.0 + e2)
    eid_ref[...] = jnp.where(lane == 0, i1 - N_GROUPS, jnp.where(lane == 1, i2 - N_GROUPS, 0))
    wt_ref[...] = jnp.where(lane == 0, w1, jnp.where(lane == 1, w2, 0.0))


def _router(x1, g, w_router):
    n, d = x1.shape
    tr = _tile(n, STREAM_ROWS)
    return pl.pallas_call(
        _router_kernel,
        grid=(n // tr,),
        in_specs=[pl.BlockSpec((tr, d), lambda i: (i, 0)),
                  pl.BlockSpec((1, d), lambda i: (0, 0)),
                  pl.BlockSpec((d, LANES), lambda i: (0, 0))],
        out_specs=[pl.BlockSpec((tr, d), lambda i: (i, 0)),
                   pl.BlockSpec((tr, LANES), lambda i: (i, 0)),
                   pl.BlockSpec((tr, LANES), lambda i: (i, 0))],
        out_shape=[jax.ShapeDtypeStruct((n, d), F32),
                   jax.ShapeDtypeStruct((n, LANES), jnp.int32),
                   jax.ShapeDtypeStruct((n, LANES), F32)],
        compiler_params=_cparams(("arbitrary",)),
        name="router",
    )(x1, g.reshape(1, d), w_router)


def _moe_kernel(ge_ref, gs_ref, gn_ref, tok_ref, dst_ref,
                h_hbm, wg_ref, wu_ref, wd_ref, out_hbm,
                xf_scr, xb_scr, acc_scr, sem_g, sem_s, *, rows, n_ftiles):
    g = pl.program_id(0)
    f = pl.program_id(1)
    nrows = gn_ref[g]
    start = gs_ref[g]

    def gather_copy(tok, i):
        return pltpu.make_async_copy(h_hbm.at[pl.ds(tok, 1)], xf_scr.at[pl.ds(i, 1)], sem_g)

    def scatter_copy(i, dst):
        return pltpu.make_async_copy(acc_scr.at[pl.ds(i, 1)], out_hbm.at[pl.ds(dst, 1)], sem_s)

    @pl.when(nrows > 0)
    def _group():
        def issue_gather(first):
            def issue(i, carry):
                gather_copy(tok_ref[first + i], i).start(priority=1)
                return carry
            lax.fori_loop(0, rows, issue, 0, unroll=8)

        @pl.when(f == 0)
        def _rows_in():
            @pl.when(g == 0)
            def _():
                issue_gather(start)
            pltpu.make_async_copy(h_hbm.at[pl.ds(0, rows)], xf_scr, sem_g).wait()
            xb_scr[...] = xf_scr[...].astype(BF16)

        @pl.when((f == 1) & (gn_ref[g + 1] > 0))
        def _prefetch():
            issue_gather(gs_ref[g + 1])

        def ffn_slice(first):
            x = xb_scr[...]
            hg = _dot(x, wg_ref[0].astype(BF16))
            hu = _dot(x, wu_ref[0].astype(BF16))
            act = (hg * _sigmoid(hg) * hu).astype(BF16)
            part = _dot(act, wd_ref[0].astype(BF16))
            if first:
                acc_scr[...] = part
            else:
                acc_scr[...] += part

        @pl.when(f == 0)
        def _():
            ffn_slice(True)

        @pl.when(f > 0)
        def _():
            ffn_slice(False)

        @pl.when(f == n_ftiles - 1)
        def _scatter():
            nblk = nrows // SCATTER_BLOCK

            def issue_block(b, carry):
                for j in range(SCATTER_BLOCK):
                    i = b * SCATTER_BLOCK + j
                    scatter_copy(i, dst_ref[start + i]).start(priority=j % 2)
                return carry
            lax.fori_loop(0, nblk, issue_block, 0)

            def issue_one(i, carry):
                scatter_copy(i, dst_ref[start + i]).start()
                return carry
            lax.fori_loop(nblk * SCATTER_BLOCK, nrows, issue_one, 0)

            def wait_block(b, carry):
                pltpu.make_async_copy(acc_scr.at[pl.ds(0, SCATTER_BLOCK)], out_hbm.at[pl.ds(0, SCATTER_BLOCK)],
                                      sem_s).wait()
                return carry
            lax.fori_loop(0, nblk, wait_block, 0)

            def wait_one(i, carry):
                scatter_copy(0, 0).wait()
                return carry
            lax.fori_loop(nblk * SCATTER_BLOCK, nrows, wait_one, 0)


def _moe(h2, eid, n_p, w_gate, w_up, w_down, rows, tf):
    n, d = h2.shape
    n_s = n - n_p
    d_e = w_gate.shape[2]
    n_assign = n * TOP_K
    n_ftiles = d_e // tf
    assert n_ftiles >= 2
    max_groups = -(-n_assign // rows) + N_EXPERTS
    eflat = eid.reshape(-1)
    order = jnp.argsort(eflat, stable=True).astype(jnp.int32)
    tok_sorted = order // TOP_K
    slot_sorted = order % TOP_K
    dst_sorted = jnp.where(tok_sorted < n_p, slot_sorted * n_p + tok_sorted,
                           TOP_K * n_p + slot_sorted * n_s + (tok_sorted - n_p))
    experts = jnp.arange(N_EXPERTS, dtype=jnp.int32)
    counts = jnp.sum((eflat[:, None] == experts[None, :]).astype(jnp.int32), axis=0)
    starts = jnp.cumsum(counts) - counts
    groups_per_e = (counts + rows - 1) // rows
    g_ends = jnp.cumsum(groups_per_e)
    gidx = jnp.arange(max_groups, dtype=jnp.int32)
    n_groups = g_ends[-1]
    g_e = jnp.minimum(jnp.sum((g_ends[None, :] <= gidx[:, None]).astype(jnp.int32), axis=1), N_EXPERTS - 1)
    local = gidx - (g_ends[g_e] - groups_per_e[g_e])
    g_start = starts[g_e] + local * rows
    g_n = jnp.clip(counts[g_e] - local * rows, 0, rows)
    valid = gidx < n_groups
    last_e = g_e[jnp.maximum(n_groups - 1, 0)]
    g_e = jnp.where(valid, g_e, last_e).astype(jnp.int32)
    one = jnp.zeros((1,), jnp.int32)
    g_start = jnp.concatenate([jnp.where(valid, g_start, 0).astype(jnp.int32), one])
    g_n = jnp.concatenate([jnp.where(valid, g_n, 0).astype(jnp.int32), one])
    pad = jnp.zeros((rows,), jnp.int32)
    tok_sorted = jnp.concatenate([tok_sorted, pad])
    dst_sorted = jnp.concatenate([dst_sorted, pad])

    def fsel(g, f, gn):
        return jnp.where(gn[g] > 0, f, n_ftiles - 1)

    grid_spec = pltpu.PrefetchScalarGridSpec(
        num_scalar_prefetch=5,
        grid=(max_groups, n_ftiles),
        in_specs=[pl.BlockSpec(memory_space=pl.ANY),
                  pl.BlockSpec((1, d, tf), lambda g, f, ge, gs, gn, tk, ds: (ge[g], 0, fsel(g, f, gn))),
                  pl.BlockSpec((1, d, tf), lambda g, f, ge, gs, gn, tk, ds: (ge[g], 0, fsel(g, f, gn))),
                  pl.BlockSpec((1, tf, d), lambda g, f, ge, gs, gn, tk, ds: (ge[g], fsel(g, f, gn), 0))],
        out_specs=pl.BlockSpec(memory_space=pl.ANY),
        scratch_shapes=[pltpu.VMEM((rows, d), F32), pltpu.VMEM((rows, d), BF16), pltpu.VMEM((rows, d), F32),
                        pltpu.SemaphoreType.DMA(()), pltpu.SemaphoreType.DMA(())],
    )
    return pl.pallas_call(
        functools.partial(_moe_kernel, rows=rows, n_ftiles=n_ftiles),
        grid_spec=grid_spec,
        out_shape=jax.ShapeDtypeStruct((TOP_K * n, d), F32),
        compiler_params=_cparams(("arbitrary", "arbitrary")),
        name="moe_experts",
    )(g_e, g_start, g_n, tok_sorted, dst_sorted, h2, w_gate, w_up, w_down)


def _combine_kernel(x_ref, y0_ref, y1_ref, wt_ref, g_ref, o_ref):
    wt = wt_ref[...]
    x = x_ref[...] + (y0_ref[...] * wt[:, 0:1] + y1_ref[...] * wt[:, 1:2])
    o_ref[...] = x * lax.rsqrt(jnp.mean(x * x, axis=-1, keepdims=True) + RMS_EPS) * g_ref[...]


def _combine(x1, y2, wt, g, row0, rows, y_row0):
    d = x1.shape[1]
    tr = _tile(rows, NORM_ROWS)
    assert row0 % tr == 0 and y_row0 % tr == 0
    x0, y0, y1 = row0 // tr, y_row0 // tr, (y_row0 + rows) // tr
    return pl.pallas_call(
        _combine_kernel,
        grid=(rows // tr,),
        in_specs=[pl.BlockSpec((tr, d), lambda i: (x0 + i, 0)),
                  pl.BlockSpec((tr, d), lambda i: (y0 + i, 0)),
                  pl.BlockSpec((tr, d), lambda i: (y1 + i, 0)),
                  pl.BlockSpec((tr, LANES), lambda i: (x0 + i, 0)),
                  pl.BlockSpec((1, d), lambda i: (0, 0))],
        out_specs=pl.BlockSpec((tr, d), lambda i: (i, 0)),
        out_shape=jax.ShapeDtypeStruct((rows, d), F32),
        compiler_params=_cparams(("arbitrary",)),
        name="combine",
    )(x1, y2, y2, wt, g.reshape(1, d))


def _layer(xp, xs, st_wkv, st_shift, st_pool, norm_out_g, norm1_g, w_in, mu_shift, w0, w_decay_up, a0,
           w_iclr_up, w_gate_up, k_k, k_a, r_k, ln_x_g, ln_x_b, w_pool, pool_scale,
           w_branch_a, w_branch_b, w_out, norm2_g, w_router_group, w_router_expert,
           w_exp_gate, w_exp_up, w_exp_down):
    n_b, seq, d = xp.shape
    n_sb = xs.shape[0]
    n_p, n_s = n_b * seq, n_sb * xs.shape[1]
    n = n_p + n_s
    d_a = w_branch_a.shape[0]
    pw = w_branch_b.shape[0]
    sw = 3 * d_a + LORA_W
    assert d_a == pw and d == 2 * d_a and xs.shape[1] == 1 and w_in.shape[1] == sw + pw + 2 * d

    zpad = LORA_PAD - LORA_W
    w_in_t = w_in.T
    head_w = 3 * d_a + LORA_PAD
    pieces = lambda t: ([t[..., i * d_a:(i + 1) * d_a] for i in range(3)],
                        jnp.pad(t[..., 3 * d_a:sw], [(0, 0)] * (t.ndim - 1) + [(0, zpad)]))
    mu_rkv, mu_z = pieces(mu_shift.reshape(1, sw))
    ss_rkv, ss_z = pieces(st_shift)
    vec = lambda t: t.reshape(1, d_a)
    prm = (vec(w0), vec(a0), vec(k_k), vec(k_a), vec(r_k), vec(ln_x_g), vec(ln_x_b))
    lpad = lambda w, r0: jnp.zeros((LORA_PAD, d_a), F32).at[r0:r0 + w.shape[0]].set(w).astype(BF16)
    lora = (lpad(w_decay_up, 0), lpad(w_iclr_up, DECAY_RANK), lpad(w_gate_up, DECAY_RANK + ICLR_RANK))
    lane_head = np.arange(LANES) // HEAD_DIM
    bsum = jnp.asarray(np.tile((lane_head[:, None] == lane_head[None, :]).astype(np.float32), (2, 1)), BF16)

    tm = _tile(n, DENSE_ROW_TILE, 8)
    xp2, xs2 = xp.reshape(n_p, d), xs.reshape(n_s, d)
    h = _norm1(xp2, xs2, norm1_g)
    tn = _tile(d, MXU_COLS, LANES)
    p_head = _matmul_nt(h, w_in_t, 0, head_w, tm, tn, "in_proj_head")
    p_tail = _matmul_nt(h, w_in_t, sw, pw + 2 * d, tm, tn, "in_proj_tail")

    ya_p, new_wkv_p = _wkv_prompt(p_head, n_b, seq, d_a, mu_rkv, mu_z, prm, lora, bsum)
    ya_s, new_wkv_s = _wkv_sample(p_head, n_p, n_s, d_a, (*ss_rkv, ss_z), jnp.transpose(st_wkv, (1, 2, 3, 0)),
                                  mu_rkv, mu_z, prm, lora, bsum)
    yb_p = _pool_prompt(p_tail, n_b, seq, pw, w_pool, pool_scale)
    yb_s = _pool_sample(p_tail, n_p, n_s, pw, jnp.swapaxes(st_pool, 0, 1), w_pool, pool_scale)

    merged = _merge(ya_p, ya_s, yb_p, yb_s, w_branch_a, w_branch_b, p_tail, pw, d, tm, tn)
    x1 = _outproj(merged, w_out, xp2, xs2, tm, tn)

    w_router = jnp.concatenate([w_router_group, w_router_expert,
                                jnp.zeros((d, LANES - N_GROUPS - N_EXPERTS), F32)], axis=1).astype(BF16)
    h2, eid, wt = _router(x1, norm2_g, w_router)
    d_e = w_exp_gate.shape[2]
    y2 = _moe(h2, eid[:, :TOP_K], n_p, w_exp_gate, w_exp_up, w_exp_down,
              rows=min(MOE_GROUP_ROWS, 2 * (n * TOP_K // N_EXPERTS // SCATTER_BLOCK) * SCATTER_BLOCK),
              tf=_tile(d_e, min(MXU_COLS, d_e // 2), LANES))
    y_p = _combine(x1, y2, wt, norm_out_g, 0, n_p, 0)
    y_s = _combine(x1, y2, wt, norm_out_g, n_p, n_s, TOP_K * n_p)

    new_wkv_s = jnp.transpose(new_wkv_s, (3, 0, 1, 2))
    new_shift_p = jnp.stack([p_head[b * seq + seq - 1, :sw] for b in range(n_b)])
    new_shift_s = p_head[n_p:, :sw]
    new_pool_p = jnp.stack([p_tail[b * seq + seq - POOL_BUF:(b + 1) * seq, :pw] for b in range(n_b)])
    new_pool_s = jnp.concatenate([st_pool[:, 1:], p_tail[n_p:, :pw][:, None, :]], axis=1)
    return (y_p.reshape(n_b, seq, d), y_s.reshape(n_sb, 1, d),
            new_wkv_p, new_shift_p, new_pool_p, new_wkv_s, new_shift_s, new_pool_s)


def kernel(x_prompt, x_sample, state_wkv, state_shift, state_pool, norm1_g, w_in, mu_shift, w0, w_decay_up, a0, w_iclr_up, w_gate_up, k_k, k_a, r_k, ln_x_g, ln_x_b, w_pool, pool_scale, w_branch_a, w_branch_b, w_out, norm2_g, w_router_group, w_router_expert, w_exp_gate, w_exp_up, w_exp_down, norm_f_g):
    assert norm1_g.shape[0] == 1, "single-layer trunk"
    outs = _layer(x_prompt, x_sample, state_wkv[0], state_shift[0], state_pool[0], norm_f_g,
                  norm1_g[0], w_in[0], mu_shift[0], w0[0], w_decay_up[0], a0[0], w_iclr_up[0], w_gate_up[0],
                  k_k[0], k_a[0], r_k[0], ln_x_g[0], ln_x_b[0], w_pool[0], pool_scale[0],
                  w_branch_a[0], w_branch_b[0], w_out[0], norm2_g[0], w_router_group[0], w_router_expert[0],
                  w_exp_gate[0], w_exp_up[0], w_exp_down[0])
    y_p, y_s = outs[0], outs[1]
    return (y_p, y_s) + tuple(o[None] for o in outs[2:])
```
